```python
import jax, jax.numpy as jnp
from jax import lax
import numpy as np

D_MODEL = 1024
BATCH = 8
SEQ = 2048
DEPTH = 1

CHUNK = 64
QUERY_BLOCK = 128
FOX_HEAD_DIM = 64
N_FOX_HEADS = (D_MODEL // 2) // FOX_HEAD_DIM
D_FOX = N_FOX_HEADS * FOX_HEAD_DIM
GDN_HEAD_DIM = 128
N_GDN_HEADS = (D_MODEL // 2) // GDN_HEAD_DIM
D_GDN = N_GDN_HEADS * GDN_HEAD_DIM
D_MIX = D_FOX + D_GDN
CONV_K = 4
D_FF = 4 * D_MODEL
EPS = 1e-6
PROJ_SIZES = (D_FOX, D_FOX, D_FOX, N_FOX_HEADS, D_GDN, D_GDN, D_GDN, N_GDN_HEADS, N_GDN_HEADS, D_GDN)
D_PROJ = sum(PROJ_SIZES)

kernel_name = "fox_gdn_hymba_sandwich_block"


def rms_norm(x, w):
    xf = x.astype(jnp.float32)
    y = xf * lax.rsqrt(jnp.mean(xf * xf, axis=-1, keepdims=True) + EPS)
    return (y * w.astype(jnp.float32)).astype(x.dtype)


def causal_depthwise_conv(x, w):
    k, c = w.shape
    return lax.conv_general_dilated(
        x, w[:, None, :].astype(x.dtype), window_strides=(1,), padding=[(k - 1, 0)],
        dimension_numbers=("NWC", "WIO", "NWC"), feature_group_count=c)


def forgetting_attention(q, k, v, log_f):
    b, s, h, d = q.shape
    scale = d ** -0.5
    cum = jnp.cumsum(log_f, axis=1).transpose(0, 2, 1)
    outs = []
    for i in range(s // QUERY_BLOCK):
        qs, qe = i * QUERY_BLOCK, (i + 1) * QUERY_BLOCK
        qb = q[:, qs:qe]
        kb = k[:, :qe]
        vb = v[:, :qe]
        logits = jnp.einsum("bqhd,bkhd->bhqk", qb, kb).astype(jnp.float32) * scale
        logits = logits + cum[:, :, qs:qe, None] - cum[:, :, None, :qe]
        mask = jnp.arange(qe)[None, :] <= (qs + jnp.arange(QUERY_BLOCK))[:, None]
        logits = jnp.where(mask, logits, -jnp.inf)
        p = jax.nn.softmax(logits, axis=-1)
        outs.append(jnp.einsum("bhqk,bkhd->bqhd", p.astype(v.dtype), vb))
    return jnp.concatenate(outs, axis=1)


def gated_delta_chunked(q, k, v, g, beta):
    b, s, h, dk = q.shape
    dv = v.shape[-1]
    n = s // CHUNK

    def to_chunks(t):
        return t.reshape(b, n, CHUNK, h, t.shape[-1]).transpose(0, 3, 1, 2, 4)

    q, k, v = to_chunks(q), to_chunks(k), to_chunks(v)
    g = g.reshape(b, n, CHUNK, h).transpose(0, 3, 1, 2)
    beta = beta.reshape(b, n, CHUNK, h).transpose(0, 3, 1, 2)
    gc = jnp.cumsum(g, axis=-1)
    incl = jnp.tril(jnp.ones((CHUNK, CHUNK), dtype=bool))
    strict = jnp.tril(jnp.ones((CHUNK, CHUNK), dtype=bool), k=-1)
    decay = jnp.exp(jnp.where(incl, gc[..., :, None] - gc[..., None, :], -jnp.inf))
    k_beta = k * beta[..., None]
    v_beta = v * beta[..., None]
    m = jnp.where(strict, jnp.einsum("bhncd,bhnkd->bhnck", k_beta, k) * decay, 0.0)
    eye = jnp.broadcast_to(jnp.eye(CHUNK, dtype=m.dtype), m.shape)
    t_inv = lax.linalg.triangular_solve(eye + m, eye, left_side=True, lower=True,
                                        unit_diagonal=True)
    u = jnp.einsum("bhnck,bhnkv->bhncv", t_inv, v_beta)
    w = jnp.einsum("bhnck,bhnkd->bhncd", t_inv, k_beta * jnp.exp(gc)[..., None])
    a_intra = jnp.einsum("bhncd,bhnkd->bhnck", q, k) * decay
    q_dec = q * jnp.exp(gc)[..., None]
    k_dec = k * jnp.exp(gc[..., -1:] - gc)[..., None]
    g_last = jnp.exp(gc[..., -1])

    def step(state, inp):
        qd, kd, uc, wc, ac, gl = inp
        v_new = uc - jnp.einsum("bhcd,bhdv->bhcv", wc, state)
        o = jnp.einsum("bhcd,bhdv->bhcv", qd, state) + jnp.einsum("bhck,bhkv->bhcv", ac, v_new)
        state = state * gl[..., None, None] + jnp.einsum("bhcd,bhcv->bhdv", kd, v_new)
        return state, o

    def chunk_major(t):
        return jnp.moveaxis(t, 2, 0)

    state0 = jnp.zeros((b, h, dk, dv), dtype=jnp.float32)
    _, o = lax.scan(step, state0, (chunk_major(q_dec), chunk_major(k_dec), chunk_major(u),
                                   chunk_major(w), chunk_major(a_intra), chunk_major(g_last)))
    return o.transpose(1, 0, 3, 2, 4).reshape(b, s, h, dv)


def setup_inputs(seed: int = 0) -> dict:
    key = jax.random.key(seed)
    ks = jax.random.split(key, 16)
    f32 = jnp.float32

    def gain(k, n):
        return 1.0 + 0.1 * jax.random.normal(k, (n,), f32)

    x = jax.random.normal(ks[0], (BATCH, SEQ, D_MODEL), f32)
    pre_mix_norm = gain(ks[1], D_MODEL)
    w_in = jax.random.normal(ks[2], (D_MODEL, D_PROJ), f32) * D_MODEL ** -0.5
    fox_f_bias = 3.0 + 0.5 * jax.random.normal(ks[3], (N_FOX_HEADS,), f32)
    fox_out_norm = gain(ks[4], FOX_HEAD_DIM)
    gdn_conv_w = jax.random.normal(ks[5], (CONV_K, 3 * D_GDN), f32) * CONV_K ** -0.5
    gdn_a_log = jnp.log(jax.random.uniform(ks[6], (N_GDN_HEADS,), f32, 1.0, 16.0))
    dt = jnp.exp(jax.random.uniform(ks[7], (N_GDN_HEADS,), f32, np.log(1e-3), np.log(1e-1)))
    gdn_dt_bias = dt + jnp.log(-jnp.expm1(-dt))
    gdn_out_norm = gain(ks[8], GDN_HEAD_DIM)
    w_out = jax.random.normal(ks[9], (D_MIX, D_MODEL), f32) * D_MIX ** -0.5
    post_mix_norm = gain(ks[10], D_MODEL)
    pre_mlp_norm = gain(ks[11], D_MODEL)
    w_up = jax.random.normal(ks[12], (D_MODEL, D_FF), f32) * D_MODEL ** -0.5
    w_down = jax.random.normal(ks[13], (D_FF, D_MODEL), f32) * D_FF ** -0.5
    post_mlp_norm = gain(ks[14], D_MODEL)
    return {"x": x, "pre_mix_norm": pre_mix_norm, "w_in": w_in, "fox_f_bias": fox_f_bias,
            "fox_out_norm": fox_out_norm, "gdn_conv_w": gdn_conv_w, "gdn_a_log": gdn_a_log,
            "gdn_dt_bias": gdn_dt_bias, "gdn_out_norm": gdn_out_norm, "w_out": w_out,
            "post_mix_norm": post_mix_norm, "pre_mlp_norm": pre_mlp_norm, "w_up": w_up,
            "w_down": w_down, "post_mlp_norm": post_mlp_norm}


def reference(x, pre_mix_norm, w_in, fox_f_bias, fox_out_norm, gdn_conv_w, gdn_a_log,
              gdn_dt_bias, gdn_out_norm, w_out, post_mix_norm, pre_mlp_norm, w_up, w_down,
              post_mlp_norm):
    b, s, _ = x.shape
    split_at = [int(v) for v in np.cumsum(PROJ_SIZES)[:-1]]
    for _layer in range(DEPTH):
        h = rms_norm(x, pre_mix_norm)
        proj = h @ w_in
        fq, fk, fv, ff, gq, gk, gv, gb, ga, gz = jnp.split(proj, split_at, axis=-1)

        fq = fq.reshape(b, s, N_FOX_HEADS, FOX_HEAD_DIM)
        fk = fk.reshape(b, s, N_FOX_HEADS, FOX_HEAD_DIM)
        fv = fv.reshape(b, s, N_FOX_HEADS, FOX_HEAD_DIM)
        log_f = jax.nn.log_sigmoid(ff.astype(jnp.float32) + fox_f_bias.astype(jnp.float32))
        fox_o = forgetting_attention(fq, fk, fv, log_f)
        fox_o = rms_norm(fox_o, fox_out_norm).reshape(b, s, D_FOX)

        qkv = jax.nn.silu(causal_depthwise_conv(jnp.concatenate([gq, gk, gv], axis=-1), gdn_conv_w))
        qkv = qkv.astype(jnp.float32)
        cq, ck, cv = jnp.split(qkv, [D_GDN, 2 * D_GDN], axis=-1)
        cq = cq.reshape(b, s, N_GDN_HEADS, GDN_HEAD_DIM)
        ck = ck.reshape(b, s, N_GDN_HEADS, GDN_HEAD_DIM)
        cv = cv.reshape(b, s, N_GDN_HEADS, GDN_HEAD_DIM)
        cq = cq * lax.rsqrt(jnp.sum(cq * cq, axis=-1, keepdims=True) + EPS) * GDN_HEAD_DIM ** -0.5
        ck = ck * lax.rsqrt(jnp.sum(ck * ck, axis=-1, keepdims=True) + EPS)
        beta = jax.nn.sigmoid(gb.astype(jnp.float32))
        g = -jnp.exp(gdn_a_log.astype(jnp.float32)) * jax.nn.softplus(
            ga.astype(jnp.float32) + gdn_dt_bias.astype(jnp.float32))
        gdn_o = gated_delta_chunked(cq, ck, cv, g, beta)
        gate = jax.nn.silu(gz.astype(jnp.float32)).reshape(b, s, N_GDN_HEADS, GDN_HEAD_DIM)
        gdn_o = (rms_norm(gdn_o, gdn_out_norm) * gate).astype(x.dtype).reshape(b, s, D_GDN)

        mixed = jnp.concatenate([fox_o, gdn_o], axis=-1) @ w_out
        x = x + rms_norm(mixed, post_mix_norm)

        h = rms_norm(x, pre_mlp_norm)
        y = jnp.square(jax.nn.relu(h @ w_up)) @ w_down
        x = x + rms_norm(y, post_mlp_norm)
    return x
```

```python
import functools

import jax
import jax.numpy as jnp
from jax import lax
from jax.experimental import pallas as pl
from jax.experimental.pallas import tpu as pltpu

F32 = jnp.float32
BF16 = jnp.bfloat16

EPS = 1e-6
LANES = 128
FOX_HEAD_DIM = 64
N_FOX_HEADS = 8
D_FOX = FOX_HEAD_DIM * N_FOX_HEADS
GDN_HEAD_DIM = 128
N_GDN_HEADS = 4
D_GDN = GDN_HEAD_DIM * N_GDN_HEADS
CHUNK = 64
CONV_K = 4
SUB = 16

SMALL_F0 = 0
SMALL_B0 = N_FOX_HEADS
SMALL_A0 = N_FOX_HEADS + N_GDN_HEADS

VMEM_LIMIT = 56 * 1024 * 1024


def _dot(a, b):
    return jnp.dot(a, b, preferred_element_type=F32)


def _dot_nt(a, b):
    return lax.dot_general(a, b, (((1,), (1,)), ((), ())), preferred_element_type=F32)


def _dot_tn(a, b):
    return lax.dot_general(a, b, (((0,), (0,)), ((), ())), preferred_element_type=F32)


def _rms(x, w):
    return x * lax.rsqrt(jnp.mean(x * x, axis=-1, keepdims=True) + EPS) * w


def _split3(x):
    hi = x.astype(BF16)
    r1 = x - hi.astype(F32)
    mid = r1.astype(BF16)
    lo = (r1 - mid.astype(F32)).astype(BF16)
    return hi, mid, lo


def _tril_ones(n, dtype):
    r = lax.broadcasted_iota(jnp.int32, (n, n), 0)
    c = lax.broadcasted_iota(jnp.int32, (n, n), 1)
    return jnp.where(c <= r, 1.0, 0.0).astype(dtype)


def _cumsum_rows(tril_bf16, x):
    hi, mid, lo = _split3(x)
    return _dot(tril_bf16, hi) + _dot(tril_bf16, mid) + _dot(tril_bf16, lo)


def _sigmoid(x):
    return 1.0 / (1.0 + jnp.exp(-x))


def _softplus(x):
    return jnp.maximum(x, 0.0) + jnp.log1p(jnp.exp(-jnp.abs(x)))


def _in_proj_kernel(x_ref, g_ref, wf_ref, wg_ref, ws_ref, fox_ref, gdn_ref, small_ref):
    h = _rms(x_ref[...], g_ref[...]).astype(BF16)
    fox_ref[...] = _dot(h, wf_ref[...]).astype(BF16)
    gdn_ref[...] = _dot(h, wg_ref[...])
    small_ref[...] = _dot(h, ws_ref[...])


def _in_proj(x2, gain, wf, wg, ws, tm):
    n, d = x2.shape
    const = lambda i: (0, 0)
    row = lambda i: (i, 0)
    return pl.pallas_call(
        _in_proj_kernel,
        grid=(n // tm,),
        in_specs=[
            pl.BlockSpec((tm, d), row),
            pl.BlockSpec((1, d), const),
            pl.BlockSpec(wf.shape, const, pipeline_mode=pl.Buffered(1)),
            pl.BlockSpec(wg.shape, const, pipeline_mode=pl.Buffered(1)),
            pl.BlockSpec(ws.shape, const, pipeline_mode=pl.Buffered(1)),
        ],
        out_specs=[
            pl.BlockSpec((tm, wf.shape[1]), row),
            pl.BlockSpec((tm, wg.shape[1]), row),
            pl.BlockSpec((tm, ws.shape[1]), row),
        ],
        out_shape=[
            jax.ShapeDtypeStruct((n, wf.shape[1]), BF16),
            jax.ShapeDtypeStruct((n, wg.shape[1]), F32),
            jax.ShapeDtypeStruct((n, ws.shape[1]), F32),
        ],
        compiler_params=pltpu.CompilerParams(
            dimension_semantics=("arbitrary",), vmem_limit_bytes=VMEM_LIMIT),
        name="in_proj",
    )(x2, gain, wf, wg, ws)


def _fox_kernel(q_ref, k_ref, v_ref, small_ref, fbias_ref, fnorm_ref, o_ref,
                kaug_ref, vaug_ref, qext_ref, *, seq, tq, cum_blk):
    p = pl.program_id(1)
    qi = pl.program_id(2)
    lane = lax.broadcasted_iota(jnp.int32, (1, LANES), 1)
    half = FOX_HEAD_DIM
    head_mask = (lane < half, lane >= half)
    ext_off = (half, 0)
    ones_lane = (half, 0)

    @pl.when(qi == 0)
    def _prepare():
        tril = _tril_ones(cum_blk, BF16)
        carry = jnp.zeros((1, LANES), F32)
        cums = []
        for blk in range(seq // cum_blk):
            z = small_ref[blk * cum_blk:(blk + 1) * cum_blk, :] + fbias_ref[...]
            lf = -_softplus(-z)
            c = _cumsum_rows(tril, lf) + carry
            carry = c[cum_blk - 1:cum_blk, :]
            cums.append(c)
        cum = jnp.concatenate(cums, axis=0)
        kp = k_ref[...]
        vp = v_ref[...]
        for e in range(2):
            sel = lane == (2 * p + e + SMALL_F0)
            c = jnp.sum(jnp.where(sel, cum, 0.0), axis=-1, keepdims=True)
            hi, mid, lo = (t.astype(F32) for t in _split3(c))
            o = ext_off[e]
            eq = jnp.where(lane == o, hi, jnp.where(lane == o + 1, mid, jnp.where(
                lane == o + 2, lo, jnp.where((lane >= o + 3) & (lane < o + 6), 1.0, 0.0))))
            ek = jnp.where((lane >= o) & (lane < o + 3), 1.0, jnp.where(
                lane == o + 3, -hi, jnp.where(lane == o + 4, -mid, jnp.where(
                    lane == o + 5, -lo, 0.0))))
            qext_ref[e] = eq.astype(BF16)
            kaug_ref[e] = jnp.where(head_mask[e], kp, ek.astype(BF16))
            ones_col = jnp.where(lane == ones_lane[e], 1.0, 0.0).astype(BF16)
            vaug_ref[e] = jnp.where(head_mask[e], vp, ones_col)

    scale = FOX_HEAD_DIM ** -0.5
    r0 = pl.multiple_of(qi * tq, tq)
    qs = q_ref[...] * jnp.asarray(scale, BF16)
    qaug = [jnp.where(head_mask[e], qs, qext_ref[e, pl.ds(r0, tq), :]) for e in range(2)]

    def block(j0, carry, masked):
        out = []
        for e in range(2):
            m, acc = carry[e]
            kb = kaug_ref[e, pl.ds(j0, tq), :]
            vb = vaug_ref[e, pl.ds(j0, tq), :]
            s = _dot_nt(qaug[e], kb)
            if masked:
                rr = lax.broadcasted_iota(jnp.int32, (tq, tq), 0)
                cc = lax.broadcasted_iota(jnp.int32, (tq, tq), 1)
                s = jnp.where(cc <= rr, s, -jnp.inf)
            m_new = jnp.maximum(m, jnp.max(s, axis=-1, keepdims=True))
            alpha = jnp.exp(m - m_new)
            pexp = jnp.exp(s - m_new)
            acc = alpha * acc + _dot(pexp.astype(BF16), vb)
            out.append((m_new, acc))
        return tuple(out)

    init = tuple((jnp.full((tq, 1), -jnp.inf, F32), jnp.zeros((tq, LANES), F32)) for _ in range(2))
    carry = lax.fori_loop(
        0, qi, lambda j, c: block(pl.multiple_of(j * tq, tq), c, False), init)
    carry = block(r0, carry, True)

    outs = []
    for e in range(2):
        _, acc = carry[e]
        l = jnp.sum(jnp.where(lane == ones_lane[e], acc, 0.0), axis=-1, keepdims=True)
        outs.append(acc / l)
    o = jnp.where(head_mask[0], outs[0], outs[1])
    o2 = o * o
    ss0 = jnp.sum(jnp.where(head_mask[0], o2, 0.0), axis=-1, keepdims=True)
    ss1 = jnp.sum(jnp.where(head_mask[1], o2, 0.0), axis=-1, keepdims=True)
    ms = jnp.where(head_mask[0], ss0, ss1) * (1.0 / FOX_HEAD_DIM)
    o_ref[...] = (o * lax.rsqrt(ms + EPS) * fnorm_ref[...]).astype(o_ref.dtype)


def _fox(fox_qkv, small, fbias, fnorm, batch, seq, tq):
    n = batch * seq
    nq = seq // tq
    pairs = N_FOX_HEADS // 2
    kern = functools.partial(_fox_kernel, seq=seq, tq=tq, cum_blk=256)
    return pl.pallas_call(
        kern,
        grid=(batch, pairs, nq),
        in_specs=[
            pl.BlockSpec((tq, LANES), lambda b, p, i: (b * nq + i, p)),
            pl.BlockSpec((seq, LANES), lambda b, p, i: (b, pairs + p)),
            pl.BlockSpec((seq, LANES), lambda b, p, i: (b, 2 * pairs + p)),
            pl.BlockSpec((seq, LANES), lambda b, p, i: (b, 0)),
            pl.BlockSpec((1, LANES), lambda b, p, i: (0, 0)),
            pl.BlockSpec((1, LANES), lambda b, p, i: (0, 0)),
        ],
        out_specs=pl.BlockSpec((tq, LANES), lambda b, p, i: (b * nq + i, p)),
        out_shape=jax.ShapeDtypeStruct((n, D_FOX), BF16),
        scratch_shapes=[
            pltpu.VMEM((2, seq, LANES), BF16),
            pltpu.VMEM((2, seq, LANES), BF16),
            pltpu.VMEM((2, seq, LANES), BF16),
        ],
        compiler_params=pltpu.CompilerParams(
            dimension_semantics=("arbitrary", "arbitrary", "arbitrary"),
            vmem_limit_bytes=VMEM_LIMIT),
        name="fox_attention",
    )(fox_qkv, fox_qkv, fox_qkv, small, fbias, fnorm)


def _block_inverse(mt_bd, m_low, heads):
    c = CHUNK
    nb = c // SUB
    lane = lax.broadcasted_iota(jnp.int32, (1, c), 1)
    sub = lax.broadcasted_iota(jnp.int32, (SUB, 1), 0)
    row = lax.broadcasted_iota(jnp.int32, (c, c), 0)
    col = lax.broadcasted_iota(jnp.int32, (c, c), 1)
    result = []
    for h in range(heads):
        xb = [jnp.where(lane == (sub + b * SUB), 1.0, 0.0) for b in range(nb)]
        for i in range(1, SUB):
            pcol = jnp.sum(jnp.where((lane & (SUB - 1)) == i, mt_bd[h], 0.0),
                           axis=-1, keepdims=True)
            rows_needed = 8 if i <= 8 else SUB
            for b in range(nb):
                pb = pcol[b * SUB:b * SUB + rows_needed, :]
                s = jnp.sum(xb[b][:rows_needed, :] * pb, axis=0, keepdims=True)
                new = jnp.where(lane == (b * SUB + i), 1.0, 0.0) - s
                xb[b] = jnp.where(sub == i, new, xb[b])
        x = jnp.concatenate(xb, axis=0)
        size = SUB
        while size < c:
            sh = size.bit_length() - 1
            rb = row >> sh
            below = ((rb & 1) == 1) & ((col >> sh) == rb - 1)
            nmat = jnp.where(below, m_low[h], 0.0).astype(BF16)
            xbf = x.astype(BF16)
            x = x - _dot(xbf, _dot(nmat, xbf).astype(BF16))
            size *= 2
        result.append(x)
    return result


def _gdn_kernel(q_ref, k_ref, v_ref, z_ref, small_ref, cwq_ref, cwk_ref, cwv_ref,
                alog_ref, dtb_ref, onorm_ref, o_ref,
                u_ref, w_ref, qd_ref, kd_ref, a_ref, gl_ref, st_ref, *, seq, heads):
    hg = pl.program_id(1)
    c = CHUNK
    d = GDN_HEAD_DIM
    n_chunks = seq // c
    lane = lax.broadcasted_iota(jnp.int32, (1, LANES), 1)
    row = lax.broadcasted_iota(jnp.int32, (c, c), 0)
    col = lax.broadcasted_iota(jnp.int32, (c, c), 1)
    tril = _tril_ones(c, BF16)

    def conv_silu(ref, w_ref_, r0, first):
        cur = ref[pl.ds(r0, c), :]
        prev = ref[pl.ds(pl.multiple_of(jnp.maximum(r0 - 8, 0), 8), 8), :]
        prev = jnp.where(first, 0.0, prev)
        win = jnp.concatenate([prev, cur], axis=0)
        w = w_ref_[...]
        y = cur * w[CONV_K - 1:CONV_K, :]
        for j in range(1, CONV_K):
            shifted = pltpu.roll(win, j, axis=0)[8:, :]
            y = y + shifted * w[CONV_K - 1 - j:CONV_K - j, :]
        return y * _sigmoid(y)

    def prep(ci, _):
        r0 = pl.multiple_of(ci * c, c)
        first = ci == 0
        qa = conv_silu(q_ref, cwq_ref, r0, first)
        ka = conv_silu(k_ref, cwk_ref, r0, first)
        va = conv_silu(v_ref, cwv_ref, r0, first)
        sm = small_ref[pl.ds(r0, c), :]
        g_all = -jnp.exp(alog_ref[...]) * _softplus(sm + dtb_ref[...])
        gc_all = _cumsum_rows(tril, g_all)
        mt_bd, m_low, per_head = [], [], []
        for h in range(heads):
            hh = hg * heads + h
            q = qa[:, h * d:(h + 1) * d]
            k = ka[:, h * d:(h + 1) * d]
            v = va[:, h * d:(h + 1) * d]
            q = q * lax.rsqrt(jnp.sum(q * q, axis=-1, keepdims=True) + EPS) * (d ** -0.5)
            k = k * lax.rsqrt(jnp.sum(k * k, axis=-1, keepdims=True) + EPS)
            beta = _sigmoid(jnp.sum(jnp.where(lane == SMALL_B0 + hh, sm, 0.0),
                                    axis=-1, keepdims=True))
            gc = jnp.sum(jnp.where(lane == SMALL_A0 + hh, gc_all, 0.0),
                         axis=-1, keepdims=True)
            gc_last = gc[c - 1:c, :]
            eg = jnp.exp(gc)
            kb = k * beta
            vb = v * beta
            hi, mid, lo = (t.astype(F32) for t in _split3(gc))
            al = jnp.where(lane == 0, hi, jnp.where(lane == 1, mid, jnp.where(
                lane == 2, lo, jnp.where(lane < 6, 1.0, 0.0)))).astype(BF16)
            ar = jnp.where(lane < 3, 1.0, jnp.where(lane == 3, -hi, jnp.where(
                lane == 4, -mid, jnp.where(lane == 5, -lo, 0.0)))).astype(BF16)
            dmat = _dot_nt(al, ar)
            dmat_t = _dot_nt(ar, al)
            decay = jnp.exp(jnp.where(col <= row, dmat, -jnp.inf))
            decay_t = jnp.exp(jnp.where(col > row, dmat_t, -jnp.inf))
            kbf = k.astype(BF16)
            kbb = kb.astype(BF16)
            m = jnp.where(col < row, _dot_nt(kbb, kbf) * decay, 0.0)
            mt = _dot_nt(kbf, kbb) * decay_t
            sub_sh = SUB.bit_length() - 1
            mt_bd.append(jnp.where((row >> sub_sh) == (col >> sub_sh), mt, 0.0))
            m_low.append(m)
            a_intra = _dot_nt(q.astype(BF16), kbf) * decay
            per_head.append((q, k, vb, kb, eg, gc, gc_last, a_intra))
        t_inv = _block_inverse(mt_bd, m_low, heads)
        for h in range(heads):
            q, k, vb, kb, eg, gc, gc_last, a_intra = per_head[h]
            tb = t_inv[h].astype(BF16)
            rhs = jnp.concatenate([vb.astype(BF16), (kb * eg).astype(BF16)], axis=1)
            uw = _dot(tb, rhs)
            u_ref[h, pl.ds(r0, c), :] = uw[:, :d]
            w_ref[h, pl.ds(r0, c), :] = uw[:, d:].astype(BF16)
            qd_ref[h, pl.ds(r0, c), :] = (q * eg).astype(BF16)
            kd_ref[h, pl.ds(r0, c), :] = (k * jnp.exp(gc_last - gc)).astype(BF16)
            a_ref[h, pl.ds(r0, c), :] = a_intra.astype(BF16)
            gl_ref[h, pl.ds(pl.multiple_of(ci * 8, 8), 8), :] = jnp.broadcast_to(
                jnp.exp(gc_last), (8, d))
        return 0

    lax.fori_loop(0, n_chunks, prep, 0)

    st_ref[...] = jnp.zeros_like(st_ref)

    def scan(ci, _):
        r0 = pl.multiple_of(ci * c, c)
        for h in range(heads):
            state = st_ref[h]
            sb = state.astype(BF16)
            wq = jnp.concatenate([w_ref[h, pl.ds(r0, c), :], qd_ref[h, pl.ds(r0, c), :]], axis=0)
            r = _dot(wq, sb)
            v_new = u_ref[h, pl.ds(r0, c), :] - r[:c]
            vnb = v_new.astype(BF16)
            o = r[c:] + _dot(a_ref[h, pl.ds(r0, c), :], vnb)
            gl = gl_ref[h, pl.ds(pl.multiple_of(ci * 8, 8), 8), :][0:1, :]
            st_ref[h] = state * gl + _dot_tn(kd_ref[h, pl.ds(r0, c), :], vnb)
            z = z_ref[pl.ds(r0, c), h * d:(h + 1) * d]
            gate = z * _sigmoid(z)
            o_ref[pl.ds(r0, c), h * d:(h + 1) * d] = (
                _rms(o, onorm_ref[...]) * gate).astype(o_ref.dtype)
        return 0

    lax.fori_loop(0, n_chunks, scan, 0)


def _gdn(gdn_in, small, conv_w, alog_vec, dtb_vec, onorm, batch, seq, heads):
    n = batch * seq
    d = GDN_HEAD_DIM
    groups = N_GDN_HEADS // heads
    wd = heads * d
    kern = functools.partial(_gdn_kernel, seq=seq, heads=heads)
    col_block = lambda part: (lambda b, g: (b, part * groups + g))
    w_block = lambda part: (lambda b, g: (0, part * groups + g))
    const = lambda b, g: (0, 0)
    return pl.pallas_call(
        kern,
        grid=(batch, groups),
        in_specs=[
            pl.BlockSpec((seq, wd), col_block(0)),
            pl.BlockSpec((seq, wd), col_block(1)),
            pl.BlockSpec((seq, wd), col_block(2)),
            pl.BlockSpec((seq, wd), col_block(3)),
            pl.BlockSpec((seq, LANES), lambda b, g: (b, 0)),
            pl.BlockSpec((CONV_K, wd), w_block(0)),
            pl.BlockSpec((CONV_K, wd), w_block(1)),
            pl.BlockSpec((CONV_K, wd), w_block(2)),
            pl.BlockSpec((1, LANES), const),
            pl.BlockSpec((1, LANES), const),
            pl.BlockSpec((1, d), const),
        ],
        out_specs=pl.BlockSpec((seq, wd), lambda b, g: (b, g)),
        out_shape=jax.ShapeDtypeStruct((n, D_GDN), BF16),
        scratch_shapes=[
            pltpu.VMEM((heads, seq, d), F32),
            pltpu.VMEM((heads, seq, d), BF16),
            pltpu.VMEM((heads, seq, d), BF16),
            pltpu.VMEM((heads, seq, d), BF16),
            pltpu.VMEM((heads, seq, CHUNK), BF16),
            pltpu.VMEM((heads, (seq // CHUNK) * 8, d), F32),
            pltpu.VMEM((heads, d, d), F32),
        ],
        compiler_params=pltpu.CompilerParams(
            dimension_semantics=("arbitrary", "arbitrary"), vmem_limit_bytes=VMEM_LIMIT),
        name="gated_deltanet",
    )(gdn_in, gdn_in, gdn_in, gdn_in, small, conv_w, conv_w, conv_w, alog_vec, dtb_vec, onorm)


def _out_mlp_kernel(fox_ref, gdn_ref, x_ref, wof_ref, wog_ref, pmix_ref, pre_ref,
                    wup_ref, wdn_ref, post_ref, o_ref, *, ff_blk):
    mixed = _dot(fox_ref[...], wof_ref[...]) + _dot(gdn_ref[...], wog_ref[...])
    x1 = x_ref[...] + _rms(mixed, pmix_ref[...])
    h = _rms(x1, pre_ref[...]).astype(BF16)
    d_ff = wup_ref.shape[1]
    y = None
    for j in range(d_ff // ff_blk):
        a = _dot(h, wup_ref[:, j * ff_blk:(j + 1) * ff_blk])
        a = jnp.square(jnp.maximum(a, 0.0)).astype(BF16)
        t = _dot(a, wdn_ref[j * ff_blk:(j + 1) * ff_blk, :])
        y = t if y is None else y + t
    o_ref[...] = x1 + _rms(y, post_ref[...])


def _out_mlp(fox_o, gdn_o, x2, wof, wog, pmix, pre, wup, wdn, post, tm):
    n, d = x2.shape
    const = lambda i: (0, 0)
    row = lambda i: (i, 0)
    single = pl.Buffered(1)
    kern = functools.partial(_out_mlp_kernel, ff_blk=1024)
    return pl.pallas_call(
        kern,
        grid=(n // tm,),
        in_specs=[
            pl.BlockSpec((tm, fox_o.shape[1]), row),
            pl.BlockSpec((tm, gdn_o.shape[1]), row),
            pl.BlockSpec((tm, d), row),
            pl.BlockSpec(wof.shape, const, pipeline_mode=single),
            pl.BlockSpec(wog.shape, const, pipeline_mode=single),
            pl.BlockSpec((1, d), const),
            pl.BlockSpec((1, d), const),
            pl.BlockSpec(wup.shape, const, pipeline_mode=single),
            pl.BlockSpec(wdn.shape, const, pipeline_mode=single),
            pl.BlockSpec((1, d), const),
        ],
        out_specs=pl.BlockSpec((tm, d), row),
        out_shape=jax.ShapeDtypeStruct((n, d), F32),
        compiler_params=pltpu.CompilerParams(
            dimension_semantics=("arbitrary",), vmem_limit_bytes=VMEM_LIMIT),
        name="out_mlp",
    )(fox_o, gdn_o, x2, wof, wog, pmix, pre, wup, wdn, post)


def _lane_vec(values, offset):
    return jnp.zeros((1, LANES), F32).at[0, offset:offset + values.shape[0]].set(values.astype(F32))


def kernel(x, pre_mix_norm, w_in, fox_f_bias, fox_out_norm, gdn_conv_w, gdn_a_log, gdn_dt_bias,
           gdn_out_norm, w_out, post_mix_norm, pre_mlp_norm, w_up, w_down, post_mlp_norm):
    b, s, d = x.shape
    n = b * s
    x2 = x.reshape(n, d)

    o = 0
    parts = []
    for size in (D_FOX, D_FOX, D_FOX, N_FOX_HEADS, D_GDN, D_GDN, D_GDN, N_GDN_HEADS, N_GDN_HEADS, D_GDN):
        parts.append(w_in[:, o:o + size])
        o += size
    fq, fk, fv, ff, gq, gk, gv, gb, ga, gz = parts
    wf = jnp.concatenate([fq, fk, fv], axis=1).astype(BF16)
    wg = jnp.concatenate([gq, gk, gv, gz], axis=1).astype(BF16)
    ws = jnp.concatenate(
        [ff, gb, ga, jnp.zeros((d, LANES - N_FOX_HEADS - 2 * N_GDN_HEADS), w_in.dtype)],
        axis=1).astype(BF16)

    fox_qkv, gdn_in, small = _in_proj(x2, pre_mix_norm.reshape(1, d).astype(F32), wf, wg, ws, tm=512)

    fbias = _lane_vec(fox_f_bias, SMALL_F0)
    fnorm = jnp.tile(fox_out_norm.astype(F32), 2).reshape(1, LANES)
    fox_o = _fox(fox_qkv, small, fbias, fnorm, b, s, tq=256)

    alog_vec = _lane_vec(gdn_a_log, SMALL_A0)
    dtb_vec = _lane_vec(gdn_dt_bias, SMALL_A0)
    gdn_o = _gdn(gdn_in, small, gdn_conv_w.astype(F32), alog_vec, dtb_vec,
                 gdn_out_norm.reshape(1, GDN_HEAD_DIM).astype(F32), b, s, heads=2)

    wo = w_out.astype(BF16)
    out = _out_mlp(fox_o, gdn_o, x2, wo[:D_FOX], wo[D_FOX:],
                   post_mix_norm.reshape(1, d).astype(F32), pre_mlp_norm.reshape(1, d).astype(F32),
                   w_up.astype(BF16), w_down.astype(BF16), post_mlp_norm.reshape(1, d).astype(F32),
                   tm=512)
    return out.reshape(b, s, d)
```

```python
import functools

import jax
import jax.numpy as jnp
from jax import lax
from jax.experimental import pallas as pl
from jax.experimental.pallas import tpu as pltpu

F32 = jnp.float32
BF16 = jnp.bfloat16

EPS = 1e-6
LANES = 128
FOX_HEAD_DIM = 64
N_FOX_HEADS = 8
D_FOX = FOX_HEAD_DIM * N_FOX_HEADS
GDN_HEAD_DIM = 128
N_GDN_HEADS = 4
D_GDN = GDN_HEAD_DIM * N_GDN_HEADS
CHUNK = 64
CONV_K = 4

SMALL_F0 = 0
SMALL_B0 = N_FOX_HEADS
SMALL_A0 = N_FOX_HEADS + N_GDN_HEADS

VMEM_LIMIT = 56 * 1024 * 1024


def _dot(a, b):
    return jnp.dot(a, b, preferred_element_type=F32)


def _dot_nt(a, b):
    return lax.dot_general(a, b, (((1,), (1,)), ((), ())), preferred_element_type=F32)


def _dot_tn(a, b):
    return lax.dot_general(a, b, (((0,), (0,)), ((), ())), preferred_element_type=F32)


def _rms(x, w):
    return x * lax.rsqrt(jnp.mean(x * x, axis=-1, keepdims=True) + EPS) * w


def _split3(x):
    hi = x.astype(BF16)
    r1 = x - hi.astype(F32)
    mid = r1.astype(BF16)
    lo = (r1 - mid.astype(F32)).astype(BF16)
    return hi, mid, lo


def _tril_ones(n, dtype, block=None):
    r = lax.broadcasted_iota(jnp.int32, (n, n), 0)
    c = lax.broadcasted_iota(jnp.int32, (n, n), 1)
    keep = c <= r
    if block is not None:
        sh = block.bit_length() - 1
        keep = keep & ((r >> sh) == (c >> sh))
    return jnp.where(keep, 1.0, 0.0).astype(dtype)


def _cumsum_rows(tril_bf16, x):
    hi, mid, lo = _split3(x)
    return _dot(tril_bf16, hi) + _dot(tril_bf16, mid) + _dot(tril_bf16, lo)


def _sigmoid(x):
    return 0.5 + 0.5 * jnp.tanh(0.5 * x)


def _silu(x):
    h = 0.5 * x
    return h + h * jnp.tanh(h)


def _softplus(x):
    return jnp.maximum(x, 0.0) + jnp.log1p(jnp.exp(-jnp.abs(x)))


def _in_proj_kernel(x_ref, g_ref, wf_ref, wg_ref, ws_ref, fox_ref, gdn_ref, small_ref):
    h = _rms(x_ref[...], g_ref[...]).astype(BF16)
    fox_ref[...] = _dot(h, wf_ref[...]).astype(BF16)
    gdn_ref[...] = _dot(h, wg_ref[...])
    small_ref[...] = _dot(h, ws_ref[...])


def _in_proj(x2, gain, wf, wg, ws, tm):
    n, d = x2.shape
    const = lambda i: (0, 0)
    row = lambda i: (i, 0)
    return pl.pallas_call(
        _in_proj_kernel,
        grid=(n // tm,),
        in_specs=[
            pl.BlockSpec((tm, d), row),
            pl.BlockSpec((1, d), const),
            pl.BlockSpec(wf.shape, const, pipeline_mode=pl.Buffered(1)),
            pl.BlockSpec(wg.shape, const, pipeline_mode=pl.Buffered(1)),
            pl.BlockSpec(ws.shape, const, pipeline_mode=pl.Buffered(1)),
        ],
        out_specs=[
            pl.BlockSpec((tm, wf.shape[1]), row),
            pl.BlockSpec((tm, wg.shape[1]), row),
            pl.BlockSpec((tm, ws.shape[1]), row),
        ],
        out_shape=[
            jax.ShapeDtypeStruct((n, wf.shape[1]), BF16),
            jax.ShapeDtypeStruct((n, wg.shape[1]), F32),
            jax.ShapeDtypeStruct((n, ws.shape[1]), F32),
        ],
        compiler_params=pltpu.CompilerParams(
            dimension_semantics=("arbitrary",), vmem_limit_bytes=VMEM_LIMIT),
        name="in_proj",
    )(x2, gain, wf, wg, ws)


def _fox_kernel(q_ref, k_ref, v_ref, small_ref, fbias_ref, fnorm_ref, o_ref,
                kaug_ref, vaug_ref, qext_ref, *, seq, tq, cum_blk):
    p = pl.program_id(1)
    qi = pl.program_id(2)
    lane = lax.broadcasted_iota(jnp.int32, (1, LANES), 1)
    half = FOX_HEAD_DIM
    head_mask = (lane < half, lane >= half)
    ext_off = (half, 0)
    ones_lane = (half, 0)

    @pl.when(qi == 0)
    def _prepare():
        tril = _tril_ones(cum_blk, BF16)
        carry = jnp.zeros((1, LANES), F32)
        cums = []
        for blk in range(seq // cum_blk):
            z = small_ref[blk * cum_blk:(blk + 1) * cum_blk, :] + fbias_ref[...]
            lf = -_softplus(-z)
            c = _cumsum_rows(tril, lf) + carry
            carry = c[cum_blk - 1:cum_blk, :]
            cums.append(c)
        cum = jnp.concatenate(cums, axis=0)
        kp = k_ref[...]
        vp = v_ref[...]
        for e in range(2):
            sel = lane == (2 * p + e + SMALL_F0)
            c = jnp.sum(jnp.where(sel, cum, 0.0), axis=-1, keepdims=True)
            hi, mid, lo = (t.astype(F32) for t in _split3(c))
            o = ext_off[e]
            eq = jnp.where(lane == o, hi, jnp.where(lane == o + 1, mid, jnp.where(
                lane == o + 2, lo, jnp.where((lane >= o + 3) & (lane < o + 6), 1.0, 0.0))))
            ek = jnp.where((lane >= o) & (lane < o + 3), 1.0, jnp.where(
                lane == o + 3, -hi, jnp.where(lane == o + 4, -mid, jnp.where(
                    lane == o + 5, -lo, 0.0))))
            qext_ref[e] = eq.astype(BF16)
            kaug_ref[e] = jnp.where(head_mask[e], kp, ek.astype(BF16))
            ones_col = jnp.where(lane == ones_lane[e], 1.0, 0.0).astype(BF16)
            vaug_ref[e] = jnp.where(head_mask[e], vp, ones_col)

    scale = FOX_HEAD_DIM ** -0.5
    r0 = pl.multiple_of(qi * tq, tq)
    qs = q_ref[...] * jnp.asarray(scale, BF16)
    qaug = [jnp.where(head_mask[e], qs, qext_ref[e, pl.ds(r0, tq), :]) for e in range(2)]

    def block(j0, carry, masked):
        out = []
        for e in range(2):
            m, acc = carry[e]
            kb = kaug_ref[e, pl.ds(j0, tq), :]
            vb = vaug_ref[e, pl.ds(j0, tq), :]
            s = _dot_nt(qaug[e], kb)
            if masked:
                rr = lax.broadcasted_iota(jnp.int32, (tq, tq), 0)
                cc = lax.broadcasted_iota(jnp.int32, (tq, tq), 1)
                s = jnp.where(cc <= rr, s, -jnp.inf)
            m_new = jnp.maximum(m, jnp.max(s, axis=-1, keepdims=True))
            alpha = jnp.exp(m - m_new)
            pexp = jnp.exp(s - m_new)
            acc = alpha * acc + _dot(pexp.astype(BF16), vb)
            out.append((m_new, acc))
        return tuple(out)

    init = tuple((jnp.full((tq, 1), -jnp.inf, F32), jnp.zeros((tq, LANES), F32)) for _ in range(2))
    carry = lax.fori_loop(
        0, qi, lambda j, c: block(pl.multiple_of(j * tq, tq), c, False), init)
    carry = block(r0, carry, True)

    outs = []
    for e in range(2):
        _, acc = carry[e]
        l = jnp.sum(jnp.where(lane == ones_lane[e], acc, 0.0), axis=-1, keepdims=True)
        outs.append(acc / l)
    o = jnp.where(head_mask[0], outs[0], outs[1])
    o2 = o * o
    ss0 = jnp.sum(jnp.where(head_mask[0], o2, 0.0), axis=-1, keepdims=True)
    ss1 = jnp.sum(jnp.where(head_mask[1], o2, 0.0), axis=-1, keepdims=True)
    ms = jnp.where(head_mask[0], ss0, ss1) * (1.0 / FOX_HEAD_DIM)
    o_ref[...] = (o * lax.rsqrt(ms + EPS) * fnorm_ref[...]).astype(o_ref.dtype)


def _fox(fox_qkv, small, fbias, fnorm, batch, seq, tq):
    n = batch * seq
    nq = seq // tq
    pairs = N_FOX_HEADS // 2
    kern = functools.partial(_fox_kernel, seq=seq, tq=tq, cum_blk=256)
    return pl.pallas_call(
        kern,
        grid=(batch, pairs, nq),
        in_specs=[
            pl.BlockSpec((tq, LANES), lambda b, p, i: (b * nq + i, p)),
            pl.BlockSpec((seq, LANES), lambda b, p, i: (b, pairs + p)),
            pl.BlockSpec((seq, LANES), lambda b, p, i: (b, 2 * pairs + p)),
            pl.BlockSpec((seq, LANES), lambda b, p, i: (b, 0)),
            pl.BlockSpec((1, LANES), lambda b, p, i: (0, 0)),
            pl.BlockSpec((1, LANES), lambda b, p, i: (0, 0)),
        ],
        out_specs=pl.BlockSpec((tq, LANES), lambda b, p, i: (b * nq + i, p)),
        out_shape=jax.ShapeDtypeStruct((n, D_FOX), BF16),
        scratch_shapes=[
            pltpu.VMEM((2, seq, LANES), BF16),
            pltpu.VMEM((2, seq, LANES), BF16),
            pltpu.VMEM((2, seq, LANES), BF16),
        ],
        compiler_params=pltpu.CompilerParams(
            dimension_semantics=("arbitrary", "arbitrary", "arbitrary"),
            vmem_limit_bytes=VMEM_LIMIT),
        name="fox_attention",
    )(fox_qkv, fox_qkv, fox_qkv, small, fbias, fnorm)


def _doubling_inverse(m_lows):
    c = m_lows[0].shape[0]
    row = lax.broadcasted_iota(jnp.int32, (c, c), 0)
    col = lax.broadcasted_iota(jnp.int32, (c, c), 1)
    eye = jnp.where(row == col, 1.0, 0.0)
    pair = (row >> 1) == (col >> 1)
    xs = [eye - jnp.where(pair, m, 0.0) for m in m_lows]
    size = 2
    while size < c:
        sh = size.bit_length() - 1
        rb = row >> sh
        below = ((rb & 1) == 1) & ((col >> sh) == rb - 1)
        xbfs = [x.astype(BF16) for x in xs]
        nxs = [_dot(jnp.where(below, m, 0.0).astype(BF16), xb).astype(BF16)
               for m, xb in zip(m_lows, xbfs)]
        xs = [x - _dot(xb, nx) for x, xb, nx in zip(xs, xbfs, nxs)]
        size *= 2
    return xs


def _gdn_kernel(q_ref, k_ref, v_ref, z_ref, small_ref, cwq_ref, cwk_ref, cwv_ref,
                alog_ref, dtb_ref, onorm_ref, o_ref,
                w2_ref, rr_ref, qp_ref, op_ref, gl_ref, st_ref, *, seq, heads, group):
    hg = pl.program_id(1)
    c = CHUNK
    d = GDN_HEAD_DIM
    rows = group * c
    n_groups = seq // rows
    n_chunks = seq // c
    lane = lax.broadcasted_iota(jnp.int32, (1, LANES), 1)
    row = lax.broadcasted_iota(jnp.int32, (c, c), 0)
    col = lax.broadcasted_iota(jnp.int32, (c, c), 1)
    tril = _tril_ones(rows, BF16, block=c)

    def conv_silu(ref, w_ref_, r0, first):
        cur = ref[pl.ds(r0, rows), :]
        prev = ref[pl.ds(pl.multiple_of(jnp.maximum(r0 - 8, 0), 8), 8), :]
        prev = jnp.where(first, 0.0, prev)
        win = jnp.concatenate([prev, cur], axis=0)
        w = w_ref_[...]
        y = cur * w[CONV_K - 1:CONV_K, :]
        for j in range(1, CONV_K):
            shifted = pltpu.roll(win, j, axis=0)[8:, :]
            y = y + shifted * w[CONV_K - 1 - j:CONV_K - j, :]
        return _silu(y)

    def prep(gi, _):
        r0 = pl.multiple_of(gi * rows, rows)
        first = gi == 0
        qa = conv_silu(q_ref, cwq_ref, r0, first)
        ka = conv_silu(k_ref, cwk_ref, r0, first)
        va = conv_silu(v_ref, cwv_ref, r0, first)
        sm = small_ref[pl.ds(r0, rows), :]
        g_all = -jnp.exp(alog_ref[...]) * _softplus(sm + dtb_ref[...])
        gc_all = _cumsum_rows(tril, g_all)
        items = []
        for h in range(heads):
            hh = hg * heads + h
            qh = qa[:, h * d:(h + 1) * d]
            kh = ka[:, h * d:(h + 1) * d]
            vh = va[:, h * d:(h + 1) * d]
            qh = qh * (lax.rsqrt(jnp.sum(qh * qh, axis=-1, keepdims=True) + EPS) * (d ** -0.5))
            kh = kh * lax.rsqrt(jnp.sum(kh * kh, axis=-1, keepdims=True) + EPS)
            beta_h = _sigmoid(jnp.sum(jnp.where(lane == SMALL_B0 + hh, sm, 0.0),
                                      axis=-1, keepdims=True))
            gc_h = jnp.sum(jnp.where(lane == SMALL_A0 + hh, gc_all, 0.0),
                           axis=-1, keepdims=True)
            for g in range(group):
                sl = slice(g * c, (g + 1) * c)
                items.append(dict(h=h, g=g, q=qh[sl], k=kh[sl], v=vh[sl], beta=beta_h[sl],
                                  gc=gc_h[sl]))
        for it in items:
            gc = it["gc"]
            hi, mid, lo = (t.astype(F32) for t in _split3(gc))
            it["al"] = jnp.where(lane == 0, hi, jnp.where(lane == 1, mid, jnp.where(
                lane == 2, lo, jnp.where(lane < 6, 1.0, 0.0)))).astype(BF16)
            it["ar"] = jnp.where(lane < 3, 1.0, jnp.where(lane == 3, -hi, jnp.where(
                lane == 4, -mid, jnp.where(lane == 5, -lo, 0.0)))).astype(BF16)
            it["kb"] = it["k"] * it["beta"]
            it["kbf"] = it["k"].astype(BF16)
        dmats = [_dot_nt(it["al"], it["ar"]) for it in items]
        kks = [_dot_nt(it["kb"].astype(BF16), it["kbf"]) for it in items]
        qks = [_dot_nt(it["q"].astype(BF16), it["kbf"]) for it in items]
        m_lows = []
        for it, dmat, kk, qk in zip(items, dmats, kks, qks):
            decay = jnp.exp(jnp.where(col <= row, dmat, -jnp.inf))
            m_lows.append(jnp.where(col < row, kk * decay, 0.0))
            it["a"] = (qk * decay).astype(BF16)
        t_invs = _doubling_inverse(m_lows)
        wus = []
        for it, t_inv in zip(items, t_invs):
            eg = jnp.exp(it["gc"])
            it["eg"] = eg
            rhs = jnp.concatenate(
                [(it["kb"] * eg).astype(BF16), (it["v"] * it["beta"]).astype(BF16)], axis=1)
            wus.append(_dot(t_inv.astype(BF16), rhs).astype(BF16))
        kwus, awus = [], []
        for it, wu in zip(items, wus):
            gc_last = it["gc"][c - 1:c, :]
            it["gl"] = jnp.exp(gc_last)
            kd = (it["k"] * jnp.exp(gc_last - it["gc"])).astype(BF16)
            kwus.append(_dot_tn(kd, wu))
            awus.append(_dot(it["a"], wu))
        for it, kwu, awu in zip(items, kwus, awus):
            h = it["h"]
            ci = gi * group + it["g"]
            rw = pl.multiple_of(ci * d, d)
            rc = pl.multiple_of(ci * c, c)
            w2_ref[h, pl.ds(rw, d), :] = kwu[:, :d].astype(BF16)
            rr_ref[h, pl.ds(rw, d), :] = kwu[:, d:]
            qp_ref[h, pl.ds(rc, c), :] = (it["q"] * it["eg"] - awu[:, :d]).astype(BF16)
            op_ref[h, pl.ds(rc, c), :] = awu[:, d:]
            gl_ref[h, pl.ds(pl.multiple_of(ci * 8, 8), 8), :] = jnp.broadcast_to(it["gl"], (8, d))
        return 0

    lax.fori_loop(0, n_groups, prep, 0)

    st_ref[...] = jnp.zeros_like(st_ref)

    def scan(ci, _):
        rc = pl.multiple_of(ci * c, c)
        rw = pl.multiple_of(ci * d, d)
        for h in range(heads):
            state = st_ref[h]
            lhs = jnp.concatenate([w2_ref[h, pl.ds(rw, d), :], qp_ref[h, pl.ds(rc, c), :]], axis=0)
            r = _dot(lhs, state.astype(BF16))
            gl = gl_ref[h, pl.ds(pl.multiple_of(ci * 8, 8), 8), :][0:1, :]
            st_ref[h] = state * gl - r[:d] + rr_ref[h, pl.ds(rw, d), :]
            o = r[d:] + op_ref[h, pl.ds(rc, c), :]
            z = z_ref[pl.ds(rc, c), h * d:(h + 1) * d]
            o_ref[pl.ds(rc, c), h * d:(h + 1) * d] = (
                _rms(o, onorm_ref[...]) * _silu(z)).astype(o_ref.dtype)
        return 0

    lax.fori_loop(0, n_chunks, scan, 0)


def _gdn(gdn_in, small, conv_w, alog_vec, dtb_vec, onorm, batch, seq, heads, group):
    n = batch * seq
    d = GDN_HEAD_DIM
    groups = N_GDN_HEADS // heads
    wd = heads * d
    n_chunks = seq // CHUNK
    kern = functools.partial(_gdn_kernel, seq=seq, heads=heads, group=group)
    col_block = lambda part: (lambda b, g: (b, part * groups + g))
    w_block = lambda part: (lambda b, g: (0, part * groups + g))
    const = lambda b, g: (0, 0)
    return pl.pallas_call(
        kern,
        grid=(batch, groups),
        in_specs=[
            pl.BlockSpec((seq, wd), col_block(0)),
            pl.BlockSpec((seq, wd), col_block(1)),
            pl.BlockSpec((seq, wd), col_block(2)),
            pl.BlockSpec((seq, wd), col_block(3)),
            pl.BlockSpec((seq, LANES), lambda b, g: (b, 0)),
            pl.BlockSpec((CONV_K, wd), w_block(0)),
            pl.BlockSpec((CONV_K, wd), w_block(1)),
            pl.BlockSpec((CONV_K, wd), w_block(2)),
            pl.BlockSpec((1, LANES), const),
            pl.BlockSpec((1, LANES), const),
            pl.BlockSpec((1, d), const),
        ],
        out_specs=pl.BlockSpec((seq, wd), lambda b, g: (b, g)),
        out_shape=jax.ShapeDtypeStruct((n, D_GDN), BF16),
        scratch_shapes=[
            pltpu.VMEM((heads, n_chunks * d, d), BF16),
            pltpu.VMEM((heads, n_chunks * d, d), F32),
            pltpu.VMEM((heads, seq, d), BF16),
            pltpu.VMEM((heads, seq, d), F32),
            pltpu.VMEM((heads, n_chunks * 8, d), F32),
            pltpu.VMEM((heads, d, d), F32),
        ],
        compiler_params=pltpu.CompilerParams(
            dimension_semantics=("arbitrary", "arbitrary"), vmem_limit_bytes=VMEM_LIMIT),
        name="gated_deltanet",
    )(gdn_in, gdn_in, gdn_in, gdn_in, small, conv_w, conv_w, conv_w, alog_vec, dtb_vec, onorm)


def _out_mlp_kernel(fox_ref, gdn_ref, x_ref, wof_ref, wog_ref, pmix_ref, pre_ref,
                    wup_ref, wdn_ref, post_ref, o_ref, *, ff_blk):
    mixed = _dot(fox_ref[...], wof_ref[...]) + _dot(gdn_ref[...], wog_ref[...])
    x1 = x_ref[...] + _rms(mixed, pmix_ref[...])
    h = _rms(x1, pre_ref[...]).astype(BF16)
    d_ff = wup_ref.shape[1]
    y = None
    for j in range(d_ff // ff_blk):
        a = _dot(h, wup_ref[:, j * ff_blk:(j + 1) * ff_blk])
        a = jnp.square(jnp.maximum(a, 0.0)).astype(BF16)
        t = _dot(a, wdn_ref[j * ff_blk:(j + 1) * ff_blk, :])
        y = t if y is None else y + t
    o_ref[...] = x1 + _rms(y, post_ref[...])


def _out_mlp(fox_o, gdn_o, x2, wof, wog, pmix, pre, wup, wdn, post, tm):
    n, d = x2.shape
    const = lambda i: (0, 0)
    row = lambda i: (i, 0)
    single = pl.Buffered(1)
    kern = functools.partial(_out_mlp_kernel, ff_blk=1024)
    return pl.pallas_call(
        kern,
        grid=(n // tm,),
        in_specs=[
            pl.BlockSpec((tm, fox_o.shape[1]), row),
            pl.BlockSpec((tm, gdn_o.shape[1]), row),
            pl.BlockSpec((tm, d), row),
            pl.BlockSpec(wof.shape, const, pipeline_mode=single),
            pl.BlockSpec(wog.shape, const, pipeline_mode=single),
            pl.BlockSpec((1, d), const),
            pl.BlockSpec((1, d), const),
            pl.BlockSpec(wup.shape, const, pipeline_mode=single),
            pl.BlockSpec(wdn.shape, const, pipeline_mode=single),
            pl.BlockSpec((1, d), const),
        ],
        out_specs=pl.BlockSpec((tm, d), row),
        out_shape=jax.ShapeDtypeStruct((n, d), F32),
        compiler_params=pltpu.CompilerParams(
            dimension_semantics=("arbitrary",), vmem_limit_bytes=VMEM_LIMIT),
        name="out_mlp",
    )(fox_o, gdn_o, x2, wof, wog, pmix, pre, wup, wdn, post)


def _lane_vec(values, offset):
    return jnp.zeros((1, LANES), F32).at[0, offset:offset + values.shape[0]].set(values.astype(F32))


def kernel(x, pre_mix_norm, w_in, fox_f_bias, fox_out_norm, gdn_conv_w, gdn_a_log, gdn_dt_bias,
           gdn_out_norm, w_out, post_mix_norm, pre_mlp_norm, w_up, w_down, post_mlp_norm):
    b, s, d = x.shape
    n = b * s
    x2 = x.reshape(n, d)

    o = 0
    parts = []
    for size in (D_FOX, D_FOX, D_FOX, N_FOX_HEADS, D_GDN, D_GDN, D_GDN, N_GDN_HEADS, N_GDN_HEADS, D_GDN):
        parts.append(w_in[:, o:o + size])
        o += size
    fq, fk, fv, ff, gq, gk, gv, gb, ga, gz = parts
    wf = jnp.concatenate([fq, fk, fv], axis=1).astype(BF16)
    wg = jnp.concatenate([gq, gk, gv, gz], axis=1).astype(BF16)
    ws = jnp.concatenate(
        [ff, gb, ga, jnp.zeros((d, LANES - N_FOX_HEADS - 2 * N_GDN_HEADS), w_in.dtype)],
        axis=1).astype(BF16)

    fox_qkv, gdn_in, small = _in_proj(x2, pre_mix_norm.reshape(1, d).astype(F32), wf, wg, ws, tm=512)

    fbias = _lane_vec(fox_f_bias, SMALL_F0)
    fnorm = jnp.tile(fox_out_norm.astype(F32), 2).reshape(1, LANES)
    fox_o = _fox(fox_qkv, small, fbias, fnorm, b, s, tq=512)

    alog_vec = _lane_vec(gdn_a_log, SMALL_A0)
    dtb_vec = _lane_vec(gdn_dt_bias, SMALL_A0)
    gdn_o = _gdn(gdn_in, small, gdn_conv_w.astype(F32), alog_vec, dtb_vec,
                 gdn_out_norm.reshape(1, GDN_HEAD_DIM).astype(F32), b, s, heads=2, group=4)

    wo = w_out.astype(BF16)
    out = _out_mlp(fox_o, gdn_o, x2, wo[:D_FOX], wo[D_FOX:],
                   post_mix_norm.reshape(1, d).astype(F32), pre_mlp_norm.reshape(1, d).astype(F32),
                   w_up.astype(BF16), w_down.astype(BF16), post_mlp_norm.reshape(1, d).astype(F32),
                   tm=512)
    return out.reshape(b, s, d)
```

```python
import functools

import jax
import jax.numpy as jnp
from jax import lax
from jax.experimental import pallas as pl
from jax.experimental.pallas import tpu as pltpu

F32 = jnp.float32
BF16 = jnp.bfloat16

EPS = 1e-6
LANES = 128
FOX_HEAD_DIM = 64
N_FOX_HEADS = 8
D_FOX = FOX_HEAD_DIM * N_FOX_HEADS
GDN_HEAD_DIM = 128
N_GDN_HEADS = 4
D_GDN = GDN_HEAD_DIM * N_GDN_HEADS
CHUNK = 64
CONV_K = 4

FOX_REP = 6
SMALL_B0 = 64
SMALL_A0 = SMALL_B0 + N_GDN_HEADS

VMEM_LIMIT = 56 * 1024 * 1024


def _dot(a, b):
    return jnp.dot(a, b, preferred_element_type=F32)


def _dot_nt(a, b):
    return lax.dot_general(a, b, (((1,), (1,)), ((), ())), preferred_element_type=F32)


def _dot_tn(a, b):
    return lax.dot_general(a, b, (((0,), (0,)), ((), ())), preferred_element_type=F32)


def _rms(x, w):
    return x * lax.rsqrt(jnp.mean(x * x, axis=-1, keepdims=True) + EPS) * w


def _split3(x):
    hi = x.astype(BF16)
    r1 = x - hi.astype(F32)
    mid = r1.astype(BF16)
    lo = (r1 - mid.astype(F32)).astype(BF16)
    return hi, mid, lo


def _tril_ones(n, dtype, block=None):
    r = lax.broadcasted_iota(jnp.int32, (n, n), 0)
    c = lax.broadcasted_iota(jnp.int32, (n, n), 1)
    keep = c <= r
    if block is not None:
        sh = block.bit_length() - 1
        keep = keep & ((r >> sh) == (c >> sh))
    return jnp.where(keep, 1.0, 0.0).astype(dtype)


def _cumsum_rows(tril_bf16, x):
    hi, mid, lo = _split3(x)
    return _dot(tril_bf16, hi) + _dot(tril_bf16, mid) + _dot(tril_bf16, lo)


def _sigmoid(x):
    return 0.5 + 0.5 * jnp.tanh(0.5 * x)


def _silu(x):
    h = 0.5 * x
    return h + h * jnp.tanh(h)


def _softplus(x):
    return jnp.maximum(x, 0.0) + jnp.log1p(jnp.exp(-jnp.abs(x)))


def _in_proj_kernel(x_ref, g_ref, wf_ref, wg_ref, ws_ref, cw_ref, fox_ref, gdn_ref, small_ref,
                    halo_ref, win_ref, *, tiles_per_seq):
    i = pl.program_id(0)
    d = GDN_HEAD_DIM
    h = _rms(x_ref[...], g_ref[...]).astype(BF16)
    tm = h.shape[0]
    first = i % tiles_per_seq == 0
    slab = 2 * d
    for sl in range(4 * D_GDN // slab):
        cols = slice(sl * slab, (sl + 1) * slab)
        g = _dot(h, wg_ref[:, cols])
        if sl * slab >= 3 * D_GDN:
            gdn_ref[:, cols] = _silu(g).astype(BF16)
            continue
        win_ref[0:8, :] = jnp.where(first, 0.0, halo_ref[:, cols])
        win_ref[8:, :] = g
        halo_ref[:, cols] = g[tm - 8:, :]
        w = cw_ref[:, cols]
        y = g * w[CONV_K - 1:CONV_K, :]
        for j in range(1, CONV_K):
            shifted = win_ref[8 - j:8 - j + tm, :]
            y = y + shifted * w[CONV_K - 1 - j:CONV_K - j, :]
        act = _silu(y)
        if sl * slab >= 2 * D_GDN:
            gdn_ref[:, cols] = act.astype(BF16)
            continue
        for hd in range(slab // d):
            t = act[:, hd * d:(hd + 1) * d]
            r = lax.rsqrt(jnp.sum(t * t, axis=-1, keepdims=True) + EPS)
            if sl * slab < D_GDN:
                r = r * (d ** -0.5)
            gdn_ref[:, sl * slab + hd * d:sl * slab + (hd + 1) * d] = (t * r).astype(BF16)
    fox_ref[...] = _dot(h, wf_ref[...]).astype(BF16)
    small_ref[...] = _dot(h, ws_ref[...])


def _in_proj(x2, gain, wf, wg, ws, conv_w, seq, tm):
    n, d = x2.shape
    const = lambda i: (0, 0)
    row = lambda i: (i, 0)
    kern = functools.partial(_in_proj_kernel, tiles_per_seq=seq // tm)
    return pl.pallas_call(
        kern,
        grid=(n // tm,),
        in_specs=[
            pl.BlockSpec((tm, d), row),
            pl.BlockSpec((1, d), const),
            pl.BlockSpec(wf.shape, const, pipeline_mode=pl.Buffered(1)),
            pl.BlockSpec(wg.shape, const, pipeline_mode=pl.Buffered(1)),
            pl.BlockSpec(ws.shape, const, pipeline_mode=pl.Buffered(1)),
            pl.BlockSpec(conv_w.shape, const),
        ],
        out_specs=[
            pl.BlockSpec((tm, wf.shape[1]), row),
            pl.BlockSpec((tm, wg.shape[1]), row),
            pl.BlockSpec((tm, ws.shape[1]), row),
        ],
        out_shape=[
            jax.ShapeDtypeStruct((n, wf.shape[1]), BF16),
            jax.ShapeDtypeStruct((n, wg.shape[1]), BF16),
            jax.ShapeDtypeStruct((n, ws.shape[1]), F32),
        ],
        scratch_shapes=[pltpu.VMEM((8, 3 * D_GDN), F32),
                        pltpu.VMEM((tm + 8, 2 * GDN_HEAD_DIM), F32)],
        compiler_params=pltpu.CompilerParams(
            dimension_semantics=("arbitrary",), vmem_limit_bytes=VMEM_LIMIT),
        name="in_proj",
    )(x2, gain, wf, wg, ws, conv_w)


def _fox_kernel(q_ref, k_ref, v_ref, small_ref, fbias_ref, fnorm_ref, o_ref,
                qx_ref, kx_ref, vaug_ref, *, seq, tq, cum_blk):
    p = pl.program_id(1)
    qi = pl.program_id(2)
    lane = lax.broadcasted_iota(jnp.int32, (1, LANES), 1)
    half = FOX_HEAD_DIM
    nh = N_FOX_HEADS
    head_mask = (lane < half, lane >= half)
    ones_lane = (half, 0)
    zero = jnp.zeros((), BF16)

    @pl.when((p == 0) & (qi == 0))
    def _per_sequence():
        tril = _tril_ones(cum_blk, BF16)
        carry = jnp.zeros((1, LANES), F32)
        for blk in range(seq // cum_blk):
            rows = slice(blk * cum_blk, (blk + 1) * cum_blk)
            z = small_ref[rows, :] + fbias_ref[...]
            c = _cumsum_rows(tril, -_softplus(-z)) + carry
            carry = c[cum_blk - 1:cum_blk, :]
            hi, mid, lo = (t.astype(F32) for t in _split3(c))
            qx_ref[rows, :] = jnp.where(lane < nh, hi, jnp.where(lane < 2 * nh, mid, jnp.where(
                lane < 3 * nh, lo, jnp.where(lane < 6 * nh, 1.0, 0.0)))).astype(BF16)
            kx_ref[rows, :] = jnp.where(lane < 3 * nh, 1.0, jnp.where(lane < 4 * nh, -hi, jnp.where(
                lane < 5 * nh, -mid, jnp.where(lane < 6 * nh, -lo, 0.0)))).astype(BF16)

    @pl.when(qi == 0)
    def _per_pair():
        vp = v_ref[...]
        for e in range(2):
            ones_col = jnp.where(lane == ones_lane[e], 1.0, 0.0).astype(BF16)
            vaug_ref[e] = jnp.where(head_mask[e], vp, ones_col)

    scale = FOX_HEAD_DIM ** -0.5
    r0 = pl.multiple_of(qi * tq, tq)
    qs = q_ref[...] * jnp.asarray(scale, BF16)
    qx = qx_ref[pl.ds(r0, tq), :]
    qaug = []
    for e in range(2):
        xmask = ((lane & (nh - 1)) == 2 * p + e) & (lane < 6 * nh)
        qaug.append(jnp.concatenate(
            [jnp.where(head_mask[e], qs, zero), jnp.where(xmask, qx, zero)], axis=1))

    def block(j0, carry, masked):
        kb = jnp.concatenate([k_ref[pl.ds(j0, tq), :], kx_ref[pl.ds(j0, tq), :]], axis=1)
        out = []
        for e in range(2):
            m, acc = carry[e]
            vb = vaug_ref[e, pl.ds(j0, tq), :]
            s = _dot_nt(qaug[e], kb)
            if masked:
                rr = lax.broadcasted_iota(jnp.int32, (tq, tq), 0)
                cc = lax.broadcasted_iota(jnp.int32, (tq, tq), 1)
                s = jnp.where(cc <= rr, s, -jnp.inf)
            m_new = jnp.maximum(m, jnp.max(s, axis=-1, keepdims=True))
            alpha = jnp.exp(m - m_new)
            pexp = jnp.exp(s - m_new)
            acc = alpha * acc + _dot(pexp.astype(BF16), vb)
            out.append((m_new, acc))
        return tuple(out)

    init = tuple((jnp.full((tq, 1), -jnp.inf, F32), jnp.zeros((tq, LANES), F32)) for _ in range(2))
    carry = lax.fori_loop(
        0, qi, lambda j, c: block(pl.multiple_of(j * tq, tq), c, False), init)
    carry = block(r0, carry, True)

    outs = []
    for e in range(2):
        _, acc = carry[e]
        l = jnp.sum(jnp.where(lane == ones_lane[e], acc, 0.0), axis=-1, keepdims=True)
        outs.append(acc / l)
    o = jnp.where(head_mask[0], outs[0], outs[1])
    o2 = o * o
    ss0 = jnp.sum(jnp.where(head_mask[0], o2, 0.0), axis=-1, keepdims=True)
    ss1 = jnp.sum(jnp.where(head_mask[1], o2, 0.0), axis=-1, keepdims=True)
    ms = jnp.where(head_mask[0], ss0, ss1) * (1.0 / FOX_HEAD_DIM)
    o_ref[...] = (o * lax.rsqrt(ms + EPS) * fnorm_ref[...]).astype(o_ref.dtype)


def _fox(fox_qkv, small, fbias, fnorm, batch, seq, tq):
    n = batch * seq
    nq = seq // tq
    pairs = N_FOX_HEADS // 2
    kern = functools.partial(_fox_kernel, seq=seq, tq=tq, cum_blk=256)
    return pl.pallas_call(
        kern,
        grid=(batch, pairs, nq),
        in_specs=[
            pl.BlockSpec((tq, LANES), lambda b, p, i: (b * nq + i, p)),
            pl.BlockSpec((seq, LANES), lambda b, p, i: (b, pairs + p)),
            pl.BlockSpec((seq, LANES), lambda b, p, i: (b, 2 * pairs + p)),
            pl.BlockSpec((seq, LANES), lambda b, p, i: (b, 0)),
            pl.BlockSpec((1, LANES), lambda b, p, i: (0, 0)),
            pl.BlockSpec((1, LANES), lambda b, p, i: (0, 0)),
        ],
        out_specs=pl.BlockSpec((tq, LANES), lambda b, p, i: (b * nq + i, p)),
        out_shape=jax.ShapeDtypeStruct((n, D_FOX), BF16),
        scratch_shapes=[
            pltpu.VMEM((seq, LANES), BF16),
            pltpu.VMEM((seq, LANES), BF16),
            pltpu.VMEM((2, seq, LANES), BF16),
        ],
        compiler_params=pltpu.CompilerParams(
            dimension_semantics=("arbitrary", "arbitrary", "arbitrary"),
            vmem_limit_bytes=VMEM_LIMIT),
        name="fox_attention",
    )(fox_qkv, fox_qkv, fox_qkv, small, fbias, fnorm)


def _doubling_inverse(m_lows):
    c = m_lows[0].shape[0]
    row = lax.broadcasted_iota(jnp.int32, (c, c), 0)
    col = lax.broadcasted_iota(jnp.int32, (c, c), 1)
    eye = jnp.where(row == col, 1.0, 0.0)
    pair = (row >> 1) == (col >> 1)
    xs = [eye - jnp.where(pair, m, 0.0) for m in m_lows]
    size = 2
    while size < c:
        sh = size.bit_length() - 1
        rb = row >> sh
        below = ((rb & 1) == 1) & ((col >> sh) == rb - 1)
        xbfs = [x.astype(BF16) for x in xs]
        nxs = [_dot(jnp.where(below, m, 0.0).astype(BF16), xb).astype(BF16)
               for m, xb in zip(m_lows, xbfs)]
        xs = [x - _dot(xb, nx) for x, xb, nx in zip(xs, xbfs, nxs)]
        size *= 2
    return xs


def _gdn_kernel(q_ref, k_ref, v_ref, gate_ref, small_ref, alog_ref, dtb_ref, onorm_ref, o_ref,
                w2_ref, rr_ref, qp_ref, op_ref, gl_ref, st_ref, *, seq, heads, group):
    c = CHUNK
    d = GDN_HEAD_DIM
    rows = group * c
    n_groups = seq // rows
    n_chunks = seq // c
    lane = lax.broadcasted_iota(jnp.int32, (1, LANES), 1)
    row = lax.broadcasted_iota(jnp.int32, (c, c), 0)
    col = lax.broadcasted_iota(jnp.int32, (c, c), 1)
    tril = _tril_ones(rows, BF16, block=c)

    def prep(gi, _):
        r0 = pl.multiple_of(gi * rows, rows)
        qa = q_ref[pl.ds(r0, rows), :].astype(F32)
        ka = k_ref[pl.ds(r0, rows), :]
        va = v_ref[pl.ds(r0, rows), :].astype(F32)
        sm = small_ref[pl.ds(r0, rows), :]
        g_all = -jnp.exp(alog_ref[...]) * _softplus(sm + dtb_ref[...])
        gc_all = _cumsum_rows(tril, g_all)
        items = []
        for h in range(heads):
            qh = qa[:, h * d:(h + 1) * d]
            kh = ka[:, h * d:(h + 1) * d]
            vh = va[:, h * d:(h + 1) * d]
            beta_h = _sigmoid(jnp.sum(jnp.where(lane == SMALL_B0 + h, sm, 0.0),
                                      axis=-1, keepdims=True))
            gc_h = jnp.sum(jnp.where(lane == SMALL_A0 + h, gc_all, 0.0),
                           axis=-1, keepdims=True)
            for g in range(group):
                sl = slice(g * c, (g + 1) * c)
                items.append(dict(h=h, g=g, q=qh[sl], k=kh[sl].astype(F32), kbf=kh[sl], v=vh[sl],
                                  beta=beta_h[sl], gc=gc_h[sl]))
        for it in items:
            gc = it["gc"]
            hi, mid, lo = (t.astype(F32) for t in _split3(gc))
            it["al"] = jnp.where(lane == 0, hi, jnp.where(lane == 1, mid, jnp.where(
                lane == 2, lo, jnp.where(lane < 6, 1.0, 0.0)))).astype(BF16)
            it["ar"] = jnp.where(lane < 3, 1.0, jnp.where(lane == 3, -hi, jnp.where(
                lane == 4, -mid, jnp.where(lane == 5, -lo, 0.0)))).astype(BF16)
            it["kb"] = it["k"] * it["beta"]
        dmats = [_dot_nt(it["al"], it["ar"]) for it in items]
        kks = [_dot_nt(it["kb"].astype(BF16), it["kbf"]) for it in items]
        qks = [_dot_nt(it["q"].astype(BF16), it["kbf"]) for it in items]
        m_lows = []
        for it, dmat, kk, qk in zip(items, dmats, kks, qks):
            decay = jnp.exp(jnp.where(col <= row, dmat, -jnp.inf))
            m_lows.append(jnp.where(col < row, kk * decay, 0.0))
            it["a"] = (qk * decay).astype(BF16)
        t_invs = _doubling_inverse(m_lows)
        wus = []
        for it, t_inv in zip(items, t_invs):
            eg = jnp.exp(it["gc"])
            it["eg"] = eg
            rhs = jnp.concatenate(
                [(it["kb"] * eg).astype(BF16), (it["v"] * it["beta"]).astype(BF16)], axis=1)
            wus.append(_dot(t_inv.astype(BF16), rhs).astype(BF16))
        kwus, awus = [], []
        for it, wu in zip(items, wus):
            gc_last = it["gc"][c - 1:c, :]
            it["gl"] = jnp.exp(gc_last)
            kd = (it["k"] * jnp.exp(gc_last - it["gc"])).astype(BF16)
            kwus.append(_dot_tn(kd, wu))
            awus.append(_dot(it["a"], wu))
        for it, kwu, awu in zip(items, kwus, awus):
            h = it["h"]
            ci = gi * group + it["g"]
            rw = pl.multiple_of(ci * d, d)
            rc = pl.multiple_of(ci * c, c)
            w2_ref[h, pl.ds(rw, d), :] = kwu[:, :d].astype(BF16)
            rr_ref[h, pl.ds(rw, d), :] = kwu[:, d:]
            qp_ref[h, pl.ds(rc, c), :] = (it["q"] * it["eg"] - awu[:, :d]).astype(BF16)
            op_ref[h, pl.ds(rc, c), :] = awu[:, d:]
            gl_ref[h, pl.ds(pl.multiple_of(ci * 8, 8), 8), :] = jnp.broadcast_to(it["gl"], (8, d))
        return 0

    lax.fori_loop(0, n_groups, prep, 0)

    st_ref[...] = jnp.zeros_like(st_ref)

    def scan(ci, _):
        rc = pl.multiple_of(ci * c, c)
        rw = pl.multiple_of(ci * d, d)
        for h in range(heads):
            state = st_ref[h]
            lhs = jnp.concatenate([w2_ref[h, pl.ds(rw, d), :], qp_ref[h, pl.ds(rc, c), :]], axis=0)
            r = _dot(lhs, state.astype(BF16))
            gl = gl_ref[h, pl.ds(pl.multiple_of(ci * 8, 8), 8), :][0:1, :]
            st_ref[h] = state * gl - r[:d] + rr_ref[h, pl.ds(rw, d), :]
            o = r[d:] + op_ref[h, pl.ds(rc, c), :]
            gate = gate_ref[pl.ds(rc, c), h * d:(h + 1) * d].astype(F32)
            o_ref[pl.ds(rc, c), h * d:(h + 1) * d] = (
                _rms(o, onorm_ref[...]) * gate).astype(o_ref.dtype)
        return 0

    lax.fori_loop(0, n_chunks, scan, 0)


def _gdn(gdn_in, small, alog_vec, dtb_vec, onorm, batch, seq, group):
    n = batch * seq
    d = GDN_HEAD_DIM
    heads = N_GDN_HEADS
    n_chunks = seq // CHUNK
    kern = functools.partial(_gdn_kernel, seq=seq, heads=heads, group=group)
    col_block = lambda part: (lambda b: (b, part))
    const = lambda b: (0, 0)
    return pl.pallas_call(
        kern,
        grid=(batch,),
        in_specs=[
            pl.BlockSpec((seq, D_GDN), col_block(0)),
            pl.BlockSpec((seq, D_GDN), col_block(1)),
            pl.BlockSpec((seq, D_GDN), col_block(2)),
            pl.BlockSpec((seq, D_GDN), col_block(3)),
            pl.BlockSpec((seq, LANES), lambda b: (b, 0)),
            pl.BlockSpec((1, LANES), const),
            pl.BlockSpec((1, LANES), const),
            pl.BlockSpec((1, d), const),
        ],
        out_specs=pl.BlockSpec((seq, D_GDN), lambda b: (b, 0)),
        out_shape=jax.ShapeDtypeStruct((n, D_GDN), BF16),
        scratch_shapes=[
            pltpu.VMEM((heads, n_chunks * d, d), BF16),
            pltpu.VMEM((heads, n_chunks * d, d), F32),
            pltpu.VMEM((heads, seq, d), BF16),
            pltpu.VMEM((heads, seq, d), F32),
            pltpu.VMEM((heads, n_chunks * 8, d), F32),
            pltpu.VMEM((heads, d, d), F32),
        ],
        compiler_params=pltpu.CompilerParams(
            dimension_semantics=("arbitrary",), vmem_limit_bytes=VMEM_LIMIT),
        name="gated_deltanet",
    )(gdn_in, gdn_in, gdn_in, gdn_in, small, alog_vec, dtb_vec, onorm)


def _out_mlp_kernel(fox_ref, gdn_ref, x_ref, wof_ref, wog_ref, pmix_ref, pre_ref,
                    wup_ref, wdn_ref, post_ref, o_ref, *, ff_blk):
    mixed = _dot(fox_ref[...], wof_ref[...]) + _dot(gdn_ref[...], wog_ref[...])
    x1 = x_ref[...] + _rms(mixed, pmix_ref[...])
    h = _rms(x1, pre_ref[...]).astype(BF16)
    d_ff = wup_ref.shape[1]
    y = None
    for j in range(d_ff // ff_blk):
        a = _dot(h, wup_ref[:, j * ff_blk:(j + 1) * ff_blk])
        a = jnp.square(jnp.maximum(a, 0.0)).astype(BF16)
        t = _dot(a, wdn_ref[j * ff_blk:(j + 1) * ff_blk, :])
        y = t if y is None else y + t
    o_ref[...] = x1 + _rms(y, post_ref[...])


def _out_mlp(fox_o, gdn_o, x2, wof, wog, pmix, pre, wup, wdn, post, tm):
    n, d = x2.shape
    const = lambda i: (0, 0)
    row = lambda i: (i, 0)
    single = pl.Buffered(1)
    kern = functools.partial(_out_mlp_kernel, ff_blk=1024)
    return pl.pallas_call(
        kern,
        grid=(n // tm,),
        in_specs=[
            pl.BlockSpec((tm, fox_o.shape[1]), row),
            pl.BlockSpec((tm, gdn_o.shape[1]), row),
            pl.BlockSpec((tm, d), row),
            pl.BlockSpec(wof.shape, const, pipeline_mode=single),
            pl.BlockSpec(wog.shape, const, pipeline_mode=single),
            pl.BlockSpec((1, d), const),
            pl.BlockSpec((1, d), const),
            pl.BlockSpec(wup.shape, const, pipeline_mode=single),
            pl.BlockSpec(wdn.shape, const, pipeline_mode=single),
            pl.BlockSpec((1, d), const),
        ],
        out_specs=pl.BlockSpec((tm, d), row),
        out_shape=jax.ShapeDtypeStruct((n, d), F32),
        compiler_params=pltpu.CompilerParams(
            dimension_semantics=("arbitrary",), vmem_limit_bytes=VMEM_LIMIT),
        name="out_mlp",
    )(fox_o, gdn_o, x2, wof, wog, pmix, pre, wup, wdn, post)


def _lane_vec(values, offset):
    return jnp.zeros((1, LANES), F32).at[0, offset:offset + values.shape[0]].set(values.astype(F32))


def kernel(x, pre_mix_norm, w_in, fox_f_bias, fox_out_norm, gdn_conv_w, gdn_a_log, gdn_dt_bias,
           gdn_out_norm, w_out, post_mix_norm, pre_mlp_norm, w_up, w_down, post_mlp_norm):
    b, s, d = x.shape
    n = b * s
    x2 = x.reshape(n, d)

    o = 0
    parts = []
    for size in (D_FOX, D_FOX, D_FOX, N_FOX_HEADS, D_GDN, D_GDN, D_GDN, N_GDN_HEADS, N_GDN_HEADS, D_GDN):
        parts.append(w_in[:, o:o + size])
        o += size
    fq, fk, fv, ff, gq, gk, gv, gb, ga, gz = parts
    wf = jnp.concatenate([fq, fk, fv], axis=1).astype(BF16)
    wg = jnp.concatenate([gq, gk, gv, gz], axis=1).astype(BF16)
    ws = jnp.concatenate(
        [ff] * FOX_REP + [jnp.zeros((d, SMALL_B0 - FOX_REP * N_FOX_HEADS), w_in.dtype), gb, ga,
                          jnp.zeros((d, LANES - SMALL_A0 - N_GDN_HEADS), w_in.dtype)],
        axis=1).astype(BF16)

    fox_qkv, gdn_in, small = _in_proj(x2, pre_mix_norm.reshape(1, d).astype(F32), wf, wg, ws,
                                      gdn_conv_w.astype(F32), s, tm=1024)

    fbias = _lane_vec(jnp.tile(fox_f_bias, FOX_REP), 0)
    fnorm = jnp.tile(fox_out_norm.astype(F32), 2).reshape(1, LANES)
    fox_o = _fox(fox_qkv, small, fbias, fnorm, b, s, tq=512)

    alog_vec = _lane_vec(gdn_a_log, SMALL_A0)
    dtb_vec = _lane_vec(gdn_dt_bias, SMALL_A0)
    gdn_o = _gdn(gdn_in, small, alog_vec, dtb_vec,
                 gdn_out_norm.reshape(1, GDN_HEAD_DIM).astype(F32), b, s, group=4)

    wo = w_out.astype(BF16)
    out = _out_mlp(fox_o, gdn_o, x2, wo[:D_FOX], wo[D_FOX:],
                   post_mix_norm.reshape(1, d).astype(F32), pre_mlp_norm.reshape(1, d).astype(F32),
                   w_up.astype(BF16), w_down.astype(BF16), post_mlp_norm.reshape(1, d).astype(F32),
                   tm=512)
    return out.reshape(b, s, d)
```

```python
import functools

import jax
import jax.numpy as jnp
from jax import lax
from jax.experimental import pallas as pl
from jax.experimental.pallas import tpu as pltpu

F32 = jnp.float32
BF16 = jnp.bfloat16

EPS = 1e-6
LANES = 128
FOX_HEAD_DIM = 64
N_FOX_HEADS = 8
D_FOX = FOX_HEAD_DIM * N_FOX_HEADS
GDN_HEAD_DIM = 128
N_GDN_HEADS = 4
D_GDN = GDN_HEAD_DIM * N_GDN_HEADS
CHUNK = 64
CONV_K = 4

FOX_REP = 6
SMALL_B0 = 64
SMALL_A0 = SMALL_B0 + N_GDN_HEADS

VMEM_LIMIT = 56 * 1024 * 1024


def _dot(a, b):
    return jnp.dot(a, b, preferred_element_type=F32)


def _dot_nt(a, b):
    return lax.dot_general(a, b, (((1,), (1,)), ((), ())), preferred_element_type=F32)


def _dot_tn(a, b):
    return lax.dot_general(a, b, (((0,), (0,)), ((), ())), preferred_element_type=F32)


def _rms(x, w):
    return x * lax.rsqrt(jnp.mean(x * x, axis=-1, keepdims=True) + EPS) * w


def _split3(x):
    hi = x.astype(BF16)
    r1 = x - hi.astype(F32)
    mid = r1.astype(BF16)
    lo = (r1 - mid.astype(F32)).astype(BF16)
    return hi, mid, lo


def _tril_ones(n, dtype, block=None):
    r = lax.broadcasted_iota(jnp.int32, (n, n), 0)
    c = lax.broadcasted_iota(jnp.int32, (n, n), 1)
    keep = c <= r
    if block is not None:
        sh = block.bit_length() - 1
        keep = keep & ((r >> sh) == (c >> sh))
    return jnp.where(keep, 1.0, 0.0).astype(dtype)


def _cumsum_rows(tril_bf16, x):
    hi, mid, lo = _split3(x)
    return _dot(tril_bf16, hi) + _dot(tril_bf16, mid) + _dot(tril_bf16, lo)


def _sigmoid(x):
    return 0.5 + 0.5 * jnp.tanh(0.5 * x)


def _silu(x):
    h = 0.5 * x
    return h + h * jnp.tanh(h)


def _softplus(x):
    return jnp.maximum(x, 0.0) + jnp.log1p(jnp.exp(-jnp.abs(x)))


def _in_proj_kernel(x_ref, g_ref, wf_ref, wg_ref, ws_ref, cw_ref, fox_ref, gdn_ref, small_ref,
                    halo_ref, win_ref, *, tiles_per_seq):
    i = pl.program_id(0)
    d = GDN_HEAD_DIM
    h = _rms(x_ref[...], g_ref[...]).astype(BF16)
    tm = h.shape[0]
    first = i % tiles_per_seq == 0
    slab = 2 * d
    for sl in range(4 * D_GDN // slab):
        cols = slice(sl * slab, (sl + 1) * slab)
        g = _dot(h, wg_ref[:, cols])
        if sl * slab >= 3 * D_GDN:
            gdn_ref[:, cols] = _silu(g).astype(BF16)
            continue
        win_ref[0:8, :] = jnp.where(first, 0.0, halo_ref[:, cols])
        win_ref[8:, :] = g
        halo_ref[:, cols] = g[tm - 8:, :]
        w = cw_ref[:, cols]
        y = g * w[CONV_K - 1:CONV_K, :]
        for j in range(1, CONV_K):
            shifted = win_ref[8 - j:8 - j + tm, :]
            y = y + shifted * w[CONV_K - 1 - j:CONV_K - j, :]
        act = _silu(y)
        if sl * slab >= 2 * D_GDN:
            gdn_ref[:, cols] = act.astype(BF16)
            continue
        for hd in range(slab // d):
            t = act[:, hd * d:(hd + 1) * d]
            r = lax.rsqrt(jnp.sum(t * t, axis=-1, keepdims=True) + EPS)
            if sl * slab < D_GDN:
                r = r * (d ** -0.5)
            gdn_ref[:, sl * slab + hd * d:sl * slab + (hd + 1) * d] = (t * r).astype(BF16)
    fox_ref[...] = _dot(h, wf_ref[...]).astype(BF16)
    small_ref[...] = _dot(h, ws_ref[...])


def _in_proj(x2, gain, wf, wg, ws, conv_w, seq, tm):
    n, d = x2.shape
    const = lambda i: (0, 0)
    row = lambda i: (i, 0)
    kern = functools.partial(_in_proj_kernel, tiles_per_seq=seq // tm)
    return pl.pallas_call(
        kern,
        grid=(n // tm,),
        in_specs=[
            pl.BlockSpec((tm, d), row),
            pl.BlockSpec((1, d), const),
            pl.BlockSpec(wf.shape, const, pipeline_mode=pl.Buffered(1)),
            pl.BlockSpec(wg.shape, const, pipeline_mode=pl.Buffered(1)),
            pl.BlockSpec(ws.shape, const, pipeline_mode=pl.Buffered(1)),
            pl.BlockSpec(conv_w.shape, const),
        ],
        out_specs=[
            pl.BlockSpec((tm, wf.shape[1]), row),
            pl.BlockSpec((tm, wg.shape[1]), row),
            pl.BlockSpec((tm, ws.shape[1]), row),
        ],
        out_shape=[
            jax.ShapeDtypeStruct((n, wf.shape[1]), BF16),
            jax.ShapeDtypeStruct((n, wg.shape[1]), BF16),
            jax.ShapeDtypeStruct((n, ws.shape[1]), F32),
        ],
        scratch_shapes=[pltpu.VMEM((8, 3 * D_GDN), F32),
                        pltpu.VMEM((tm + 8, 2 * GDN_HEAD_DIM), F32)],
        compiler_params=pltpu.CompilerParams(
            dimension_semantics=("arbitrary",), vmem_limit_bytes=VMEM_LIMIT),
        name="in_proj",
    )(x2, gain, wf, wg, ws, conv_w)


def _fox_kernel(q_ref, k_ref, v_ref, small_ref, fbias_ref, fnorm_ref, o_ref,
                qx_ref, kx_ref, vaug_ref, *, seq, tq, tk, dg, cum_blk):
    p = pl.program_id(1)
    qi = pl.program_id(2)
    lane = lax.broadcasted_iota(jnp.int32, (1, LANES), 1)
    half = FOX_HEAD_DIM
    nh = N_FOX_HEADS
    head_mask = (lane < half, lane >= half)
    ones_lane = (half, 0)
    zero = jnp.zeros((), BF16)

    @pl.when((p == 0) & (qi == 0))
    def _per_sequence():
        tril = _tril_ones(cum_blk, BF16)
        carry = jnp.zeros((1, LANES), F32)
        for blk in range(seq // cum_blk):
            rows = slice(blk * cum_blk, (blk + 1) * cum_blk)
            z = small_ref[rows, :] + fbias_ref[...]
            c = _cumsum_rows(tril, -_softplus(-z)) + carry
            carry = c[cum_blk - 1:cum_blk, :]
            hi, mid, lo = (t.astype(F32) for t in _split3(c))
            qx_ref[rows, :] = jnp.where(lane < nh, hi, jnp.where(lane < 2 * nh, mid, jnp.where(
                lane < 3 * nh, lo, jnp.where(lane < 6 * nh, 1.0, 0.0)))).astype(BF16)
            kx_ref[rows, :] = jnp.where(lane < 3 * nh, 1.0, jnp.where(lane < 4 * nh, -hi, jnp.where(
                lane < 5 * nh, -mid, jnp.where(lane < 6 * nh, -lo, 0.0)))).astype(BF16)

    @pl.when(qi == 0)
    def _per_pair():
        vp = v_ref[...]
        for e in range(2):
            ones_col = jnp.where(lane == ones_lane[e], 1.0, 0.0).astype(BF16)
            vaug_ref[e] = jnp.where(head_mask[e], vp, ones_col)

    scale = FOX_HEAD_DIM ** -0.5
    r0 = pl.multiple_of(qi * tq, tq)
    qs = q_ref[...] * jnp.asarray(scale, BF16)
    qx = qx_ref[pl.ds(r0, tq), :]
    qaug = []
    for e in range(2):
        xmask = ((lane & (nh - 1)) == 2 * p + e) & (lane < 6 * nh)
        qaug.append(jnp.concatenate(
            [jnp.where(head_mask[e], qs, zero), jnp.where(xmask, qx, zero)], axis=1))

    def block(j0, width, row_lo, carry, masked):
        kb = jnp.concatenate([k_ref[pl.ds(j0, width), :], kx_ref[pl.ds(j0, width), :]], axis=1)
        nrows = tq - row_lo
        out = []
        for e in range(2):
            m_all, acc_all = carry[e]
            m, acc = m_all[row_lo:], acc_all[row_lo:]
            vb = vaug_ref[e, pl.ds(j0, width), :]
            s = _dot_nt(qaug[e][row_lo:], kb)
            if masked:
                rr = lax.broadcasted_iota(jnp.int32, (width, width), 0)
                cc = lax.broadcasted_iota(jnp.int32, (width, width), 1)
                top = jnp.where(cc <= rr, s[:width], -jnp.inf)
                s = top if nrows == width else jnp.concatenate([top, s[width:]], axis=0)
            m_new = jnp.maximum(m, jnp.max(s, axis=-1, keepdims=True))
            alpha = jnp.exp(m - m_new)
            pexp = jnp.exp(s - m_new)
            acc = alpha * acc + _dot(pexp.astype(BF16), vb)
            if row_lo:
                m_new = jnp.concatenate([m_all[:row_lo], m_new], axis=0)
                acc = jnp.concatenate([acc_all[:row_lo], acc], axis=0)
            out.append((m_new, acc))
        return tuple(out)

    init = tuple((jnp.full((tq, 1), -jnp.inf, F32), jnp.zeros((tq, LANES), F32)) for _ in range(2))
    carry = lax.fori_loop(
        0, qi * (tq // tk),
        lambda j, c: block(pl.multiple_of(j * tk, tk), tk, 0, c, False), init)
    for c in range(tq // dg):
        carry = block(pl.multiple_of(r0 + c * dg, dg), dg, c * dg, carry, True)

    outs = []
    for e in range(2):
        _, acc = carry[e]
        l = jnp.sum(jnp.where(lane == ones_lane[e], acc, 0.0), axis=-1, keepdims=True)
        outs.append(acc / l)
    o = jnp.where(head_mask[0], outs[0], outs[1])
    o2 = o * o
    ss0 = jnp.sum(jnp.where(head_mask[0], o2, 0.0), axis=-1, keepdims=True)
    ss1 = jnp.sum(jnp.where(head_mask[1], o2, 0.0), axis=-1, keepdims=True)
    ms = jnp.where(head_mask[0], ss0, ss1) * (1.0 / FOX_HEAD_DIM)
    o_ref[...] = (o * lax.rsqrt(ms + EPS) * fnorm_ref[...]).astype(o_ref.dtype)


def _fox(fox_qkv, small, fbias, fnorm, batch, seq, tq, tk, dg):
    n = batch * seq
    nq = seq // tq
    pairs = N_FOX_HEADS // 2
    kern = functools.partial(_fox_kernel, seq=seq, tq=tq, tk=tk, dg=dg, cum_blk=256)
    return pl.pallas_call(
        kern,
        grid=(batch, pairs, nq),
        in_specs=[
            pl.BlockSpec((tq, LANES), lambda b, p, i: (b * nq + i, p)),
            pl.BlockSpec((seq, LANES), lambda b, p, i: (b, pairs + p)),
            pl.BlockSpec((seq, LANES), lambda b, p, i: (b, 2 * pairs + p)),
            pl.BlockSpec((seq, LANES), lambda b, p, i: (b, 0)),
            pl.BlockSpec((1, LANES), lambda b, p, i: (0, 0)),
            pl.BlockSpec((1, LANES), lambda b, p, i: (0, 0)),
        ],
        out_specs=pl.BlockSpec((tq, LANES), lambda b, p, i: (b * nq + i, p)),
        out_shape=jax.ShapeDtypeStruct((n, D_FOX), BF16),
        scratch_shapes=[
            pltpu.VMEM((seq, LANES), BF16),
            pltpu.VMEM((seq, LANES), BF16),
            pltpu.VMEM((2, seq, LANES), BF16),
        ],
        compiler_params=pltpu.CompilerParams(
            dimension_semantics=("arbitrary", "arbitrary", "arbitrary"),
            vmem_limit_bytes=VMEM_LIMIT),
        name="fox_attention",
    )(fox_qkv, fox_qkv, fox_qkv, small, fbias, fnorm)


def _doubling_inverse(m_lows):
    c = m_lows[0].shape[0]
    row = lax.broadcasted_iota(jnp.int32, (c, c), 0)
    col = lax.broadcasted_iota(jnp.int32, (c, c), 1)
    eye = jnp.where(row == col, 1.0, 0.0)
    pair = (row >> 1) == (col >> 1)
    xs = [eye - jnp.where(pair, m, 0.0) for m in m_lows]
    size = 2
    while size < c:
        sh = size.bit_length() - 1
        rb = row >> sh
        below = ((rb & 1) == 1) & ((col >> sh) == rb - 1)
        xbfs = [x.astype(BF16) for x in xs]
        nxs = [_dot(jnp.where(below, m, 0.0).astype(BF16), xb).astype(BF16)
               for m, xb in zip(m_lows, xbfs)]
        xs = [x - _dot(xb, nx) for x, xb, nx in zip(xs, xbfs, nxs)]
        size *= 2
        yield
    return xs


def _gdn_kernel(q_ref, k_ref, v_ref, gate_ref, small_ref, alog_ref, dtb_ref, onorm_ref, o_ref,
                w2_ref, rr_ref, qp_ref, op_ref, gl_ref, st_ref, *, seq, heads, group):
    c = CHUNK
    d = GDN_HEAD_DIM
    rows = group * c
    n_groups = seq // rows
    n_chunks = seq // c
    lane = lax.broadcasted_iota(jnp.int32, (1, LANES), 1)
    row = lax.broadcasted_iota(jnp.int32, (c, c), 0)
    col = lax.broadcasted_iota(jnp.int32, (c, c), 1)
    tril = _tril_ones(rows, BF16, block=c)

    def aligned(x, m):
        return x if isinstance(x, int) else pl.multiple_of(x, m)

    def prep_stages(gi):
        r0 = aligned(gi * rows, rows)
        qa = q_ref[pl.ds(r0, rows), :].astype(F32)
        ka = k_ref[pl.ds(r0, rows), :]
        va = v_ref[pl.ds(r0, rows), :].astype(F32)
        sm = small_ref[pl.ds(r0, rows), :]
        g_all = -jnp.exp(alog_ref[...]) * _softplus(sm + dtb_ref[...])
        gc_all = _cumsum_rows(tril, g_all)
        items = []
        for h in range(heads):
            qh = qa[:, h * d:(h + 1) * d]
            kh = ka[:, h * d:(h + 1) * d]
            vh = va[:, h * d:(h + 1) * d]
            beta_h = _sigmoid(jnp.sum(jnp.where(lane == SMALL_B0 + h, sm, 0.0),
                                      axis=-1, keepdims=True))
            gc_h = jnp.sum(jnp.where(lane == SMALL_A0 + h, gc_all, 0.0),
                           axis=-1, keepdims=True)
            for g in range(group):
                sl = slice(g * c, (g + 1) * c)
                items.append(dict(h=h, g=g, q=qh[sl], k=kh[sl].astype(F32), kbf=kh[sl], v=vh[sl],
                                  beta=beta_h[sl], gc=gc_h[sl]))
        for it in items:
            gc = it["gc"]
            hi, mid, lo = (t.astype(F32) for t in _split3(gc))
            it["al"] = jnp.where(lane == 0, hi, jnp.where(lane == 1, mid, jnp.where(
                lane == 2, lo, jnp.where(lane < 6, 1.0, 0.0)))).astype(BF16)
            it["ar"] = jnp.where(lane < 3, 1.0, jnp.where(lane == 3, -hi, jnp.where(
                lane == 4, -mid, jnp.where(lane == 5, -lo, 0.0)))).astype(BF16)
            it["kb"] = it["k"] * it["beta"]
        dmats = [_dot_nt(it["al"], it["ar"]) for it in items]
        kks = [_dot_nt(it["kb"].astype(BF16), it["kbf"]) for it in items]
        qks = [_dot_nt(it["q"].astype(BF16), it["kbf"]) for it in items]
        m_lows = []
        for it, dmat, kk, qk in zip(items, dmats, kks, qks):
            decay = jnp.exp(jnp.where(col <= row, dmat, -jnp.inf))
            m_lows.append(jnp.where(col < row, kk * decay, 0.0))
            it["a"] = (qk * decay).astype(BF16)
        yield
        t_invs = yield from _doubling_inverse(m_lows)
        wus = []
        for it, t_inv in zip(items, t_invs):
            eg = jnp.exp(it["gc"])
            it["eg"] = eg
            rhs = jnp.concatenate(
                [(it["kb"] * eg).astype(BF16), (it["v"] * it["beta"]).astype(BF16)], axis=1)
            wus.append(_dot(t_inv.astype(BF16), rhs).astype(BF16))
        yield
        kwus, awus = [], []
        for it, wu in zip(items, wus):
            gc_last = it["gc"][c - 1:c, :]
            it["gl"] = jnp.exp(gc_last)
            kd = (it["k"] * jnp.exp(gc_last - it["gc"])).astype(BF16)
            kwus.append(_dot_tn(kd, wu))
            awus.append(_dot(it["a"], wu))
        for it, kwu, awu in zip(items, kwus, awus):
            h = it["h"]
            ci = gi * group + it["g"]
            rw = aligned(ci * d, d)
            rc = aligned(ci * c, c)
            w2_ref[h, pl.ds(rw, d), :] = kwu[:, :d].astype(BF16)
            rr_ref[h, pl.ds(rw, d), :] = kwu[:, d:]
            qp_ref[h, pl.ds(rc, c), :] = (it["q"] * it["eg"] - awu[:, :d]).astype(BF16)
            op_ref[h, pl.ds(rc, c), :] = awu[:, d:]
            gl_ref[h, pl.ds(aligned(ci * 8, 8), 8), :] = jnp.broadcast_to(it["gl"], (8, d))

    def scan_chunk(ci):
        rc = aligned(ci * c, c)
        rw = aligned(ci * d, d)
        for h in range(heads):
            state = st_ref[h]
            lhs = jnp.concatenate([w2_ref[h, pl.ds(rw, d), :], qp_ref[h, pl.ds(rc, c), :]], axis=0)
            r = _dot(lhs, state.astype(BF16))
            gl = gl_ref[h, pl.ds(aligned(ci * 8, 8), 8), :][0:1, :]
            st_ref[h] = state * gl - r[:d] + rr_ref[h, pl.ds(rw, d), :]
            o = r[d:] + op_ref[h, pl.ds(rc, c), :]
            gate = gate_ref[pl.ds(rc, c), h * d:(h + 1) * d].astype(F32)
            o_ref[pl.ds(rc, c), h * d:(h + 1) * d] = (
                _rms(o, onorm_ref[...]) * gate).astype(o_ref.dtype)

    def prep_and_scan(gi, scan_group):
        pending = [] if scan_group is None else [scan_group * group + j for j in range(group)]
        for _ in prep_stages(gi):
            if pending:
                scan_chunk(pending.pop(0))
        for ci in pending:
            scan_chunk(ci)

    st_ref[...] = jnp.zeros_like(st_ref)
    prep_and_scan(0, None)

    def body(gi, _):
        prep_and_scan(gi, gi - 1)
        return 0

    lax.fori_loop(1, n_groups, body, 0)
    for j in range(group):
        scan_chunk((n_groups - 1) * group + j)


def _gdn(gdn_in, small, alog_vec, dtb_vec, onorm, batch, seq, group):
    n = batch * seq
    d = GDN_HEAD_DIM
    heads = N_GDN_HEADS
    n_chunks = seq // CHUNK
    kern = functools.partial(_gdn_kernel, seq=seq, heads=heads, group=group)
    col_block = lambda part: (lambda b: (b, part))
    const = lambda b: (0, 0)
    return pl.pallas_call(
        kern,
        grid=(batch,),
        in_specs=[
            pl.BlockSpec((seq, D_GDN), col_block(0)),
            pl.BlockSpec((seq, D_GDN), col_block(1)),
            pl.BlockSpec((seq, D_GDN), col_block(2)),
            pl.BlockSpec((seq, D_GDN), col_block(3)),
            pl.BlockSpec((seq, LANES), lambda b: (b, 0)),
            pl.BlockSpec((1, LANES), const),
            pl.BlockSpec((1, LANES), const),
            pl.BlockSpec((1, d), const),
        ],
        out_specs=pl.BlockSpec((seq, D_GDN), lambda b: (b, 0)),
        out_shape=jax.ShapeDtypeStruct((n, D_GDN), BF16),
        scratch_shapes=[
            pltpu.VMEM((heads, n_chunks * d, d), BF16),
            pltpu.VMEM((heads, n_chunks * d, d), F32),
            pltpu.VMEM((heads, seq, d), BF16),
            pltpu.VMEM((heads, seq, d), F32),
            pltpu.VMEM((heads, n_chunks * 8, d), F32),
            pltpu.VMEM((heads, d, d), F32),
        ],
        compiler_params=pltpu.CompilerParams(
            dimension_semantics=("arbitrary",), vmem_limit_bytes=VMEM_LIMIT),
        name="gated_deltanet",
    )(gdn_in, gdn_in, gdn_in, gdn_in, small, alog_vec, dtb_vec, onorm)


def _out_mlp_kernel(fox_ref, gdn_ref, x_ref, wof_ref, wog_ref, pmix_ref, pre_ref,
                    wup_ref, wdn_ref, post_ref, o_ref, *, ff_blk):
    mixed = _dot(fox_ref[...], wof_ref[...]) + _dot(gdn_ref[...], wog_ref[...])
    x1 = x_ref[...] + _rms(mixed, pmix_ref[...])
    h = _rms(x1, pre_ref[...]).astype(BF16)
    d_ff = wup_ref.shape[1]
    y = None
    for j in range(d_ff // ff_blk):
        a = _dot(h, wup_ref[:, j * ff_blk:(j + 1) * ff_blk])
        a = jnp.square(jnp.maximum(a, 0.0)).astype(BF16)
        t = _dot(a, wdn_ref[j * ff_blk:(j + 1) * ff_blk, :])
        y = t if y is None else y + t
    o_ref[...] = x1 + _rms(y, post_ref[...])


def _out_mlp(fox_o, gdn_o, x2, wof, wog, pmix, pre, wup, wdn, post, tm):
    n, d = x2.shape
    const = lambda i: (0, 0)
    row = lambda i: (i, 0)
    single = pl.Buffered(1)
    kern = functools.partial(_out_mlp_kernel, ff_blk=1024)
    return pl.pallas_call(
        kern,
        grid=(n // tm,),
        in_specs=[
            pl.BlockSpec((tm, fox_o.shape[1]), row),
            pl.BlockSpec((tm, gdn_o.shape[1]), row),
            pl.BlockSpec((tm, d), row),
            pl.BlockSpec(wof.shape, const, pipeline_mode=single),
            pl.BlockSpec(wog.shape, const, pipeline_mode=single),
            pl.BlockSpec((1, d), const),
            pl.BlockSpec((1, d), const),
            pl.BlockSpec(wup.shape, const, pipeline_mode=single),
            pl.BlockSpec(wdn.shape, const, pipeline_mode=single),
            pl.BlockSpec((1, d), const),
        ],
        out_specs=pl.BlockSpec((tm, d), row),
        out_shape=jax.ShapeDtypeStruct((n, d), F32),
        compiler_params=pltpu.CompilerParams(
            dimension_semantics=("arbitrary",), vmem_limit_bytes=VMEM_LIMIT),
        name="out_mlp",
    )(fox_o, gdn_o, x2, wof, wog, pmix, pre, wup, wdn, post)


def _lane_vec(values, offset):
    return jnp.zeros((1, LANES), F32).at[0, offset:offset + values.shape[0]].set(values.astype(F32))


def kernel(x, pre_mix_norm, w_in, fox_f_bias, fox_out_norm, gdn_conv_w, gdn_a_log, gdn_dt_bias,
           gdn_out_norm, w_out, post_mix_norm, pre_mlp_norm, w_up, w_down, post_mlp_norm):
    b, s, d = x.shape
    n = b * s
    x2 = x.reshape(n, d)

    o = 0
    parts = []
    for size in (D_FOX, D_FOX, D_FOX, N_FOX_HEADS, D_GDN, D_GDN, D_GDN, N_GDN_HEADS, N_GDN_HEADS, D_GDN):
        parts.append(w_in[:, o:o + size])
        o += size
    fq, fk, fv, ff, gq, gk, gv, gb, ga, gz = parts
    wf = jnp.concatenate([fq, fk, fv], axis=1).astype(BF16)
    wg = jnp.concatenate([gq, gk, gv, gz], axis=1).astype(BF16)
    ws = jnp.concatenate(
        [ff] * FOX_REP + [jnp.zeros((d, SMALL_B0 - FOX_REP * N_FOX_HEADS), w_in.dtype), gb, ga,
                          jnp.zeros((d, LANES - SMALL_A0 - N_GDN_HEADS), w_in.dtype)],
        axis=1).astype(BF16)

    fox_qkv, gdn_in, small = _in_proj(x2, pre_mix_norm.reshape(1, d).astype(F32), wf, wg, ws,
                                      gdn_conv_w.astype(F32), s, tm=1024)

    fbias = _lane_vec(jnp.tile(fox_f_bias, FOX_REP), 0)
    fnorm = jnp.tile(fox_out_norm.astype(F32), 2).reshape(1, LANES)
    fox_o = _fox(fox_qkv, small, fbias, fnorm, b, s, tq=1024, tk=512, dg=512)

    alog_vec = _lane_vec(gdn_a_log, SMALL_A0)
    dtb_vec = _lane_vec(gdn_dt_bias, SMALL_A0)
    gdn_o = _gdn(gdn_in, small, alog_vec, dtb_vec,
                 gdn_out_norm.reshape(1, GDN_HEAD_DIM).astype(F32), b, s, group=4)

    wo = w_out.astype(BF16)
    out = _out_mlp(fox_o, gdn_o, x2, wo[:D_FOX], wo[D_FOX:],
                   post_mix_norm.reshape(1, d).astype(F32), pre_mlp_norm.reshape(1, d).astype(F32),
                   w_up.astype(BF16), w_down.astype(BF16), post_mlp_norm.reshape(1, d).astype(F32),
                   tm=512)
    return out.reshape(b, s, d)
```

```python
import functools

import jax
import jax.numpy as jnp
from jax import lax
from jax.experimental import pallas as pl
from jax.experimental.pallas import tpu as pltpu

F32 = jnp.float32
BF16 = jnp.bfloat16

EPS = 1e-6
LOG2E = 1.4426950408889634
LANES = 128
FOX_HEAD_DIM = 64
N_FOX_HEADS = 8
D_FOX = FOX_HEAD_DIM * N_FOX_HEADS
GDN_HEAD_DIM = 128
N_GDN_HEADS = 4
D_GDN = GDN_HEAD_DIM * N_GDN_HEADS
CHUNK = 64
CONV_K = 4

FOX_REP = 6
SMALL_B0 = 64
SMALL_A0 = SMALL_B0 + N_GDN_HEADS

VMEM_LIMIT = 56 * 1024 * 1024


def _dot(a, b):
    return jnp.dot(a, b, preferred_element_type=F32)


def _dot_nt(a, b):
    return lax.dot_general(a, b, (((1,), (1,)), ((), ())), preferred_element_type=F32)


def _dot_tn(a, b):
    return lax.dot_general(a, b, (((0,), (0,)), ((), ())), preferred_element_type=F32)


def _rms(x, w):
    return x * lax.rsqrt(jnp.mean(x * x, axis=-1, keepdims=True) + EPS) * w


def _split3(x):
    hi = x.astype(BF16)
    r1 = x - hi.astype(F32)
    mid = r1.astype(BF16)
    lo = (r1 - mid.astype(F32)).astype(BF16)
    return hi, mid, lo


def _tril_ones(n, dtype, block=None):
    r = lax.broadcasted_iota(jnp.int32, (n, n), 0)
    c = lax.broadcasted_iota(jnp.int32, (n, n), 1)
    keep = c <= r
    if block is not None:
        sh = block.bit_length() - 1
        keep = keep & ((r >> sh) == (c >> sh))
    return jnp.where(keep, 1.0, 0.0).astype(dtype)


def _cumsum_rows(tril_bf16, x):
    hi, mid, lo = _split3(x)
    return _dot(tril_bf16, hi) + _dot(tril_bf16, mid) + _dot(tril_bf16, lo)


def _sigmoid(x):
    return 0.5 + 0.5 * jnp.tanh(0.5 * x)


def _silu(x):
    h = 0.5 * x
    return h + h * jnp.tanh(h)


def _softplus(x):
    return jnp.maximum(x, 0.0) + jnp.log1p(jnp.exp(-jnp.abs(x)))


def _in_proj_kernel(x_ref, g_ref, wf_ref, wg_ref, ws_ref, cw_ref, fox_ref, gdn_ref, small_ref,
                    halo_ref, win_ref, *, tiles_per_seq):
    i = pl.program_id(0)
    d = GDN_HEAD_DIM
    h = _rms(x_ref[...], g_ref[...]).astype(BF16)
    tm = h.shape[0]
    first = i % tiles_per_seq == 0
    slab = 2 * d
    for sl in range(4 * D_GDN // slab):
        cols = slice(sl * slab, (sl + 1) * slab)
        if sl * slab < 3 * D_FOX:
            fox_ref[:, cols] = _dot(h, wf_ref[:, cols]).astype(BF16)
        g = _dot(h, wg_ref[:, cols])
        if sl * slab >= 3 * D_GDN:
            gdn_ref[:, cols] = _silu(g).astype(BF16)
            continue
        win_ref[0:8, :] = jnp.where(first, 0.0, halo_ref[:, cols])
        win_ref[8:, :] = g
        halo_ref[:, cols] = g[tm - 8:, :]
        w = cw_ref[:, cols]
        y = g * w[CONV_K - 1:CONV_K, :]
        for j in range(1, CONV_K):
            shifted = win_ref[8 - j:8 - j + tm, :]
            y = y + shifted * w[CONV_K - 1 - j:CONV_K - j, :]
        act = _silu(y)
        if sl * slab >= 2 * D_GDN:
            gdn_ref[:, cols] = act.astype(BF16)
            continue
        for hd in range(slab // d):
            t = act[:, hd * d:(hd + 1) * d]
            r = lax.rsqrt(jnp.sum(t * t, axis=-1, keepdims=True) + EPS)
            if sl * slab < D_GDN:
                r = r * (d ** -0.5)
            gdn_ref[:, sl * slab + hd * d:sl * slab + (hd + 1) * d] = (t * r).astype(BF16)
    small_ref[...] = _dot(h, ws_ref[...])


def _in_proj(x2, gain, wf, wg, ws, conv_w, seq, tm):
    n, d = x2.shape
    const = lambda i: (0, 0)
    row = lambda i: (i, 0)
    kern = functools.partial(_in_proj_kernel, tiles_per_seq=seq // tm)
    return pl.pallas_call(
        kern,
        grid=(n // tm,),
        in_specs=[
            pl.BlockSpec((tm, d), row),
            pl.BlockSpec((1, d), const),
            pl.BlockSpec(wf.shape, const, pipeline_mode=pl.Buffered(1)),
            pl.BlockSpec(wg.shape, const, pipeline_mode=pl.Buffered(1)),
            pl.BlockSpec(ws.shape, const, pipeline_mode=pl.Buffered(1)),
            pl.BlockSpec(conv_w.shape, const),
        ],
        out_specs=[
            pl.BlockSpec((tm, wf.shape[1]), row),
            pl.BlockSpec((tm, wg.shape[1]), row),
            pl.BlockSpec((tm, ws.shape[1]), row),
        ],
        out_shape=[
            jax.ShapeDtypeStruct((n, wf.shape[1]), BF16),
            jax.ShapeDtypeStruct((n, wg.shape[1]), BF16),
            jax.ShapeDtypeStruct((n, ws.shape[1]), F32),
        ],
        scratch_shapes=[pltpu.VMEM((8, 3 * D_GDN), F32),
                        pltpu.VMEM((tm + 8, 2 * GDN_HEAD_DIM), F32)],
        compiler_params=pltpu.CompilerParams(
            dimension_semantics=("arbitrary",), vmem_limit_bytes=VMEM_LIMIT),
        name="in_proj",
    )(x2, gain, wf, wg, ws, conv_w)


def _fox_kernel(q_ref, k_ref, v_ref, small_ref, fbias_ref, fnorm_ref, o_ref,
                qx_ref, kx_ref, vaug_ref, *, seq, tq, tk, dg, cum_blk):
    p = pl.program_id(1)
    qi = pl.program_id(2)
    lane = lax.broadcasted_iota(jnp.int32, (1, LANES), 1)
    half = FOX_HEAD_DIM
    nh = N_FOX_HEADS
    head_mask = (lane < half, lane >= half)
    ones_lane = (half, 0)
    zero = jnp.zeros((), BF16)

    @pl.when((p == 0) & (qi == 0))
    def _per_sequence():
        tril = _tril_ones(cum_blk, BF16)
        carry = jnp.zeros((1, LANES), F32)
        for blk in range(seq // cum_blk):
            rows = slice(blk * cum_blk, (blk + 1) * cum_blk)
            z = small_ref[rows, :] + fbias_ref[...]
            c = _cumsum_rows(tril, -_softplus(-z)) + carry
            carry = c[cum_blk - 1:cum_blk, :]
            hi, mid, lo = (t.astype(F32) for t in _split3(c * LOG2E))
            qx_ref[rows, :] = jnp.where(lane < nh, hi, jnp.where(lane < 2 * nh, mid, jnp.where(
                lane < 3 * nh, lo, jnp.where(lane < 6 * nh, 1.0, 0.0)))).astype(BF16)
            kx_ref[rows, :] = jnp.where(lane < 3 * nh, 1.0, jnp.where(lane < 4 * nh, -hi, jnp.where(
                lane < 5 * nh, -mid, jnp.where(lane < 6 * nh, -lo, 0.0)))).astype(BF16)

    @pl.when(qi == 0)
    def _per_pair():
        vp = v_ref[...]
        for e in range(2):
            ones_col = jnp.where(lane == ones_lane[e], 1.0, 0.0).astype(BF16)
            vaug_ref[e] = jnp.where(head_mask[e], vp, ones_col)

    r0 = pl.multiple_of(qi * tq, tq)
    qs = q_ref[...]
    qx = qx_ref[pl.ds(r0, tq), :]
    qaug = []
    for e in range(2):
        xmask = ((lane & (nh - 1)) == 2 * p + e) & (lane < 6 * nh)
        qaug.append(jnp.concatenate(
            [jnp.where(head_mask[e], qs, zero), jnp.where(xmask, qx, zero)], axis=1))

    def block(j0, width, row_lo, carry, masked):
        kb = jnp.concatenate([k_ref[pl.ds(j0, width), :], kx_ref[pl.ds(j0, width), :]], axis=1)
        nrows = tq - row_lo
        out = []
        for e in range(2):
            m_all, acc_all = carry[e]
            m, acc = m_all[row_lo:], acc_all[row_lo:]
            vb = vaug_ref[e, pl.ds(j0, width), :]
            s = _dot_nt(qaug[e][row_lo:], kb)
            if masked:
                rr = lax.broadcasted_iota(jnp.int32, (width, width), 0)
                cc = lax.broadcasted_iota(jnp.int32, (width, width), 1)
                top = jnp.where(cc <= rr, s[:width], -jnp.inf)
                s = top if nrows == width else jnp.concatenate([top, s[width:]], axis=0)
            m_new = jnp.maximum(m, jnp.max(s, axis=-1, keepdims=True))
            alpha = jnp.exp2(m - m_new)
            pexp = jnp.exp2(s - m_new)
            acc = alpha * acc + _dot(pexp.astype(BF16), vb)
            if row_lo:
                m_new = jnp.concatenate([m_all[:row_lo], m_new], axis=0)
                acc = jnp.concatenate([acc_all[:row_lo], acc], axis=0)
            out.append((m_new, acc))
        return tuple(out)

    init = tuple((jnp.full((tq, 1), -jnp.inf, F32), jnp.zeros((tq, LANES), F32)) for _ in range(2))
    carry = lax.fori_loop(
        0, qi * (tq // tk),
        lambda j, c: block(pl.multiple_of(j * tk, tk), tk, 0, c, False), init)
    for c in range(tq // dg):
        carry = block(pl.multiple_of(r0 + c * dg, dg), dg, c * dg, carry, True)

    outs = []
    for e in range(2):
        _, acc = carry[e]
        l = jnp.sum(jnp.where(lane == ones_lane[e], acc, 0.0), axis=-1, keepdims=True)
        outs.append(acc / l)
    o = jnp.where(head_mask[0], outs[0], outs[1])
    o2 = o * o
    ss0 = jnp.sum(jnp.where(head_mask[0], o2, 0.0), axis=-1, keepdims=True)
    ss1 = jnp.sum(jnp.where(head_mask[1], o2, 0.0), axis=-1, keepdims=True)
    ms = jnp.where(head_mask[0], ss0, ss1) * (1.0 / FOX_HEAD_DIM)
    o_ref[...] = (o * lax.rsqrt(ms + EPS) * fnorm_ref[...]).astype(o_ref.dtype)


def _fox(fox_qkv, small, fbias, fnorm, batch, seq, tq, tk, dg):
    n = batch * seq
    nq = seq // tq
    pairs = N_FOX_HEADS // 2
    kern = functools.partial(_fox_kernel, seq=seq, tq=tq, tk=tk, dg=dg, cum_blk=256)
    return pl.pallas_call(
        kern,
        grid=(batch, pairs, nq),
        in_specs=[
            pl.BlockSpec((tq, LANES), lambda b, p, i: (b * nq + i, p)),
            pl.BlockSpec((seq, LANES), lambda b, p, i: (b, pairs + p)),
            pl.BlockSpec((seq, LANES), lambda b, p, i: (b, 2 * pairs + p)),
            pl.BlockSpec((seq, LANES), lambda b, p, i: (b, 0)),
            pl.BlockSpec((1, LANES), lambda b, p, i: (0, 0)),
            pl.BlockSpec((1, LANES), lambda b, p, i: (0, 0)),
        ],
        out_specs=pl.BlockSpec((tq, LANES), lambda b, p, i: (b * nq + i, p)),
        out_shape=jax.ShapeDtypeStruct((n, D_FOX), BF16),
        scratch_shapes=[
            pltpu.VMEM((seq, LANES), BF16),
            pltpu.VMEM((seq, LANES), BF16),
            pltpu.VMEM((2, seq, LANES), BF16),
        ],
        compiler_params=pltpu.CompilerParams(
            dimension_semantics=("arbitrary", "arbitrary", "arbitrary"),
            vmem_limit_bytes=VMEM_LIMIT),
        name="fox_attention",
    )(fox_qkv, fox_qkv, fox_qkv, small, fbias, fnorm)


def _doubling_inverse(m_lows):
    c = m_lows[0].shape[0]
    row = lax.broadcasted_iota(jnp.int32, (c, c), 0)
    col = lax.broadcasted_iota(jnp.int32, (c, c), 1)
    eye = jnp.where(row == col, 1.0, 0.0)
    pair = (row >> 1) == (col >> 1)
    xs = [eye - jnp.where(pair, m, 0.0) for m in m_lows]
    size = 2
    while size < c:
        sh = size.bit_length() - 1
        rb = row >> sh
        below = ((rb & 1) == 1) & ((col >> sh) == rb - 1)
        xbfs = [x.astype(BF16) for x in xs]
        nxs = [_dot(jnp.where(below, m, 0.0).astype(BF16), xb).astype(BF16)
               for m, xb in zip(m_lows, xbfs)]
        xs = [x - _dot(xb, nx) for x, xb, nx in zip(xs, xbfs, nxs)]
        size *= 2
        yield
    return xs


def _gdn_kernel(q_ref, k_ref, v_ref, gate_ref, small_ref, alog_ref, dtb_ref, onorm_ref, o_ref,
                w2_ref, rr_ref, qp_ref, op_ref, gl_ref, st_ref, *, seq, heads, group):
    c = CHUNK
    d = GDN_HEAD_DIM
    rows = group * c
    n_groups = seq // rows
    n_chunks = seq // c
    lane = lax.broadcasted_iota(jnp.int32, (1, LANES), 1)
    row = lax.broadcasted_iota(jnp.int32, (c, c), 0)
    col = lax.broadcasted_iota(jnp.int32, (c, c), 1)
    tril = _tril_ones(rows, BF16, block=c)

    def aligned(x, m):
        return x if isinstance(x, int) else pl.multiple_of(x, m)

    def prep_stages(gi):
        r0 = aligned(gi * rows, rows)
        qa = q_ref[pl.ds(r0, rows), :].astype(F32)
        ka = k_ref[pl.ds(r0, rows), :]
        va = v_ref[pl.ds(r0, rows), :].astype(F32)
        sm = small_ref[pl.ds(r0, rows), :]
        g_all = -jnp.exp(alog_ref[...]) * _softplus(sm + dtb_ref[...])
        gc_all = _cumsum_rows(tril, g_all)
        items = []
        for h in range(heads):
            qh = qa[:, h * d:(h + 1) * d]
            kh = ka[:, h * d:(h + 1) * d]
            vh = va[:, h * d:(h + 1) * d]
            beta_h = _sigmoid(jnp.sum(jnp.where(lane == SMALL_B0 + h, sm, 0.0),
                                      axis=-1, keepdims=True))
            gc_h = jnp.sum(jnp.where(lane == SMALL_A0 + h, gc_all, 0.0),
                           axis=-1, keepdims=True)
            for g in range(group):
                sl = slice(g * c, (g + 1) * c)
                items.append(dict(h=h, g=g, q=qh[sl], k=kh[sl].astype(F32), kbf=kh[sl], v=vh[sl],
                                  beta=beta_h[sl], gc=gc_h[sl]))
        for it in items:
            gc = it["gc"]
            hi, mid, lo = (t.astype(F32) for t in _split3(gc))
            it["al"] = jnp.where(lane == 0, hi, jnp.where(lane == 1, mid, jnp.where(
                lane == 2, lo, jnp.where(lane < 6, 1.0, 0.0)))).astype(BF16)
            it["ar"] = jnp.where(lane < 3, 1.0, jnp.where(lane == 3, -hi, jnp.where(
                lane == 4, -mid, jnp.where(lane == 5, -lo, 0.0)))).astype(BF16)
            it["kb"] = it["k"] * it["beta"]
        dmats = [_dot_nt(it["al"], it["ar"]) for it in items]
        kks = [_dot_nt(it["kb"].astype(BF16), it["kbf"]) for it in items]
        qks = [_dot_nt(it["q"].astype(BF16), it["kbf"]) for it in items]
        m_lows = []
        for it, dmat, kk, qk in zip(items, dmats, kks, qks):
            decay = jnp.exp(jnp.where(col <= row, dmat, -jnp.inf))
            m_lows.append(jnp.where(col < row, kk * decay, 0.0))
            it["a"] = (qk * decay).astype(BF16)
        yield
        t_invs = yield from _doubling_inverse(m_lows)
        wus = []
        for it, t_inv in zip(items, t_invs):
            eg = jnp.exp(it["gc"])
            it["eg"] = eg
            rhs = jnp.concatenate(
                [(it["kb"] * eg).astype(BF16), (it["v"] * it["beta"]).astype(BF16)], axis=1)
            wus.append(_dot(t_inv.astype(BF16), rhs).astype(BF16))
        yield
        kwus, awus = [], []
        for it, wu in zip(items, wus):
            gc_last = it["gc"][c - 1:c, :]
            it["gl"] = jnp.exp(gc_last)
            kd = (it["k"] * jnp.exp(gc_last - it["gc"])).astype(BF16)
            kwus.append(_dot_tn(kd, wu))
            awus.append(_dot(it["a"], wu))
        for it, kwu, awu in zip(items, kwus, awus):
            h = it["h"]
            ci = gi * group + it["g"]
            rw = aligned(ci * d, d)
            rc = aligned(ci * c, c)
            w2_ref[h, pl.ds(rw, d), :] = kwu[:, :d].astype(BF16)
            rr_ref[h, pl.ds(rw, d), :] = kwu[:, d:]
            qp_ref[h, pl.ds(rc, c), :] = (it["q"] * it["eg"] - awu[:, :d]).astype(BF16)
            op_ref[h, pl.ds(rc, c), :] = awu[:, d:]
            gl_ref[h, pl.ds(aligned(ci * 8, 8), 8), :] = jnp.broadcast_to(it["gl"], (8, d))

    def scan_chunk(ci):
        rc = aligned(ci * c, c)
        rw = aligned(ci * d, d)
        for h in range(heads):
            state = st_ref[h]
            lhs = jnp.concatenate([w2_ref[h, pl.ds(rw, d), :], qp_ref[h, pl.ds(rc, c), :]], axis=0)
            r = _dot(lhs, state.astype(BF16))
            gl = gl_ref[h, pl.ds(aligned(ci * 8, 8), 8), :][0:1, :]
            st_ref[h] = state * gl - r[:d] + rr_ref[h, pl.ds(rw, d), :]
            o = r[d:] + op_ref[h, pl.ds(rc, c), :]
            gate = gate_ref[pl.ds(rc, c), h * d:(h + 1) * d].astype(F32)
            o_ref[pl.ds(rc, c), h * d:(h + 1) * d] = (
                _rms(o, onorm_ref[...]) * gate).astype(o_ref.dtype)

    def prep_and_scan(gi, scan_group):
        pending = [] if scan_group is None else [scan_group * group + j for j in range(group)]
        for _ in prep_stages(gi):
            if pending:
                scan_chunk(pending.pop(0))
        for ci in pending:
            scan_chunk(ci)

    st_ref[...] = jnp.zeros_like(st_ref)
    prep_and_scan(0, None)

    def body(gi, _):
        prep_and_scan(gi, gi - 1)
        return 0

    lax.fori_loop(1, n_groups, body, 0)
    for j in range(group):
        scan_chunk((n_groups - 1) * group + j)


def _gdn(gdn_in, small, alog_vec, dtb_vec, onorm, batch, seq, group):
    n = batch * seq
    d = GDN_HEAD_DIM
    heads = N_GDN_HEADS
    n_chunks = seq // CHUNK
    kern = functools.partial(_gdn_kernel, seq=seq, heads=heads, group=group)
    col_block = lambda part: (lambda b: (b, part))
    const = lambda b: (0, 0)
    return pl.pallas_call(
        kern,
        grid=(batch,),
        in_specs=[
            pl.BlockSpec((seq, D_GDN), col_block(0)),
            pl.BlockSpec((seq, D_GDN), col_block(1)),
            pl.BlockSpec((seq, D_GDN), col_block(2)),
            pl.BlockSpec((seq, D_GDN), col_block(3)),
            pl.BlockSpec((seq, LANES), lambda b: (b, 0)),
            pl.BlockSpec((1, LANES), const),
            pl.BlockSpec((1, LANES), const),
            pl.BlockSpec((1, d), const),
        ],
        out_specs=pl.BlockSpec((seq, D_GDN), lambda b: (b, 0)),
        out_shape=jax.ShapeDtypeStruct((n, D_GDN), BF16),
        scratch_shapes=[
            pltpu.VMEM((heads, n_chunks * d, d), BF16),
            pltpu.VMEM((heads, n_chunks * d, d), F32),
            pltpu.VMEM((heads, seq, d), BF16),
            pltpu.VMEM((heads, seq, d), F32),
            pltpu.VMEM((heads, n_chunks * 8, d), F32),
            pltpu.VMEM((heads, d, d), F32),
        ],
        compiler_params=pltpu.CompilerParams(
            dimension_semantics=("arbitrary",), vmem_limit_bytes=VMEM_LIMIT),
        name="gated_deltanet",
    )(gdn_in, gdn_in, gdn_in, gdn_in, small, alog_vec, dtb_vec, onorm)


def _out_mlp_kernel(fox_ref, gdn_ref, x_ref, wof_ref, wog_ref, pmix_ref, pre_ref,
                    wup_ref, wdn_ref, post_ref, o_ref, *, ff_blk):
    mixed = _dot(fox_ref[...], wof_ref[...]) + _dot(gdn_ref[...], wog_ref[...])
    x1 = x_ref[...] + _rms(mixed, pmix_ref[...])
    h = _rms(x1, pre_ref[...]).astype(BF16)
    d_ff = wup_ref.shape[1]
    y = None
    for j in range(d_ff // ff_blk):
        a = _dot(h, wup_ref[:, j * ff_blk:(j + 1) * ff_blk])
        a = jnp.square(jnp.maximum(a, 0.0)).astype(BF16)
        t = _dot(a, wdn_ref[j * ff_blk:(j + 1) * ff_blk, :])
        y = t if y is None else y + t
    o_ref[...] = x1 + _rms(y, post_ref[...])


def _out_mlp(fox_o, gdn_o, x2, wof, wog, pmix, pre, wup, wdn, post, tm):
    n, d = x2.shape
    const = lambda i: (0, 0)
    row = lambda i: (i, 0)
    single = pl.Buffered(1)
    kern = functools.partial(_out_mlp_kernel, ff_blk=1024)
    return pl.pallas_call(
        kern,
        grid=(n // tm,),
        in_specs=[
            pl.BlockSpec((tm, fox_o.shape[1]), row),
            pl.BlockSpec((tm, gdn_o.shape[1]), row),
            pl.BlockSpec((tm, d), row),
            pl.BlockSpec(wof.shape, const, pipeline_mode=single),
            pl.BlockSpec(wog.shape, const, pipeline_mode=single),
            pl.BlockSpec((1, d), const),
            pl.BlockSpec((1, d), const),
            pl.BlockSpec(wup.shape, const, pipeline_mode=single),
            pl.BlockSpec(wdn.shape, const, pipeline_mode=single),
            pl.BlockSpec((1, d), const),
        ],
        out_specs=pl.BlockSpec((tm, d), row),
        out_shape=jax.ShapeDtypeStruct((n, d), F32),
        compiler_params=pltpu.CompilerParams(
            dimension_semantics=("arbitrary",), vmem_limit_bytes=VMEM_LIMIT),
        name="out_mlp",
    )(fox_o, gdn_o, x2, wof, wog, pmix, pre, wup, wdn, post)


def _lane_vec(values, offset):
    return jnp.zeros((1, LANES), F32).at[0, offset:offset + values.shape[0]].set(values.astype(F32))


def kernel(x, pre_mix_norm, w_in, fox_f_bias, fox_out_norm, gdn_conv_w, gdn_a_log, gdn_dt_bias,
           gdn_out_norm, w_out, post_mix_norm, pre_mlp_norm, w_up, w_down, post_mlp_norm):
    b, s, d = x.shape
    n = b * s
    x2 = x.reshape(n, d)

    o = 0
    parts = []
    for size in (D_FOX, D_FOX, D_FOX, N_FOX_HEADS, D_GDN, D_GDN, D_GDN, N_GDN_HEADS, N_GDN_HEADS, D_GDN):
        parts.append(w_in[:, o:o + size])
        o += size
    fq, fk, fv, ff, gq, gk, gv, gb, ga, gz = parts
    wf = jnp.concatenate([fq * (FOX_HEAD_DIM ** -0.5 * LOG2E), fk, fv], axis=1).astype(BF16)
    wg = jnp.concatenate([gq, gk, gv, gz], axis=1).astype(BF16)
    ws = jnp.concatenate(
        [ff] * FOX_REP + [jnp.zeros((d, SMALL_B0 - FOX_REP * N_FOX_HEADS), w_in.dtype), gb, ga,
                          jnp.zeros((d, LANES - SMALL_A0 - N_GDN_HEADS), w_in.dtype)],
        axis=1).astype(BF16)

    fox_qkv, gdn_in, small = _in_proj(x2, pre_mix_norm.reshape(1, d).astype(F32), wf, wg, ws,
                                      gdn_conv_w.astype(F32), s, tm=1024)

    fbias = _lane_vec(jnp.tile(fox_f_bias, FOX_REP), 0)
    fnorm = jnp.tile(fox_out_norm.astype(F32), 2).reshape(1, LANES)
    fox_o = _fox(fox_qkv, small, fbias, fnorm, b, s, tq=1024, tk=512, dg=512)

    alog_vec = _lane_vec(gdn_a_log, SMALL_A0)
    dtb_vec = _lane_vec(gdn_dt_bias, SMALL_A0)
    gdn_o = _gdn(gdn_in, small, alog_vec, dtb_vec,
                 gdn_out_norm.reshape(1, GDN_HEAD_DIM).astype(F32), b, s, group=4)

    wo = w_out.astype(BF16)
    out = _out_mlp(fox_o, gdn_o, x2, wo[:D_FOX], wo[D_FOX:],
                   post_mix_norm.reshape(1, d).astype(F32), pre_mlp_norm.reshape(1, d).astype(F32),
                   w_up.astype(BF16), w_down.astype(BF16), post_mlp_norm.reshape(1, d).astype(F32),
                   tm=512)
    return out.reshape(b, s, d)
```

```python
import functools

import jax
import jax.numpy as jnp
from jax import lax
from jax.experimental import pallas as pl
from jax.experimental.pallas import tpu as pltpu

F32 = jnp.float32
BF16 = jnp.bfloat16

EPS = 1e-6
LOG2E = 1.4426950408889634
LANES = 128
FOX_HEAD_DIM = 64
N_FOX_HEADS = 8
D_FOX = FOX_HEAD_DIM * N_FOX_HEADS
GDN_HEAD_DIM = 128
N_GDN_HEADS = 4
D_GDN = GDN_HEAD_DIM * N_GDN_HEADS
CHUNK = 64
CONV_K = 4

FOX_REP = 6
SMALL_B0 = 64
SMALL_A0 = SMALL_B0 + N_GDN_HEADS

VEC_FBIAS, VEC_FNORM, VEC_ALOG, VEC_DTB, VEC_ONORM = range(5)

VMEM_LIMIT = 56 * 1024 * 1024


def _dot(a, b):
    return jnp.dot(a, b, preferred_element_type=F32)


def _dot_nt(a, b):
    return lax.dot_general(a, b, (((1,), (1,)), ((), ())), preferred_element_type=F32)


def _dot_tn(a, b):
    return lax.dot_general(a, b, (((0,), (0,)), ((), ())), preferred_element_type=F32)


def _rms(x, w):
    return x * lax.rsqrt(jnp.mean(x * x, axis=-1, keepdims=True) + EPS) * w


def _split3(x):
    hi = x.astype(BF16)
    r1 = x - hi.astype(F32)
    mid = r1.astype(BF16)
    lo = (r1 - mid.astype(F32)).astype(BF16)
    return hi, mid, lo


def _tril_ones(n, dtype, block=None):
    r = lax.broadcasted_iota(jnp.int32, (n, n), 0)
    c = lax.broadcasted_iota(jnp.int32, (n, n), 1)
    keep = c <= r
    if block is not None:
        sh = block.bit_length() - 1
        keep = keep & ((r >> sh) == (c >> sh))
    return jnp.where(keep, 1.0, 0.0).astype(dtype)


def _cumsum_rows(tril_bf16, x):
    hi, mid, lo = _split3(x)
    return _dot(tril_bf16, hi) + _dot(tril_bf16, mid) + _dot(tril_bf16, lo)


def _sigmoid(x):
    return 0.5 + 0.5 * jnp.tanh(0.5 * x)


def _silu(x):
    h = 0.5 * x
    return h + h * jnp.tanh(h)


def _softplus(x):
    return jnp.maximum(x, 0.0) + jnp.log1p(jnp.exp(-jnp.abs(x)))


def _in_proj_kernel(x_ref, g_ref, w_ref, cw_ref, fox_ref, gdn_ref, small_ref,
                    halo_ref, win_ref, *, tiles_per_seq):
    i = pl.program_id(0)
    d = GDN_HEAD_DIM
    g0 = 3 * D_FOX
    s0 = g0 + 4 * D_GDN
    h = _rms(x_ref[...], g_ref[...]).astype(BF16)
    tm = h.shape[0]
    first = i % tiles_per_seq == 0
    slab = 2 * d
    for sl in range(4 * D_GDN // slab):
        cols = slice(sl * slab, (sl + 1) * slab)
        if sl * slab < 3 * D_FOX:
            fox_ref[:, cols] = _dot(h, w_ref[:, cols]).astype(BF16)
        g = _dot(h, w_ref[:, g0 + sl * slab:g0 + (sl + 1) * slab])
        if sl * slab >= 3 * D_GDN:
            gdn_ref[:, cols] = _silu(g).astype(BF16)
            continue
        win_ref[0:8, :] = jnp.where(first, 0.0, halo_ref[:, cols])
        win_ref[8:, :] = g
        halo_ref[:, cols] = g[tm - 8:, :]
        w = cw_ref[:, cols]
        y = g * w[CONV_K - 1:CONV_K, :]
        for j in range(1, CONV_K):
            shifted = win_ref[8 - j:8 - j + tm, :]
            y = y + shifted * w[CONV_K - 1 - j:CONV_K - j, :]
        act = _silu(y)
        if sl * slab >= 2 * D_GDN:
            gdn_ref[:, cols] = act.astype(BF16)
            continue
        for hd in range(slab // d):
            t = act[:, hd * d:(hd + 1) * d]
            r = lax.rsqrt(jnp.sum(t * t, axis=-1, keepdims=True) + EPS)
            if sl * slab < D_GDN:
                r = r * (d ** -0.5)
            gdn_ref[:, sl * slab + hd * d:sl * slab + (hd + 1) * d] = (t * r).astype(BF16)
    small_ref[...] = _dot(h, w_ref[:, s0:])


def _in_proj(x2, gain, w_all, conv_w, seq, tm):
    n, d = x2.shape
    const = lambda i: (0, 0)
    row = lambda i: (i, 0)
    kern = functools.partial(_in_proj_kernel, tiles_per_seq=seq // tm)
    return pl.pallas_call(
        kern,
        grid=(n // tm,),
        in_specs=[
            pl.BlockSpec((tm, d), row),
            pl.BlockSpec((1, d), const),
            pl.BlockSpec(w_all.shape, const, pipeline_mode=pl.Buffered(1)),
            pl.BlockSpec(conv_w.shape, const),
        ],
        out_specs=[
            pl.BlockSpec((tm, 3 * D_FOX), row),
            pl.BlockSpec((tm, 4 * D_GDN), row),
            pl.BlockSpec((tm, LANES), row),
        ],
        out_shape=[
            jax.ShapeDtypeStruct((n, 3 * D_FOX), BF16),
            jax.ShapeDtypeStruct((n, 4 * D_GDN), BF16),
            jax.ShapeDtypeStruct((n, LANES), F32),
        ],
        scratch_shapes=[pltpu.VMEM((8, 3 * D_GDN), F32),
                        pltpu.VMEM((tm + 8, 2 * GDN_HEAD_DIM), F32)],
        compiler_params=pltpu.CompilerParams(
            dimension_semantics=("arbitrary",), vmem_limit_bytes=VMEM_LIMIT),
        name="in_proj",
    )(x2, gain, w_all, conv_w)


def _fox_kernel(q_ref, k_ref, v_ref, small_ref, vec_ref, o_ref,
                qx_ref, kx_ref, vaug_ref, *, seq, tq, tk, dg, cum_blk):
    p = pl.program_id(1)
    qi = pl.program_id(2)
    lane = lax.broadcasted_iota(jnp.int32, (1, LANES), 1)
    half = FOX_HEAD_DIM
    nh = N_FOX_HEADS
    head_mask = (lane < half, lane >= half)
    ones_lane = (half, 0)
    zero = jnp.zeros((), BF16)

    @pl.when((p == 0) & (qi == 0))
    def _per_sequence():
        tril = _tril_ones(cum_blk, BF16)
        carry = jnp.zeros((1, LANES), F32)
        for blk in range(seq // cum_blk):
            rows = slice(blk * cum_blk, (blk + 1) * cum_blk)
            z = small_ref[rows, :] + vec_ref[VEC_FBIAS:VEC_FBIAS + 1, :]
            c = _cumsum_rows(tril, -_softplus(-z)) + carry
            carry = c[cum_blk - 1:cum_blk, :]
            hi, mid, lo = (t.astype(F32) for t in _split3(c * LOG2E))
            qx_ref[rows, :] = jnp.where(lane < nh, hi, jnp.where(lane < 2 * nh, mid, jnp.where(
                lane < 3 * nh, lo, jnp.where(lane < 6 * nh, 1.0, 0.0)))).astype(BF16)
            kx_ref[rows, :] = jnp.where(lane < 3 * nh, 1.0, jnp.where(lane < 4 * nh, -hi, jnp.where(
                lane < 5 * nh, -mid, jnp.where(lane < 6 * nh, -lo, 0.0)))).astype(BF16)

    @pl.when(qi == 0)
    def _per_pair():
        vp = v_ref[...]
        for e in range(2):
            ones_col = jnp.where(lane == ones_lane[e], 1.0, 0.0).astype(BF16)
            vaug_ref[e] = jnp.where(head_mask[e], vp, ones_col)

    r0 = pl.multiple_of(qi * tq, tq)
    qs = q_ref[...]
    qx = qx_ref[pl.ds(r0, tq), :]
    qaug = []
    for e in range(2):
        xmask = ((lane & (nh - 1)) == 2 * p + e) & (lane < 6 * nh)
        qaug.append(jnp.concatenate(
            [jnp.where(head_mask[e], qs, zero), jnp.where(xmask, qx, zero)], axis=1))

    def block(j0, width, row_lo, carry, masked):
        kb = jnp.concatenate([k_ref[pl.ds(j0, width), :], kx_ref[pl.ds(j0, width), :]], axis=1)
        nrows = tq - row_lo
        out = []
        for e in range(2):
            m_all, acc_all = carry[e]
            m, acc = m_all[row_lo:], acc_all[row_lo:]
            vb = vaug_ref[e, pl.ds(j0, width), :]
            s = _dot_nt(qaug[e][row_lo:], kb)
            if masked:
                rr = lax.broadcasted_iota(jnp.int32, (width, width), 0)
                cc = lax.broadcasted_iota(jnp.int32, (width, width), 1)
                top = jnp.where(cc <= rr, s[:width], -jnp.inf)
                s = top if nrows == width else jnp.concatenate([top, s[width:]], axis=0)
            m_new = jnp.maximum(m, jnp.max(s, axis=-1, keepdims=True))
            alpha = jnp.exp2(m - m_new)
            pexp = jnp.exp2(s - m_new)
            acc = alpha * acc + _dot(pexp.astype(BF16), vb)
            if row_lo:
                m_new = jnp.concatenate([m_all[:row_lo], m_new], axis=0)
                acc = jnp.concatenate([acc_all[:row_lo], acc], axis=0)
            out.append((m_new, acc))
        return tuple(out)

    init = tuple((jnp.full((tq, 1), -jnp.inf, F32), jnp.zeros((tq, LANES), F32)) for _ in range(2))
    carry = lax.fori_loop(
        0, qi * (tq // tk),
        lambda j, c: block(pl.multiple_of(j * tk, tk), tk, 0, c, False), init)
    for c in range(tq // dg):
        carry = block(pl.multiple_of(r0 + c * dg, dg), dg, c * dg, carry, True)

    outs = []
    for e in range(2):
        _, acc = carry[e]
        l = jnp.sum(jnp.where(lane == ones_lane[e], acc, 0.0), axis=-1, keepdims=True)
        outs.append(acc / l)
    o = jnp.where(head_mask[0], outs[0], outs[1])
    o2 = o * o
    ss0 = jnp.sum(jnp.where(head_mask[0], o2, 0.0), axis=-1, keepdims=True)
    ss1 = jnp.sum(jnp.where(head_mask[1], o2, 0.0), axis=-1, keepdims=True)
    ms = jnp.where(head_mask[0], ss0, ss1) * (1.0 / FOX_HEAD_DIM)
    o_ref[...] = (o * lax.rsqrt(ms + EPS) * vec_ref[VEC_FNORM:VEC_FNORM + 1, :]).astype(o_ref.dtype)


def _fox(fox_qkv, small, vecs, batch, seq, tq, tk, dg):
    n = batch * seq
    nq = seq // tq
    pairs = N_FOX_HEADS // 2
    kern = functools.partial(_fox_kernel, seq=seq, tq=tq, tk=tk, dg=dg, cum_blk=256)
    return pl.pallas_call(
        kern,
        grid=(batch, pairs, nq),
        in_specs=[
            pl.BlockSpec((tq, LANES), lambda b, p, i: (b * nq + i, p)),
            pl.BlockSpec((seq, LANES), lambda b, p, i: (b, pairs + p)),
            pl.BlockSpec((seq, LANES), lambda b, p, i: (b, 2 * pairs + p)),
            pl.BlockSpec((seq, LANES), lambda b, p, i: (b, 0)),
            pl.BlockSpec(vecs.shape, lambda b, p, i: (0, 0)),
        ],
        out_specs=pl.BlockSpec((tq, LANES), lambda b, p, i: (b * nq + i, p)),
        out_shape=jax.ShapeDtypeStruct((n, D_FOX), BF16),
        scratch_shapes=[
            pltpu.VMEM((seq, LANES), BF16),
            pltpu.VMEM((seq, LANES), BF16),
            pltpu.VMEM((2, seq, LANES), BF16),
        ],
        compiler_params=pltpu.CompilerParams(
            dimension_semantics=("arbitrary", "arbitrary", "arbitrary"),
            vmem_limit_bytes=VMEM_LIMIT),
        name="fox_attention",
    )(fox_qkv, fox_qkv, fox_qkv, small, vecs)


def _doubling_inverse(m_lows):
    c = m_lows[0].shape[0]
    row = lax.broadcasted_iota(jnp.int32, (c, c), 0)
    col = lax.broadcasted_iota(jnp.int32, (c, c), 1)
    eye = jnp.where(row == col, 1.0, 0.0)
    pair = (row >> 1) == (col >> 1)
    xs = [eye - jnp.where(pair, m, 0.0) for m in m_lows]
    size = 2
    while size < c:
        sh = size.bit_length() - 1
        rb = row >> sh
        below = ((rb & 1) == 1) & ((col >> sh) == rb - 1)
        xbfs = [x.astype(BF16) for x in xs]
        nxs = [_dot(jnp.where(below, m, 0.0).astype(BF16), xb).astype(BF16)
               for m, xb in zip(m_lows, xbfs)]
        xs = [x - _dot(xb, nx) for x, xb, nx in zip(xs, xbfs, nxs)]
        size *= 2
        yield
    return xs


def _gdn_kernel(q_ref, k_ref, v_ref, gate_ref, small_ref, vec_ref, o_ref,
                w2_ref, rr_ref, qp_ref, op_ref, gl_ref, st_ref, *, seq, heads, group):
    c = CHUNK
    d = GDN_HEAD_DIM
    rows = group * c
    n_groups = seq // rows
    lane = lax.broadcasted_iota(jnp.int32, (1, LANES), 1)
    row = lax.broadcasted_iota(jnp.int32, (c, c), 0)
    col = lax.broadcasted_iota(jnp.int32, (c, c), 1)
    tril = _tril_ones(rows, BF16, block=c)

    def aligned(x, m):
        return x if isinstance(x, int) else pl.multiple_of(x, m)

    def prep_stages(gi):
        r0 = aligned(gi * rows, rows)
        qa = q_ref[pl.ds(r0, rows), :].astype(F32)
        ka = k_ref[pl.ds(r0, rows), :]
        va = v_ref[pl.ds(r0, rows), :].astype(F32)
        sm = small_ref[pl.ds(r0, rows), :]
        g_all = -jnp.exp(vec_ref[VEC_ALOG:VEC_ALOG + 1, :]) * _softplus(
            sm + vec_ref[VEC_DTB:VEC_DTB + 1, :])
        gc_all = _cumsum_rows(tril, g_all)
        items = []
        for h in range(heads):
            qh = qa[:, h * d:(h + 1) * d]
            kh = ka[:, h * d:(h + 1) * d]
            vh = va[:, h * d:(h + 1) * d]
            beta_h = _sigmoid(jnp.sum(jnp.where(lane == SMALL_B0 + h, sm, 0.0),
                                      axis=-1, keepdims=True))
            gc_h = jnp.sum(jnp.where(lane == SMALL_A0 + h, gc_all, 0.0),
                           axis=-1, keepdims=True)
            for g in range(group):
                sl = slice(g * c, (g + 1) * c)
                items.append(dict(h=h, g=g, q=qh[sl], k=kh[sl].astype(F32), kbf=kh[sl], v=vh[sl],
                                  beta=beta_h[sl], gc=gc_h[sl]))
        for it in items:
            gc = it["gc"]
            hi, mid, lo = (t.astype(F32) for t in _split3(gc))
            it["al"] = jnp.where(lane == 0, hi, jnp.where(lane == 1, mid, jnp.where(
                lane == 2, lo, jnp.where(lane < 6, 1.0, 0.0)))).astype(BF16)
            it["ar"] = jnp.where(lane < 3, 1.0, jnp.where(lane == 3, -hi, jnp.where(
                lane == 4, -mid, jnp.where(lane == 5, -lo, 0.0)))).astype(BF16)
            it["kb"] = it["k"] * it["beta"]
        dmats = [_dot_nt(it["al"], it["ar"]) for it in items]
        kks = [_dot_nt(it["kb"].astype(BF16), it["kbf"]) for it in items]
        qks = [_dot_nt(it["q"].astype(BF16), it["kbf"]) for it in items]
        m_lows = []
        for it, dmat, kk, qk in zip(items, dmats, kks, qks):
            decay = jnp.exp(jnp.where(col <= row, dmat, -jnp.inf))
            m_lows.append(jnp.where(col < row, kk * decay, 0.0))
            it["a"] = (qk * decay).astype(BF16)
        yield
        t_invs = yield from _doubling_inverse(m_lows)
        wus = []
        for it, t_inv in zip(items, t_invs):
            eg = jnp.exp(it["gc"])
            it["eg"] = eg
            rhs = jnp.concatenate(
                [(it["kb"] * eg).astype(BF16), (it["v"] * it["beta"]).astype(BF16)], axis=1)
            wus.append(_dot(t_inv.astype(BF16), rhs).astype(BF16))
        yield
        kwus, awus = [], []
        for it, wu in zip(items, wus):
            gc_last = it["gc"][c - 1:c, :]
            it["gl"] = jnp.exp(gc_last)
            kd = (it["k"] * jnp.exp(gc_last - it["gc"])).astype(BF16)
            kwus.append(_dot_tn(kd, wu))
            awus.append(_dot(it["a"], wu))
        for it, kwu, awu in zip(items, kwus, awus):
            h = it["h"]
            ci = gi * group + it["g"]
            rw = aligned(ci * d, d)
            rc = aligned(ci * c, c)
            w2_ref[h, pl.ds(rw, d), :] = kwu[:, :d].astype(BF16)
            rr_ref[h, pl.ds(rw, d), :] = kwu[:, d:]
            qp_ref[h, pl.ds(rc, c), :] = (it["q"] * it["eg"] - awu[:, :d]).astype(BF16)
            op_ref[h, pl.ds(rc, c), :] = awu[:, d:]
            gl_ref[h, pl.ds(aligned(ci * 8, 8), 8), :] = jnp.broadcast_to(it["gl"], (8, d))

    def scan_chunk(ci):
        rc = aligned(ci * c, c)
        rw = aligned(ci * d, d)
        for h in range(heads):
            state = st_ref[h]
            lhs = jnp.concatenate([w2_ref[h, pl.ds(rw, d), :], qp_ref[h, pl.ds(rc, c), :]], axis=0)
            r = _dot(lhs, state.astype(BF16))
            gl = gl_ref[h, pl.ds(aligned(ci * 8, 8), 8), :][0:1, :]
            st_ref[h] = state * gl - r[:d] + rr_ref[h, pl.ds(rw, d), :]
            o = r[d:] + op_ref[h, pl.ds(rc, c), :]
            gate = gate_ref[pl.ds(rc, c), h * d:(h + 1) * d].astype(F32)
            o_ref[pl.ds(rc, c), h * d:(h + 1) * d] = (
                _rms(o, vec_ref[VEC_ONORM:VEC_ONORM + 1, :]) * gate).astype(o_ref.dtype)

    def prep_and_scan(gi, scan_group):
        pending = [] if scan_group is None else [scan_group * group + j for j in range(group)]
        for _ in prep_stages(gi):
            if pending:
                scan_chunk(pending.pop(0))
        for ci in pending:
            scan_chunk(ci)

    st_ref[...] = jnp.zeros_like(st_ref)
    prep_and_scan(0, None)

    def body(gi, _):
        prep_and_scan(gi, gi - 1)
        return 0

    lax.fori_loop(1, n_groups, body, 0)
    for j in range(group):
        scan_chunk((n_groups - 1) * group + j)


def _gdn(gdn_in, small, vecs, batch, seq, group):
    n = batch * seq
    d = GDN_HEAD_DIM
    heads = N_GDN_HEADS
    n_chunks = seq // CHUNK
    kern = functools.partial(_gdn_kernel, seq=seq, heads=heads, group=group)
    col_block = lambda part: (lambda b: (b, part))
    const = lambda b: (0, 0)
    return pl.pallas_call(
        kern,
        grid=(batch,),
        in_specs=[
            pl.BlockSpec((seq, D_GDN), col_block(0)),
            pl.BlockSpec((seq, D_GDN), col_block(1)),
            pl.BlockSpec((seq, D_GDN), col_block(2)),
            pl.BlockSpec((seq, D_GDN), col_block(3)),
            pl.BlockSpec((seq, LANES), lambda b: (b, 0)),
            pl.BlockSpec(vecs.shape, const),
        ],
        out_specs=pl.BlockSpec((seq, D_GDN), lambda b: (b, 0)),
        out_shape=jax.ShapeDtypeStruct((n, D_GDN), BF16),
        scratch_shapes=[
            pltpu.VMEM((heads, n_chunks * d, d), BF16),
            pltpu.VMEM((heads, n_chunks * d, d), F32),
            pltpu.VMEM((heads, seq, d), BF16),
            pltpu.VMEM((heads, seq, d), F32),
            pltpu.VMEM((heads, n_chunks * 8, d), F32),
            pltpu.VMEM((heads, d, d), F32),
        ],
        compiler_params=pltpu.CompilerParams(
            dimension_semantics=("arbitrary",), vmem_limit_bytes=VMEM_LIMIT),
        name="gated_deltanet",
    )(gdn_in, gdn_in, gdn_in, gdn_in, small, vecs)


def _out_mlp_kernel(fox_ref, gdn_ref, x_ref, wo_ref, pmix_ref, pre_ref,
                    wup_ref, wdn_ref, post_ref, o_ref, *, ff_blk):
    mixed = _dot(fox_ref[...], wo_ref[:D_FOX, :]) + _dot(gdn_ref[...], wo_ref[D_FOX:, :])
    x1 = x_ref[...] + _rms(mixed, pmix_ref[...])
    h = _rms(x1, pre_ref[...]).astype(BF16)
    d_ff = wup_ref.shape[1]
    y = None
    for j in range(d_ff // ff_blk):
        a = _dot(h, wup_ref[:, j * ff_blk:(j + 1) * ff_blk])
        a = jnp.square(jnp.maximum(a, 0.0)).astype(BF16)
        t = _dot(a, wdn_ref[j * ff_blk:(j + 1) * ff_blk, :])
        y = t if y is None else y + t
    o_ref[...] = x1 + _rms(y, post_ref[...])


def _out_mlp(fox_o, gdn_o, x2, wo, pmix, pre, wup, wdn, post, tm):
    n, d = x2.shape
    const = lambda i: (0, 0)
    row = lambda i: (i, 0)
    single = pl.Buffered(1)
    kern = functools.partial(_out_mlp_kernel, ff_blk=1024)
    return pl.pallas_call(
        kern,
        grid=(n // tm,),
        in_specs=[
            pl.BlockSpec((tm, fox_o.shape[1]), row),
            pl.BlockSpec((tm, gdn_o.shape[1]), row),
            pl.BlockSpec((tm, d), row),
            pl.BlockSpec(wo.shape, const, pipeline_mode=single),
            pl.BlockSpec((1, d), const),
            pl.BlockSpec((1, d), const),
            pl.BlockSpec(wup.shape, const, pipeline_mode=single),
            pl.BlockSpec(wdn.shape, const, pipeline_mode=single),
            pl.BlockSpec((1, d), const),
        ],
        out_specs=pl.BlockSpec((tm, d), row),
        out_shape=jax.ShapeDtypeStruct((n, d), F32),
        compiler_params=pltpu.CompilerParams(
            dimension_semantics=("arbitrary",), vmem_limit_bytes=VMEM_LIMIT),
        name="out_mlp",
    )(fox_o, gdn_o, x2, wo, pmix, pre, wup, wdn, post)


def _vec_row(values):
    values = values.astype(F32)
    return jnp.concatenate([values, jnp.zeros((LANES - values.shape[0],), F32)])


def kernel(x, pre_mix_norm, w_in, fox_f_bias, fox_out_norm, gdn_conv_w, gdn_a_log, gdn_dt_bias,
           gdn_out_norm, w_out, post_mix_norm, pre_mlp_norm, w_up, w_down, post_mlp_norm):
    b, s, d = x.shape
    n = b * s
    x2 = x.reshape(n, d)

    o = 0
    parts = []
    for size in (D_FOX, D_FOX, D_FOX, N_FOX_HEADS, D_GDN, D_GDN, D_GDN, N_GDN_HEADS, N_GDN_HEADS, D_GDN):
        parts.append(w_in[:, o:o + size])
        o += size
    fq, fk, fv, ff, gq, gk, gv, gb, ga, gz = parts
    w_all = jnp.concatenate(
        [fq * (FOX_HEAD_DIM ** -0.5 * LOG2E), fk, fv, gq, gk, gv, gz]
        + [ff] * FOX_REP
        + [jnp.zeros((d, SMALL_B0 - FOX_REP * N_FOX_HEADS), w_in.dtype), gb, ga,
           jnp.zeros((d, LANES - SMALL_A0 - N_GDN_HEADS), w_in.dtype)],
        axis=1).astype(BF16)

    zeros_b = jnp.zeros((SMALL_A0,), F32)
    vecs = jnp.stack([
        _vec_row(jnp.tile(fox_f_bias, FOX_REP)),
        _vec_row(jnp.tile(fox_out_norm, 2)),
        _vec_row(jnp.concatenate([zeros_b, gdn_a_log.astype(F32)])),
        _vec_row(jnp.concatenate([zeros_b, gdn_dt_bias.astype(F32)])),
        _vec_row(gdn_out_norm),
        jnp.zeros((LANES,), F32), jnp.zeros((LANES,), F32), jnp.zeros((LANES,), F32)])

    fox_qkv, gdn_in, small = _in_proj(x2, pre_mix_norm.reshape(1, d).astype(F32), w_all,
                                      gdn_conv_w.astype(F32), s, tm=1024)
    fox_o = _fox(fox_qkv, small, vecs, b, s, tq=1024, tk=512, dg=512)
    gdn_o = _gdn(gdn_in, small, vecs, b, s, group=4)
    out = _out_mlp(fox_o, gdn_o, x2, w_out.astype(BF16),
                   post_mix_norm.reshape(1, d).astype(F32), pre_mlp_norm.reshape(1, d).astype(F32),
                   w_up.astype(BF16), w_down.astype(BF16), post_mlp_norm.reshape(1, d).astype(F32),
                   tm=512)
    return out.reshape(b, s, d)
```

```python
import functools

import jax
import jax.numpy as jnp
from jax import lax
from jax.experimental import pallas as pl
from jax.experimental.pallas import tpu as pltpu

F32 = jnp.float32
BF16 = jnp.bfloat16

EPS = 1e-6
LOG2E = 1.4426950408889634
LANES = 128
FOX_HEAD_DIM = 64
N_FOX_HEADS = 8
D_FOX = FOX_HEAD_DIM * N_FOX_HEADS
GDN_HEAD_DIM = 128
N_GDN_HEADS = 4
D_GDN = GDN_HEAD_DIM * N_GDN_HEADS
CHUNK = 64
CONV_K = 4

FOX_REP = 6
SMALL_B0 = 64
SMALL_A0 = SMALL_B0 + N_GDN_HEADS

VEC_FBIAS, VEC_FNORM, VEC_ALOG, VEC_DTB, VEC_ONORM = range(5)

VMEM_LIMIT = 56 * 1024 * 1024


def _dot(a, b):
    return jnp.dot(a, b, preferred_element_type=F32)


def _dot_nt(a, b):
    return lax.dot_general(a, b, (((1,), (1,)), ((), ())), preferred_element_type=F32)


def _dot_tn(a, b):
    return lax.dot_general(a, b, (((0,), (0,)), ((), ())), preferred_element_type=F32)


def _rms(x, w):
    return x * lax.rsqrt(jnp.mean(x * x, axis=-1, keepdims=True) + EPS) * w


def _split3(x):
    hi = x.astype(BF16)
    r1 = x - hi.astype(F32)
    mid = r1.astype(BF16)
    lo = (r1 - mid.astype(F32)).astype(BF16)
    return hi, mid, lo


def _tril_ones(n, dtype, block=None):
    r = lax.broadcasted_iota(jnp.int32, (n, n), 0)
    c = lax.broadcasted_iota(jnp.int32, (n, n), 1)
    keep = c <= r
    if block is not None:
        sh = block.bit_length() - 1
        keep = keep & ((r >> sh) == (c >> sh))
    return jnp.where(keep, 1.0, 0.0).astype(dtype)


def _cumsum_rows(tril_bf16, x):
    hi, mid, lo = _split3(x)
    return _dot(tril_bf16, hi) + _dot(tril_bf16, mid) + _dot(tril_bf16, lo)


def _sigmoid(x):
    return 0.5 + 0.5 * jnp.tanh(0.5 * x)


def _silu(x):
    h = 0.5 * x
    return h + h * jnp.tanh(h)


def _softplus(x):
    return jnp.maximum(x, 0.0) + jnp.log1p(jnp.exp(-jnp.abs(x)))


def _in_proj_kernel(x_ref, g_ref, w_ref, cw_ref, fox_ref, gdn_ref, small_ref,
                    halo_ref, win_ref, *, tiles_per_seq):
    i = pl.program_id(0)
    d = GDN_HEAD_DIM
    g0 = 3 * D_FOX
    s0 = g0 + 4 * D_GDN
    h = _rms(x_ref[...], g_ref[...]).astype(BF16)
    tm = h.shape[0]
    first = i % tiles_per_seq == 0
    slab = 2 * d
    for sl in range(4 * D_GDN // slab):
        cols = slice(sl * slab, (sl + 1) * slab)
        if sl * slab < 3 * D_FOX:
            fox_ref[:, cols] = _dot(h, w_ref[:, cols]).astype(BF16)
        g = _dot(h, w_ref[:, g0 + sl * slab:g0 + (sl + 1) * slab])
        if sl * slab >= 3 * D_GDN:
            gdn_ref[:, cols] = _silu(g).astype(BF16)
            continue
        win_ref[0:8, :] = jnp.where(first, 0.0, halo_ref[:, cols])
        win_ref[8:, :] = g
        halo_ref[:, cols] = g[tm - 8:, :]
        w = cw_ref[:, cols]
        y = g * w[CONV_K - 1:CONV_K, :]
        for j in range(1, CONV_K):
            shifted = win_ref[8 - j:8 - j + tm, :]
            y = y + shifted * w[CONV_K - 1 - j:CONV_K - j, :]
        act = _silu(y)
        if sl * slab >= 2 * D_GDN:
            gdn_ref[:, cols] = act.astype(BF16)
            continue
        for hd in range(slab // d):
            t = act[:, hd * d:(hd + 1) * d]
            r = lax.rsqrt(jnp.sum(t * t, axis=-1, keepdims=True) + EPS)
            if sl * slab < D_GDN:
                r = r * (d ** -0.5)
            gdn_ref[:, sl * slab + hd * d:sl * slab + (hd + 1) * d] = (t * r).astype(BF16)
    small_ref[...] = _dot(h, w_ref[:, s0:])


def _in_proj(x2, gain, w_all, conv_w, seq, tm):
    n, d = x2.shape
    const = lambda i: (0, 0)
    row = lambda i: (i, 0)
    kern = functools.partial(_in_proj_kernel, tiles_per_seq=seq // tm)
    return pl.pallas_call(
        kern,
        grid=(n // tm,),
        in_specs=[
            pl.BlockSpec((tm, d), row),
            pl.BlockSpec((1, d), const),
            pl.BlockSpec(w_all.shape, const, pipeline_mode=pl.Buffered(1)),
            pl.BlockSpec(conv_w.shape, const),
        ],
        out_specs=[
            pl.BlockSpec((tm, 3 * D_FOX), row),
            pl.BlockSpec((tm, 4 * D_GDN), row),
            pl.BlockSpec((tm, LANES), row),
        ],
        out_shape=[
            jax.ShapeDtypeStruct((n, 3 * D_FOX), BF16),
            jax.ShapeDtypeStruct((n, 4 * D_GDN), BF16),
            jax.ShapeDtypeStruct((n, LANES), F32),
        ],
        scratch_shapes=[pltpu.VMEM((8, 3 * D_GDN), F32),
                        pltpu.VMEM((tm + 8, 2 * GDN_HEAD_DIM), F32)],
        compiler_params=pltpu.CompilerParams(
            dimension_semantics=("arbitrary",), vmem_limit_bytes=VMEM_LIMIT),
        name="in_proj",
    )(x2, gain, w_all, conv_w)


def _fox_kernel(q_ref, k_ref, v_ref, small_ref, vec_ref, o_ref,
                qx_ref, kx_ref, vaug_ref, *, seq, tq, tk, dg, cum_blk):
    p = pl.program_id(1)
    qi = pl.program_id(2)
    lane = lax.broadcasted_iota(jnp.int32, (1, LANES), 1)
    half = FOX_HEAD_DIM
    nh = N_FOX_HEADS
    head_mask = (lane < half, lane >= half)
    ones_lane = (half, 0)
    zero = jnp.zeros((), BF16)

    @pl.when((p == 0) & (qi == 0))
    def _per_sequence():
        tril = _tril_ones(cum_blk, BF16)
        carry = jnp.zeros((1, LANES), F32)
        for blk in range(seq // cum_blk):
            rows = slice(blk * cum_blk, (blk + 1) * cum_blk)
            z = small_ref[rows, :] + vec_ref[VEC_FBIAS:VEC_FBIAS + 1, :]
            c = _cumsum_rows(tril, -_softplus(-z)) + carry
            carry = c[cum_blk - 1:cum_blk, :]
            hi, mid, lo = (t.astype(F32) for t in _split3(c * LOG2E))
            qx_ref[rows, :] = jnp.where(lane < nh, hi, jnp.where(lane < 2 * nh, mid, jnp.where(
                lane < 3 * nh, lo, jnp.where(lane < 6 * nh, 1.0, 0.0)))).astype(BF16)
            kx_ref[rows, :] = jnp.where(lane < 3 * nh, 1.0, jnp.where(lane < 4 * nh, -hi, jnp.where(
                lane < 5 * nh, -mid, jnp.where(lane < 6 * nh, -lo, 0.0)))).astype(BF16)

    @pl.when(qi == 0)
    def _per_pair():
        vp = v_ref[...]
        for e in range(2):
            ones_col = jnp.where(lane == ones_lane[e], 1.0, 0.0).astype(BF16)
            vaug_ref[e] = jnp.where(head_mask[e], vp, ones_col)

    r0 = pl.multiple_of(qi * tq, tq)
    qs = q_ref[...]
    qx = qx_ref[pl.ds(r0, tq), :]
    qaug = []
    for e in range(2):
        xmask = ((lane & (nh - 1)) == 2 * p + e) & (lane < 6 * nh)
        qaug.append(jnp.concatenate(
            [jnp.where(head_mask[e], qs, zero), jnp.where(xmask, qx, zero)], axis=1))

    def block(j0, width, row_lo, carry, masked):
        kb = jnp.concatenate([k_ref[pl.ds(j0, width), :], kx_ref[pl.ds(j0, width), :]], axis=1)
        nrows = tq - row_lo
        out = []
        for e in range(2):
            m_all, acc_all = carry[e]
            m, acc = m_all[row_lo:], acc_all[row_lo:]
            vb = vaug_ref[e, pl.ds(j0, width), :]
            s = _dot_nt(qaug[e][row_lo:], kb)
            if masked:
                rr = lax.broadcasted_iota(jnp.int32, (width, width), 0)
                cc = lax.broadcasted_iota(jnp.int32, (width, width), 1)
                top = jnp.where(cc <= rr, s[:width], -jnp.inf)
                s = top if nrows == width else jnp.concatenate([top, s[width:]], axis=0)
            m_new = jnp.maximum(m, jnp.max(s, axis=-1, keepdims=True))
            alpha = jnp.exp2(m - m_new)
            pexp = jnp.exp2(s - m_new)
            acc = alpha * acc + _dot(pexp.astype(BF16), vb)
            if row_lo:
                m_new = jnp.concatenate([m_all[:row_lo], m_new], axis=0)
                acc = jnp.concatenate([acc_all[:row_lo], acc], axis=0)
            out.append((m_new, acc))
        return tuple(out)

    init = tuple((jnp.full((tq, 1), -jnp.inf, F32), jnp.zeros((tq, LANES), F32)) for _ in range(2))
    carry = lax.fori_loop(
        0, qi * (tq // tk),
        lambda j, c: block(pl.multiple_of(j * tk, tk), tk, 0, c, False), init)
    for c in range(tq // dg):
        carry = block(pl.multiple_of(r0 + c * dg, dg), dg, c * dg, carry, True)

    outs = []
    for e in range(2):
        _, acc = carry[e]
        l = jnp.sum(jnp.where(lane == ones_lane[e], acc, 0.0), axis=-1, keepdims=True)
        outs.append(acc / l)
    o = jnp.where(head_mask[0], outs[0], outs[1])
    o2 = o * o
    ss0 = jnp.sum(jnp.where(head_mask[0], o2, 0.0), axis=-1, keepdims=True)
    ss1 = jnp.sum(jnp.where(head_mask[1], o2, 0.0), axis=-1, keepdims=True)
    ms = jnp.where(head_mask[0], ss0, ss1) * (1.0 / FOX_HEAD_DIM)
    o_ref[...] = (o * lax.rsqrt(ms + EPS) * vec_ref[VEC_FNORM:VEC_FNORM + 1, :]).astype(o_ref.dtype)


def _fox(fox_qkv, small, vecs, batch, seq, tq, tk, dg):
    n = batch * seq
    nq = seq // tq
    pairs = N_FOX_HEADS // 2
    kern = functools.partial(_fox_kernel, seq=seq, tq=tq, tk=tk, dg=dg, cum_blk=256)
    return pl.pallas_call(
        kern,
        grid=(batch, pairs, nq),
        in_specs=[
            pl.BlockSpec((tq, LANES), lambda b, p, i: (b * nq + i, p)),
            pl.BlockSpec((seq, LANES), lambda b, p, i: (b, pairs + p)),
            pl.BlockSpec((seq, LANES), lambda b, p, i: (b, 2 * pairs + p)),
            pl.BlockSpec((seq, LANES), lambda b, p, i: (b, 0)),
            pl.BlockSpec(vecs.shape, lambda b, p, i: (0, 0)),
        ],
        out_specs=pl.BlockSpec((tq, LANES), lambda b, p, i: (b * nq + i, p)),
        out_shape=jax.ShapeDtypeStruct((n, D_FOX), BF16),
        scratch_shapes=[
            pltpu.VMEM((seq, LANES), BF16),
            pltpu.VMEM((seq, LANES), BF16),
            pltpu.VMEM((2, seq, LANES), BF16),
        ],
        compiler_params=pltpu.CompilerParams(
            dimension_semantics=("arbitrary", "arbitrary", "arbitrary"),
            vmem_limit_bytes=VMEM_LIMIT),
        name="fox_attention",
    )(fox_qkv, fox_qkv, fox_qkv, small, vecs)


def _merge_levels(xs, m_lows, sizes):
    c = m_lows[0].shape[0]
    row = lax.broadcasted_iota(jnp.int32, (c, c), 0)
    col = lax.broadcasted_iota(jnp.int32, (c, c), 1)
    for size in sizes:
        sh = size.bit_length() - 1
        rb = row >> sh
        below = ((rb & 1) == 1) & ((col >> sh) == rb - 1)
        xbfs = [x.astype(BF16) for x in xs]
        nxs = [_dot(jnp.where(below, m, 0.0).astype(BF16), xb).astype(BF16)
               for m, xb in zip(m_lows, xbfs)]
        xs = [x - _dot(xb, nx) for x, xb, nx in zip(xs, xbfs, nxs)]
        yield
    return xs


MERGE_SIZES = (2, 4, 8, 16, 32)


def _gdn_kernel(q_ref, k_ref, v_ref, gate_ref, small_ref, vec_ref, o_ref,
                w2_ref, rr_ref, qp_ref, op_ref, gl_ref, st_ref, *, seq, heads, group):
    c = CHUNK
    d = GDN_HEAD_DIM
    rows = group * c
    n_groups = seq // rows
    lane = lax.broadcasted_iota(jnp.int32, (1, LANES), 1)
    row = lax.broadcasted_iota(jnp.int32, (c, c), 0)
    col = lax.broadcasted_iota(jnp.int32, (c, c), 1)
    tril = _tril_ones(rows, BF16, block=c)

    def aligned(x, m):
        return x if isinstance(x, int) else pl.multiple_of(x, m)

    def prep_stages(gi):
        r0 = aligned(gi * rows, rows)
        qa = q_ref[pl.ds(r0, rows), :].astype(F32)
        ka = k_ref[pl.ds(r0, rows), :]
        va = v_ref[pl.ds(r0, rows), :].astype(F32)
        sm = small_ref[pl.ds(r0, rows), :]
        g_all = -jnp.exp(vec_ref[VEC_ALOG:VEC_ALOG + 1, :]) * _softplus(
            sm + vec_ref[VEC_DTB:VEC_DTB + 1, :])
        gc_all = _cumsum_rows(tril, g_all)
        items = []
        for h in range(heads):
            qh = qa[:, h * d:(h + 1) * d]
            kh = ka[:, h * d:(h + 1) * d]
            vh = va[:, h * d:(h + 1) * d]
            beta_h = _sigmoid(jnp.sum(jnp.where(lane == SMALL_B0 + h, sm, 0.0),
                                      axis=-1, keepdims=True))
            gc_h = jnp.sum(jnp.where(lane == SMALL_A0 + h, gc_all, 0.0),
                           axis=-1, keepdims=True)
            for g in range(group):
                sl = slice(g * c, (g + 1) * c)
                items.append(dict(h=h, g=g, q=qh[sl], k=kh[sl].astype(F32), kbf=kh[sl], v=vh[sl],
                                  beta=beta_h[sl], gc=gc_h[sl]))
        for it in items:
            gc = it["gc"]
            hi, mid, lo = (t.astype(F32) for t in _split3(gc))
            it["al"] = jnp.where(lane == 0, hi, jnp.where(lane == 1, mid, jnp.where(
                lane == 2, lo, jnp.where(lane < 6, 1.0, 0.0)))).astype(BF16)
            it["ar"] = jnp.where(lane < 3, 1.0, jnp.where(lane == 3, -hi, jnp.where(
                lane == 4, -mid, jnp.where(lane == 5, -lo, 0.0)))).astype(BF16)
            it["kb"] = it["k"] * it["beta"]
        dmats = [_dot_nt(it["al"], it["ar"]) for it in items]
        kks = [_dot_nt(it["kb"].astype(BF16), it["kbf"]) for it in items]
        qks = [_dot_nt(it["q"].astype(BF16), it["kbf"]) for it in items]
        m_lows = []
        for it, dmat, kk, qk in zip(items, dmats, kks, qks):
            decay = jnp.exp(jnp.where(col <= row, dmat, -jnp.inf))
            m_lows.append(jnp.where(col < row, kk * decay, 0.0))
            it["a"] = (qk * decay).astype(BF16)
        yield
        eye = jnp.where(row == col, 1.0, 0.0)
        pair = (row >> 1) == (col >> 1)
        xs = [eye - jnp.where(pair, m, 0.0) for m in m_lows]
        t_invs = yield from _merge_levels(xs, m_lows, MERGE_SIZES)
        wus = []
        for it, t_inv in zip(items, t_invs):
            eg = jnp.exp(it["gc"])
            it["eg"] = eg
            rhs = jnp.concatenate(
                [(it["kb"] * eg).astype(BF16), (it["v"] * it["beta"]).astype(BF16)], axis=1)
            wus.append(_dot(t_inv.astype(BF16), rhs).astype(BF16))
        yield
        kwus, awus = [], []
        for it, wu in zip(items, wus):
            gc_last = it["gc"][c - 1:c, :]
            it["gl"] = jnp.exp(gc_last)
            kd = (it["k"] * jnp.exp(gc_last - it["gc"])).astype(BF16)
            kwus.append(_dot_tn(kd, wu))
            awus.append(_dot(it["a"], wu))
        for it, kwu, awu in zip(items, kwus, awus):
            h = it["h"]
            ci = gi * group + it["g"]
            rw = aligned(ci * d, d)
            rc = aligned(ci * c, c)
            w2_ref[h, pl.ds(rw, d), :] = kwu[:, :d].astype(BF16)
            rr_ref[h, pl.ds(rw, d), :] = kwu[:, d:]
            qp_ref[h, pl.ds(rc, c), :] = (it["q"] * it["eg"] - awu[:, :d]).astype(BF16)
            op_ref[h, pl.ds(rc, c), :] = awu[:, d:]
            gl_ref[h, pl.ds(aligned(ci * 8, 8), 8), :] = jnp.broadcast_to(it["gl"], (8, d))

    def scan_chunk(ci):
        rc = aligned(ci * c, c)
        rw = aligned(ci * d, d)
        for h in range(heads):
            state = st_ref[h]
            lhs = jnp.concatenate([w2_ref[h, pl.ds(rw, d), :], qp_ref[h, pl.ds(rc, c), :]], axis=0)
            r = _dot(lhs, state.astype(BF16))
            gl = gl_ref[h, pl.ds(aligned(ci * 8, 8), 8), :][0:1, :]
            st_ref[h] = state * gl - r[:d] + rr_ref[h, pl.ds(rw, d), :]
            o = r[d:] + op_ref[h, pl.ds(rc, c), :]
            gate = gate_ref[pl.ds(rc, c), h * d:(h + 1) * d].astype(F32)
            o_ref[pl.ds(rc, c), h * d:(h + 1) * d] = (
                _rms(o, vec_ref[VEC_ONORM:VEC_ONORM + 1, :]) * gate).astype(o_ref.dtype)

    def prep_and_scan(gi, scan_group):
        pending = [] if scan_group is None else [scan_group * group + j for j in range(group)]
        for _ in prep_stages(gi):
            if pending:
                scan_chunk(pending.pop(0))
        for ci in pending:
            scan_chunk(ci)

    st_ref[...] = jnp.zeros_like(st_ref)
    prep_and_scan(0, None)

    def body(gi, _):
        prep_and_scan(gi, gi - 1)
        return 0

    lax.fori_loop(1, n_groups, body, 0)
    for j in range(group):
        scan_chunk((n_groups - 1) * group + j)


def _gdn(gdn_in, small, vecs, batch, seq, group):
    n = batch * seq
    d = GDN_HEAD_DIM
    heads = N_GDN_HEADS
    n_chunks = seq // CHUNK
    kern = functools.partial(_gdn_kernel, seq=seq, heads=heads, group=group)
    col_block = lambda part: (lambda b: (b, part))
    const = lambda b: (0, 0)
    return pl.pallas_call(
        kern,
        grid=(batch,),
        in_specs=[
            pl.BlockSpec((seq, D_GDN), col_block(0)),
            pl.BlockSpec((seq, D_GDN), col_block(1)),
            pl.BlockSpec((seq, D_GDN), col_block(2)),
            pl.BlockSpec((seq, D_GDN), col_block(3)),
            pl.BlockSpec((seq, LANES), lambda b: (b, 0)),
            pl.BlockSpec(vecs.shape, const),
        ],
        out_specs=pl.BlockSpec((seq, D_GDN), lambda b: (b, 0)),
        out_shape=jax.ShapeDtypeStruct((n, D_GDN), BF16),
        scratch_shapes=[
            pltpu.VMEM((heads, n_chunks * d, d), BF16),
            pltpu.VMEM((heads, n_chunks * d, d), F32),
            pltpu.VMEM((heads, seq, d), BF16),
            pltpu.VMEM((heads, seq, d), F32),
            pltpu.VMEM((heads, n_chunks * 8, d), F32),
            pltpu.VMEM((heads, d, d), F32),
        ],
        compiler_params=pltpu.CompilerParams(
            dimension_semantics=("arbitrary",), vmem_limit_bytes=VMEM_LIMIT),
        name="gated_deltanet",
    )(gdn_in, gdn_in, gdn_in, gdn_in, small, vecs)


def _out_mlp_kernel(fox_ref, gdn_ref, x_ref, wo_ref, pmix_ref, pre_ref,
                    wup_ref, wdn_ref, post_ref, o_ref, *, ff_blk):
    mixed = _dot(fox_ref[...], wo_ref[:D_FOX, :]) + _dot(gdn_ref[...], wo_ref[D_FOX:, :])
    x1 = x_ref[...] + _rms(mixed, pmix_ref[...])
    h = _rms(x1, pre_ref[...]).astype(BF16)
    d_ff = wup_ref.shape[1]
    y = None
    for j in range(d_ff // ff_blk):
        a = _dot(h, wup_ref[:, j * ff_blk:(j + 1) * ff_blk])
        a = jnp.square(jnp.maximum(a, 0.0)).astype(BF16)
        t = _dot(a, wdn_ref[j * ff_blk:(j + 1) * ff_blk, :])
        y = t if y is None else y + t
    o_ref[...] = x1 + _rms(y, post_ref[...])


def _out_mlp(fox_o, gdn_o, x2, wo, pmix, pre, wup, wdn, post, tm):
    n, d = x2.shape
    const = lambda i: (0, 0)
    row = lambda i: (i, 0)
    single = pl.Buffered(1)
    kern = functools.partial(_out_mlp_kernel, ff_blk=1024)
    return pl.pallas_call(
        kern,
        grid=(n // tm,),
        in_specs=[
            pl.BlockSpec((tm, fox_o.shape[1]), row),
            pl.BlockSpec((tm, gdn_o.shape[1]), row),
            pl.BlockSpec((tm, d), row),
            pl.BlockSpec(wo.shape, const, pipeline_mode=single),
            pl.BlockSpec((1, d), const),
            pl.BlockSpec((1, d), const),
            pl.BlockSpec(wup.shape, const, pipeline_mode=single),
            pl.BlockSpec(wdn.shape, const, pipeline_mode=single),
            pl.BlockSpec((1, d), const),
        ],
        out_specs=pl.BlockSpec((tm, d), row),
        out_shape=jax.ShapeDtypeStruct((n, d), F32),
        compiler_params=pltpu.CompilerParams(
            dimension_semantics=("arbitrary",), vmem_limit_bytes=VMEM_LIMIT),
        name="out_mlp",
    )(fox_o, gdn_o, x2, wo, pmix, pre, wup, wdn, post)


def _vec_row(values):
    values = values.astype(F32)
    return jnp.concatenate([values, jnp.zeros((LANES - values.shape[0],), F32)])


def kernel(x, pre_mix_norm, w_in, fox_f_bias, fox_out_norm, gdn_conv_w, gdn_a_log, gdn_dt_bias,
           gdn_out_norm, w_out, post_mix_norm, pre_mlp_norm, w_up, w_down, post_mlp_norm):
    b, s, d = x.shape
    n = b * s
    x2 = x.reshape(n, d)

    w_bf = w_in.astype(BF16)
    o = 0
    parts = []
    for size in (D_FOX, D_FOX, D_FOX, N_FOX_HEADS, D_GDN, D_GDN, D_GDN, N_GDN_HEADS, N_GDN_HEADS, D_GDN):
        parts.append(w_bf[:, o:o + size])
        o += size
    _, fk, fv, ff, gq, gk, gv, gb, ga, gz = parts
    fq = (w_in[:, :D_FOX] * (FOX_HEAD_DIM ** -0.5 * LOG2E)).astype(BF16)
    w_all = jnp.concatenate(
        [fq, fk, fv, gq, gk, gv, gz]
        + [ff] * FOX_REP
        + [jnp.zeros((d, SMALL_B0 - FOX_REP * N_FOX_HEADS), BF16), gb, ga,
           jnp.zeros((d, LANES - SMALL_A0 - N_GDN_HEADS), BF16)],
        axis=1)

    zeros_b = jnp.zeros((SMALL_A0,), F32)
    vecs = jnp.stack([
        _vec_row(jnp.tile(fox_f_bias, FOX_REP)),
        _vec_row(jnp.tile(fox_out_norm, 2)),
        _vec_row(jnp.concatenate([zeros_b, gdn_a_log.astype(F32)])),
        _vec_row(jnp.concatenate([zeros_b, gdn_dt_bias.astype(F32)])),
        _vec_row(gdn_out_norm),
        jnp.zeros((LANES,), F32), jnp.zeros((LANES,), F32), jnp.zeros((LANES,), F32)])

    fox_qkv, gdn_in, small = _in_proj(x2, pre_mix_norm.reshape(1, d).astype(F32), w_all,
                                      gdn_conv_w.astype(F32), s, tm=1024)
    fox_o = _fox(fox_qkv, small, vecs, b, s, tq=1024, tk=512, dg=512)
    gdn_o = _gdn(gdn_in, small, vecs, b, s, group=4)
    out = _out_mlp(fox_o, gdn_o, x2, w_out.astype(BF16),
                   post_mix_norm.reshape(1, d).astype(F32), pre_mlp_norm.reshape(1, d).astype(F32),
                   w_up.astype(BF16), w_down.astype(BF16), post_mlp_norm.reshape(1, d).astype(F32),
                   tm=512)
    return out.reshape(b, s, d)
```

```python
import functools

import jax
import jax.numpy as jnp
from jax import lax
from jax.experimental import pallas as pl
from jax.experimental.pallas import tpu as pltpu

F32 = jnp.float32
BF16 = jnp.bfloat16

EPS = 1e-6
LOG2E = 1.4426950408889634
LANES = 128
FOX_HEAD_DIM = 64
N_FOX_HEADS = 8
D_FOX = FOX_HEAD_DIM * N_FOX_HEADS
GDN_HEAD_DIM = 128
N_GDN_HEADS = 4
D_GDN = GDN_HEAD_DIM * N_GDN_HEADS
CHUNK = 64
CONV_K = 4

FOX_REP = 6
SMALL_B0 = 64
SMALL_A0 = SMALL_B0 + N_GDN_HEADS

VEC_FBIAS, VEC_FNORM, VEC_ALOG, VEC_DTB, VEC_ONORM = range(5)

VMEM_LIMIT = 56 * 1024 * 1024


def _dot(a, b):
    return jnp.dot(a, b, preferred_element_type=F32)


def _dot_nt(a, b):
    return lax.dot_general(a, b, (((1,), (1,)), ((), ())), preferred_element_type=F32)


def _dot_tn(a, b):
    return lax.dot_general(a, b, (((0,), (0,)), ((), ())), preferred_element_type=F32)


def _rms(x, w):
    return x * lax.rsqrt(jnp.mean(x * x, axis=-1, keepdims=True) + EPS) * w


def _split3(x):
    hi = x.astype(BF16)
    r1 = x - hi.astype(F32)
    mid = r1.astype(BF16)
    lo = (r1 - mid.astype(F32)).astype(BF16)
    return hi, mid, lo


def _tril_ones(n, dtype, block=None):
    r = lax.broadcasted_iota(jnp.int32, (n, n), 0)
    c = lax.broadcasted_iota(jnp.int32, (n, n), 1)
    keep = c <= r
    if block is not None:
        sh = block.bit_length() - 1
        keep = keep & ((r >> sh) == (c >> sh))
    return jnp.where(keep, 1.0, 0.0).astype(dtype)


def _cumsum_rows(tril_bf16, x):
    hi, mid, lo = _split3(x)
    return _dot(tril_bf16, hi) + _dot(tril_bf16, mid) + _dot(tril_bf16, lo)


def _sigmoid(x):
    return 0.5 + 0.5 * jnp.tanh(0.5 * x)


def _silu(x):
    h = 0.5 * x
    return h + h * jnp.tanh(h)


def _softplus(x):
    return jnp.maximum(x, 0.0) + jnp.log1p(jnp.exp(-jnp.abs(x)))


def _in_proj_kernel(x_ref, g_ref, win_full_ref, wq_ref, ws_ref, cw_ref, fox_ref, gdn_ref, small_ref,
                    w_ref, halo_ref, win_ref, *, tiles_per_seq):
    i = pl.program_id(0)
    d = GDN_HEAD_DIM
    g0 = 3 * D_FOX

    @pl.when(i == 0)
    def _regroup_weights():
        gsrc = 3 * D_FOX + N_FOX_HEADS
        zsrc = gsrc + 3 * D_GDN + 2 * N_GDN_HEADS
        rb = 256
        for r in range(0, w_ref.shape[0], rb):
            rs = slice(r, r + rb)
            w_ref[rs, 0:D_FOX] = wq_ref[rs, :]
            w_ref[rs, D_FOX:g0] = win_full_ref[rs, D_FOX:g0]
            w_ref[rs, g0:g0 + 3 * D_GDN] = win_full_ref[rs, gsrc:gsrc + 3 * D_GDN]
            w_ref[rs, g0 + 3 * D_GDN:g0 + 4 * D_GDN] = win_full_ref[rs, zsrc:zsrc + D_GDN]

    h = _rms(x_ref[...], g_ref[...]).astype(BF16)
    tm = h.shape[0]
    first = i % tiles_per_seq == 0
    slab = 2 * d
    for sl in range(4 * D_GDN // slab):
        cols = slice(sl * slab, (sl + 1) * slab)
        if sl * slab < 3 * D_FOX:
            fox_ref[:, cols] = _dot(h, w_ref[:, cols]).astype(BF16)
        g = _dot(h, w_ref[:, g0 + sl * slab:g0 + (sl + 1) * slab])
        if sl * slab >= 3 * D_GDN:
            gdn_ref[:, cols] = _silu(g).astype(BF16)
            continue
        win_ref[0:8, :] = jnp.where(first, 0.0, halo_ref[:, cols])
        win_ref[8:, :] = g
        halo_ref[:, cols] = g[tm - 8:, :]
        w = cw_ref[:, cols]
        y = g * w[CONV_K - 1:CONV_K, :]
        for j in range(1, CONV_K):
            shifted = win_ref[8 - j:8 - j + tm, :]
            y = y + shifted * w[CONV_K - 1 - j:CONV_K - j, :]
        act = _silu(y)
        if sl * slab >= 2 * D_GDN:
            gdn_ref[:, cols] = act.astype(BF16)
            continue
        for hd in range(slab // d):
            t = act[:, hd * d:(hd + 1) * d]
            r = lax.rsqrt(jnp.sum(t * t, axis=-1, keepdims=True) + EPS)
            if sl * slab < D_GDN:
                r = r * (d ** -0.5)
            gdn_ref[:, sl * slab + hd * d:sl * slab + (hd + 1) * d] = (t * r).astype(BF16)
    small_ref[...] = _dot(h, ws_ref[...])


def _in_proj(x2, gain, w_bf, wq, ws, conv_w, seq, tm):
    n, d = x2.shape
    const = lambda i: (0, 0)
    row = lambda i: (i, 0)
    kern = functools.partial(_in_proj_kernel, tiles_per_seq=seq // tm)
    return pl.pallas_call(
        kern,
        grid=(n // tm,),
        in_specs=[
            pl.BlockSpec((tm, d), row),
            pl.BlockSpec((1, d), const),
            pl.BlockSpec(w_bf.shape, const, pipeline_mode=pl.Buffered(1)),
            pl.BlockSpec(wq.shape, const, pipeline_mode=pl.Buffered(1)),
            pl.BlockSpec(ws.shape, const, pipeline_mode=pl.Buffered(1)),
            pl.BlockSpec(conv_w.shape, const),
        ],
        out_specs=[
            pl.BlockSpec((tm, 3 * D_FOX), row),
            pl.BlockSpec((tm, 4 * D_GDN), row),
            pl.BlockSpec((tm, LANES), row),
        ],
        out_shape=[
            jax.ShapeDtypeStruct((n, 3 * D_FOX), BF16),
            jax.ShapeDtypeStruct((n, 4 * D_GDN), BF16),
            jax.ShapeDtypeStruct((n, LANES), F32),
        ],
        scratch_shapes=[pltpu.VMEM((d, 3 * D_FOX + 4 * D_GDN), BF16),
                        pltpu.VMEM((8, 3 * D_GDN), F32),
                        pltpu.VMEM((tm + 8, 2 * GDN_HEAD_DIM), F32)],
        compiler_params=pltpu.CompilerParams(
            dimension_semantics=("arbitrary",), vmem_limit_bytes=VMEM_LIMIT),
        name="in_proj",
    )(x2, gain, w_bf, wq, ws, conv_w)


def _fox_kernel(q_ref, k_ref, v_ref, small_ref, vec_ref, o_ref,
                qx_ref, kx_ref, vaug_ref, *, seq, tq, tk, dg, cum_blk):
    p = pl.program_id(1)
    qi = pl.program_id(2)
    lane = lax.broadcasted_iota(jnp.int32, (1, LANES), 1)
    half = FOX_HEAD_DIM
    nh = N_FOX_HEADS
    head_mask = (lane < half, lane >= half)
    ones_lane = (half, 0)
    zero = jnp.zeros((), BF16)

    @pl.when((p == 0) & (qi == 0))
    def _per_sequence():
        tril = _tril_ones(cum_blk, BF16)
        carry = jnp.zeros((1, LANES), F32)
        for blk in range(seq // cum_blk):
            rows = slice(blk * cum_blk, (blk + 1) * cum_blk)
            z = small_ref[rows, :] + vec_ref[VEC_FBIAS:VEC_FBIAS + 1, :]
            c = _cumsum_rows(tril, -_softplus(-z)) + carry
            carry = c[cum_blk - 1:cum_blk, :]
            hi, mid, lo = (t.astype(F32) for t in _split3(c * LOG2E))
            qx_ref[rows, :] = jnp.where(lane < nh, hi, jnp.where(lane < 2 * nh, mid, jnp.where(
                lane < 3 * nh, lo, jnp.where(lane < 6 * nh, 1.0, 0.0)))).astype(BF16)
            kx_ref[rows, :] = jnp.where(lane < 3 * nh, 1.0, jnp.where(lane < 4 * nh, -hi, jnp.where(
                lane < 5 * nh, -mid, jnp.where(lane < 6 * nh, -lo, 0.0)))).astype(BF16)

    @pl.when(qi == 0)
    def _per_pair():
        vp = v_ref[...]
        for e in range(2):
            ones_col = jnp.where(lane == ones_lane[e], 1.0, 0.0).astype(BF16)
            vaug_ref[e] = jnp.where(head_mask[e], vp, ones_col)

    r0 = pl.multiple_of(qi * tq, tq)
    qs = q_ref[...]
    qx = qx_ref[pl.ds(r0, tq), :]
    qaug = []
    for e in range(2):
        xmask = ((lane & (nh - 1)) == 2 * p + e) & (lane < 6 * nh)
        qaug.append(jnp.concatenate(
            [jnp.where(head_mask[e], qs, zero), jnp.where(xmask, qx, zero)], axis=1))

    def block(j0, width, row_lo, carry, masked):
        kb = jnp.concatenate([k_ref[pl.ds(j0, width), :], kx_ref[pl.ds(j0, width), :]], axis=1)
        nrows = tq - row_lo
        out = []
        for e in range(2):
            m_all, acc_all = carry[e]
            m, acc = m_all[row_lo:], acc_all[row_lo:]
            vb = vaug_ref[e, pl.ds(j0, width), :]
            s = _dot_nt(qaug[e][row_lo:], kb)
            if masked:
                rr = lax.broadcasted_iota(jnp.int32, (width, width), 0)
                cc = lax.broadcasted_iota(jnp.int32, (width, width), 1)
                top = jnp.where(cc <= rr, s[:width], -jnp.inf)
                s = top if nrows == width else jnp.concatenate([top, s[width:]], axis=0)
            m_new = jnp.maximum(m, jnp.max(s, axis=-1, keepdims=True))
            alpha = jnp.exp2(m - m_new)
            pexp = jnp.exp2(s - m_new)
            acc = alpha * acc + _dot(pexp.astype(BF16), vb)
            if row_lo:
                m_new = jnp.concatenate([m_all[:row_lo], m_new], axis=0)
                acc = jnp.concatenate([acc_all[:row_lo], acc], axis=0)
            out.append((m_new, acc))
        return tuple(out)

    init = tuple((jnp.full((tq, 1), -jnp.inf, F32), jnp.zeros((tq, LANES), F32)) for _ in range(2))
    carry = lax.fori_loop(
        0, qi * (tq // tk),
        lambda j, c: block(pl.multiple_of(j * tk, tk), tk, 0, c, False), init)
    for c in range(tq // dg):
        carry = block(pl.multiple_of(r0 + c * dg, dg), dg, c * dg, carry, True)

    outs = []
    for e in range(2):
        _, acc = carry[e]
        l = jnp.sum(jnp.where(lane == ones_lane[e], acc, 0.0), axis=-1, keepdims=True)
        outs.append(acc / l)
    o = jnp.where(head_mask[0], outs[0], outs[1])
    o2 = o * o
    ss0 = jnp.sum(jnp.where(head_mask[0], o2, 0.0), axis=-1, keepdims=True)
    ss1 = jnp.sum(jnp.where(head_mask[1], o2, 0.0), axis=-1, keepdims=True)
    ms = jnp.where(head_mask[0], ss0, ss1) * (1.0 / FOX_HEAD_DIM)
    o_ref[...] = (o * lax.rsqrt(ms + EPS) * vec_ref[VEC_FNORM:VEC_FNORM + 1, :]).astype(o_ref.dtype)


def _fox(fox_qkv, small, vecs, batch, seq, tq, tk, dg):
    n = batch * seq
    nq = seq // tq
    pairs = N_FOX_HEADS // 2
    kern = functools.partial(_fox_kernel, seq=seq, tq=tq, tk=tk, dg=dg, cum_blk=256)
    return pl.pallas_call(
        kern,
        grid=(batch, pairs, nq),
        in_specs=[
            pl.BlockSpec((tq, LANES), lambda b, p, i: (b * nq + i, p)),
            pl.BlockSpec((seq, LANES), lambda b, p, i: (b, pairs + p)),
            pl.BlockSpec((seq, LANES), lambda b, p, i: (b, 2 * pairs + p)),
            pl.BlockSpec((seq, LANES), lambda b, p, i: (b, 0)),
            pl.BlockSpec(vecs.shape, lambda b, p, i: (0, 0)),
        ],
        out_specs=pl.BlockSpec((tq, LANES), lambda b, p, i: (b * nq + i, p)),
        out_shape=jax.ShapeDtypeStruct((n, D_FOX), BF16),
        scratch_shapes=[
            pltpu.VMEM((seq, LANES), BF16),
            pltpu.VMEM((seq, LANES), BF16),
            pltpu.VMEM((2, seq, LANES), BF16),
        ],
        compiler_params=pltpu.CompilerParams(
            dimension_semantics=("arbitrary", "arbitrary", "arbitrary"),
            vmem_limit_bytes=VMEM_LIMIT),
        name="fox_attention",
    )(fox_qkv, fox_qkv, fox_qkv, small, vecs)


def _merge_levels(xs, m_lows, sizes):
    c = m_lows[0].shape[0]
    row = lax.broadcasted_iota(jnp.int32, (c, c), 0)
    col = lax.broadcasted_iota(jnp.int32, (c, c), 1)
    for size in sizes:
        sh = size.bit_length() - 1
        rb = row >> sh
        below = ((rb & 1) == 1) & ((col >> sh) == rb - 1)
        xbfs = [x.astype(BF16) for x in xs]
        nxs = [_dot(jnp.where(below, m, 0.0).astype(BF16), xb).astype(BF16)
               for m, xb in zip(m_lows, xbfs)]
        xs = [x - _dot(xb, nx) for x, xb, nx in zip(xs, xbfs, nxs)]
        yield
    return xs


MERGE_SIZES = (2, 4, 8, 16, 32)


def _gdn_kernel(q_ref, k_ref, v_ref, gate_ref, small_ref, vec_ref, o_ref,
                w2_ref, rr_ref, qp_ref, op_ref, gl_ref, st_ref, *, seq, heads, group):
    c = CHUNK
    d = GDN_HEAD_DIM
    rows = group * c
    n_groups = seq // rows
    lane = lax.broadcasted_iota(jnp.int32, (1, LANES), 1)
    row = lax.broadcasted_iota(jnp.int32, (c, c), 0)
    col = lax.broadcasted_iota(jnp.int32, (c, c), 1)
    tril = _tril_ones(rows, BF16, block=c)

    def aligned(x, m):
        return x if isinstance(x, int) else pl.multiple_of(x, m)

    def prep_stages(gi):
        r0 = aligned(gi * rows, rows)
        qa = q_ref[pl.ds(r0, rows), :].astype(F32)
        ka = k_ref[pl.ds(r0, rows), :]
        va = v_ref[pl.ds(r0, rows), :].astype(F32)
        sm = small_ref[pl.ds(r0, rows), :]
        g_all = -jnp.exp(vec_ref[VEC_ALOG:VEC_ALOG + 1, :]) * _softplus(
            sm + vec_ref[VEC_DTB:VEC_DTB + 1, :])
        gc_all = _cumsum_rows(tril, g_all)
        items = []
        for h in range(heads):
            qh = qa[:, h * d:(h + 1) * d]
            kh = ka[:, h * d:(h + 1) * d]
            vh = va[:, h * d:(h + 1) * d]
            beta_h = _sigmoid(jnp.sum(jnp.where(lane == SMALL_B0 + h, sm, 0.0),
                                      axis=-1, keepdims=True))
            gc_h = jnp.sum(jnp.where(lane == SMALL_A0 + h, gc_all, 0.0),
                           axis=-1, keepdims=True)
            for g in range(group):
                sl = slice(g * c, (g + 1) * c)
                items.append(dict(h=h, g=g, q=qh[sl], k=kh[sl].astype(F32), kbf=kh[sl], v=vh[sl],
                                  beta=beta_h[sl], gc=gc_h[sl]))
        for it in items:
            gc = it["gc"]
            hi, mid, lo = (t.astype(F32) for t in _split3(gc))
            it["al"] = jnp.where(lane == 0, hi, jnp.where(lane == 1, mid, jnp.where(
                lane == 2, lo, jnp.where(lane < 6, 1.0, 0.0)))).astype(BF16)
            it["ar"] = jnp.where(lane < 3, 1.0, jnp.where(lane == 3, -hi, jnp.where(
                lane == 4, -mid, jnp.where(lane == 5, -lo, 0.0)))).astype(BF16)
            it["kb"] = it["k"] * it["beta"]
        dmats = [_dot_nt(it["al"], it["ar"]) for it in items]
        kks = [_dot_nt(it["kb"].astype(BF16), it["kbf"]) for it in items]
        qks = [_dot_nt(it["q"].astype(BF16), it["kbf"]) for it in items]
        m_lows = []
        for it, dmat, kk, qk in zip(items, dmats, kks, qks):
            decay = jnp.exp(jnp.where(col <= row, dmat, -jnp.inf))
            m_lows.append(jnp.where(col < row, kk * decay, 0.0))
            it["a"] = (qk * decay).astype(BF16)
        yield
        eye = jnp.where(row == col, 1.0, 0.0)
        pair = (row >> 1) == (col >> 1)
        xs = [eye - jnp.where(pair, m, 0.0) for m in m_lows]
        t_invs = yield from _merge_levels(xs, m_lows, MERGE_SIZES)
        wus = []
        for it, t_inv in zip(items, t_invs):
            eg = jnp.exp(it["gc"])
            it["eg"] = eg
            rhs = jnp.concatenate(
                [(it["kb"] * eg).astype(BF16), (it["v"] * it["beta"]).astype(BF16)], axis=1)
            wus.append(_dot(t_inv.astype(BF16), rhs).astype(BF16))
        yield
        kwus, awus = [], []
        for it, wu in zip(items, wus):
            gc_last = it["gc"][c - 1:c, :]
            it["gl"] = jnp.exp(gc_last)
            kd = (it["k"] * jnp.exp(gc_last - it["gc"])).astype(BF16)
            kwus.append(_dot_tn(kd, wu))
            awus.append(_dot(it["a"], wu))
        for it, kwu, awu in zip(items, kwus, awus):
            h = it["h"]
            ci = gi * group + it["g"]
            rw = aligned(ci * d, d)
            rc = aligned(ci * c, c)
            w2_ref[h, pl.ds(rw, d), :] = kwu[:, :d].astype(BF16)
            rr_ref[h, pl.ds(rw, d), :] = kwu[:, d:]
            qp_ref[h, pl.ds(rc, c), :] = (it["q"] * it["eg"] - awu[:, :d]).astype(BF16)
            op_ref[h, pl.ds(rc, c), :] = awu[:, d:]
            gl_ref[h, pl.ds(aligned(ci * 8, 8), 8), :] = jnp.broadcast_to(it["gl"], (8, d))

    def scan_chunk(ci):
        rc = aligned(ci * c, c)
        rw = aligned(ci * d, d)
        for h in range(heads):
            state = st_ref[h]
            lhs = jnp.concatenate([w2_ref[h, pl.ds(rw, d), :], qp_ref[h, pl.ds(rc, c), :]], axis=0)
            r = _dot(lhs, state.astype(BF16))
            gl = gl_ref[h, pl.ds(aligned(ci * 8, 8), 8), :][0:1, :]
            st_ref[h] = state * gl - r[:d] + rr_ref[h, pl.ds(rw, d), :]
            o = r[d:] + op_ref[h, pl.ds(rc, c), :]
            gate = gate_ref[pl.ds(rc, c), h * d:(h + 1) * d].astype(F32)
            o_ref[pl.ds(rc, c), h * d:(h + 1) * d] = (
                _rms(o, vec_ref[VEC_ONORM:VEC_ONORM + 1, :]) * gate).astype(o_ref.dtype)

    def prep_and_scan(gi, scan_group):
        pending = [] if scan_group is None else [scan_group * group + j for j in range(group)]
        for _ in prep_stages(gi):
            if pending:
                scan_chunk(pending.pop(0))
        for ci in pending:
            scan_chunk(ci)

    st_ref[...] = jnp.zeros_like(st_ref)
    prep_and_scan(0, None)

    def body(gi, _):
        prep_and_scan(gi, gi - 1)
        return 0

    lax.fori_loop(1, n_groups, body, 0)
    for j in range(group):
        scan_chunk((n_groups - 1) * group + j)


def _gdn(gdn_in, small, vecs, batch, seq, group):
    n = batch * seq
    d = GDN_HEAD_DIM
    heads = N_GDN_HEADS
    n_chunks = seq // CHUNK
    kern = functools.partial(_gdn_kernel, seq=seq, heads=heads, group=group)
    col_block = lambda part: (lambda b: (b, part))
    const = lambda b: (0, 0)
    return pl.pallas_call(
        kern,
        grid=(batch,),
        in_specs=[
            pl.BlockSpec((seq, D_GDN), col_block(0)),
            pl.BlockSpec((seq, D_GDN), col_block(1)),
            pl.BlockSpec((seq, D_GDN), col_block(2)),
            pl.BlockSpec((seq, D_GDN), col_block(3)),
            pl.BlockSpec((seq, LANES), lambda b: (b, 0)),
            pl.BlockSpec(vecs.shape, const),
        ],
        out_specs=pl.BlockSpec((seq, D_GDN), lambda b: (b, 0)),
        out_shape=jax.ShapeDtypeStruct((n, D_GDN), BF16),
        scratch_shapes=[
            pltpu.VMEM((heads, n_chunks * d, d), BF16),
            pltpu.VMEM((heads, n_chunks * d, d), F32),
            pltpu.VMEM((heads, seq, d), BF16),
            pltpu.VMEM((heads, seq, d), F32),
            pltpu.VMEM((heads, n_chunks * 8, d), F32),
            pltpu.VMEM((heads, d, d), F32),
        ],
        compiler_params=pltpu.CompilerParams(
            dimension_semantics=("arbitrary",), vmem_limit_bytes=VMEM_LIMIT),
        name="gated_deltanet",
    )(gdn_in, gdn_in, gdn_in, gdn_in, small, vecs)


def _out_mlp_kernel(fox_ref, gdn_ref, x_ref, wo_ref, pmix_ref, pre_ref,
                    wup_ref, wdn_ref, post_ref, o_ref, *, ff_blk):
    mixed = _dot(fox_ref[...], wo_ref[:D_FOX, :]) + _dot(gdn_ref[...], wo_ref[D_FOX:, :])
    x1 = x_ref[...] + _rms(mixed, pmix_ref[...])
    h = _rms(x1, pre_ref[...]).astype(BF16)
    d_ff = wup_ref.shape[1]
    y = None
    for j in range(d_ff // ff_blk):
        a = _dot(h, wup_ref[:, j * ff_blk:(j + 1) * ff_blk])
        a = jnp.square(jnp.maximum(a, 0.0)).astype(BF16)
        t = _dot(a, wdn_ref[j * ff_blk:(j + 1) * ff_blk, :])
        y = t if y is None else y + t
    o_ref[...] = x1 + _rms(y, post_ref[...])


def _out_mlp(fox_o, gdn_o, x2, wo, pmix, pre, wup, wdn, post, tm):
    n, d = x2.shape
    const = lambda i: (0, 0)
    row = lambda i: (i, 0)
    single = pl.Buffered(1)
    kern = functools.partial(_out_mlp_kernel, ff_blk=1024)
    return pl.pallas_call(
        kern,
        grid=(n // tm,),
        in_specs=[
            pl.BlockSpec((tm, fox_o.shape[1]), row),
            pl.BlockSpec((tm, gdn_o.shape[1]), row),
            pl.BlockSpec((tm, d), row),
            pl.BlockSpec(wo.shape, const, pipeline_mode=single),
            pl.BlockSpec((1, d), const),
            pl.BlockSpec((1, d), const),
            pl.BlockSpec(wup.shape, const, pipeline_mode=single),
            pl.BlockSpec(wdn.shape, const, pipeline_mode=single),
            pl.BlockSpec((1, d), const),
        ],
        out_specs=pl.BlockSpec((tm, d), row),
        out_shape=jax.ShapeDtypeStruct((n, d), F32),
        compiler_params=pltpu.CompilerParams(
            dimension_semantics=("arbitrary",), vmem_limit_bytes=VMEM_LIMIT),
        name="out_mlp",
    )(fox_o, gdn_o, x2, wo, pmix, pre, wup, wdn, post)


def _vec_row(values):
    values = values.astype(F32)
    return jnp.concatenate([values, jnp.zeros((LANES - values.shape[0],), F32)])


def kernel(x, pre_mix_norm, w_in, fox_f_bias, fox_out_norm, gdn_conv_w, gdn_a_log, gdn_dt_bias,
           gdn_out_norm, w_out, post_mix_norm, pre_mlp_norm, w_up, w_down, post_mlp_norm):
    b, s, d = x.shape
    n = b * s
    x2 = x.reshape(n, d)

    w_bf = w_in.astype(BF16)
    wq = (w_in[:, :D_FOX] * (FOX_HEAD_DIM ** -0.5 * LOG2E)).astype(BF16)
    f0 = 3 * D_FOX
    b0 = f0 + N_FOX_HEADS + 3 * D_GDN
    ws = jnp.concatenate(
        [w_bf[:, f0:f0 + N_FOX_HEADS]] * FOX_REP
        + [jnp.zeros((d, SMALL_B0 - FOX_REP * N_FOX_HEADS), BF16), w_bf[:, b0:b0 + 2 * N_GDN_HEADS],
           jnp.zeros((d, LANES - SMALL_A0 - N_GDN_HEADS), BF16)], axis=1)

    zeros_b = jnp.zeros((SMALL_A0,), F32)
    vecs = jnp.stack([
        _vec_row(jnp.tile(fox_f_bias, FOX_REP)),
        _vec_row(jnp.tile(fox_out_norm, 2)),
        _vec_row(jnp.concatenate([zeros_b, gdn_a_log.astype(F32)])),
        _vec_row(jnp.concatenate([zeros_b, gdn_dt_bias.astype(F32)])),
        _vec_row(gdn_out_norm),
        jnp.zeros((LANES,), F32), jnp.zeros((LANES,), F32), jnp.zeros((LANES,), F32)])

    fox_qkv, gdn_in, small = _in_proj(x2, pre_mix_norm.reshape(1, d).astype(F32), w_bf, wq, ws,
                                      gdn_conv_w.astype(F32), s, tm=1024)
    fox_o = _fox(fox_qkv, small, vecs, b, s, tq=1024, tk=512, dg=512)
    gdn_o = _gdn(gdn_in, small, vecs, b, s, group=4)
    out = _out_mlp(fox_o, gdn_o, x2, w_out.astype(BF16),
                   post_mix_norm.reshape(1, d).astype(F32), pre_mlp_norm.reshape(1, d).astype(F32),
                   w_up.astype(BF16), w_down.astype(BF16), post_mlp_norm.reshape(1, d).astype(F32),
                   tm=512)
    return out.reshape(b, s, d)
```

```python
import functools

import jax
import jax.numpy as jnp
from jax import lax
from jax.experimental import pallas as pl
from jax.experimental.pallas import tpu as pltpu

F32 = jnp.float32
BF16 = jnp.bfloat16

EPS = 1e-6
LOG2E = 1.4426950408889634
LANES = 128
FOX_HEAD_DIM = 64
N_FOX_HEADS = 8
D_FOX = FOX_HEAD_DIM * N_FOX_HEADS
GDN_HEAD_DIM = 128
N_GDN_HEADS = 4
D_GDN = GDN_HEAD_DIM * N_GDN_HEADS
CHUNK = 64
CONV_K = 4

FOX_REP = 6
SMALL_B0 = 64
SMALL_A0 = SMALL_B0 + N_GDN_HEADS

VEC_FBIAS, VEC_FNORM, VEC_ALOG, VEC_DTB, VEC_ONORM = range(5)

VMEM_LIMIT = 56 * 1024 * 1024


def _dot(a, b):
    return jnp.dot(a, b, preferred_element_type=F32)


def _dot_nt(a, b):
    return lax.dot_general(a, b, (((1,), (1,)), ((), ())), preferred_element_type=F32)


def _dot_tn(a, b):
    return lax.dot_general(a, b, (((0,), (0,)), ((), ())), preferred_element_type=F32)


def _rms(x, w):
    return x * lax.rsqrt(jnp.mean(x * x, axis=-1, keepdims=True) + EPS) * w


def _split3(x):
    hi = x.astype(BF16)
    r1 = x - hi.astype(F32)
    mid = r1.astype(BF16)
    lo = (r1 - mid.astype(F32)).astype(BF16)
    return hi, mid, lo


def _tril_ones(n, dtype, block=None):
    r = lax.broadcasted_iota(jnp.int32, (n, n), 0)
    c = lax.broadcasted_iota(jnp.int32, (n, n), 1)
    keep = c <= r
    if block is not None:
        sh = block.bit_length() - 1
        keep = keep & ((r >> sh) == (c >> sh))
    return jnp.where(keep, 1.0, 0.0).astype(dtype)


def _cumsum_rows(tril_bf16, x):
    hi, mid, lo = _split3(x)
    return _dot(tril_bf16, hi) + _dot(tril_bf16, mid) + _dot(tril_bf16, lo)


def _sigmoid(x):
    return 0.5 + 0.5 * jnp.tanh(0.5 * x)


def _silu(x):
    h = 0.5 * x
    return h + h * jnp.tanh(h)


def _softplus(x):
    return jnp.maximum(x, 0.0) + jnp.log1p(jnp.exp(-jnp.abs(x)))


def _in_proj_kernel(x_ref, g_ref, wt_ref, cw_ref, fox_ref, gdn_ref, small_ref,
                    w_ref, halo_ref, win_ref, *, tiles_per_seq):
    i = pl.program_id(0)
    d = GDN_HEAD_DIM
    g0 = 3 * D_FOX
    s0 = g0 + 4 * D_GDN

    @pl.when(i == 0)
    def _regroup_weights():
        fsrc = 3 * D_FOX
        gsrc = fsrc + N_FOX_HEADS
        bsrc = gsrc + 3 * D_GDN
        zsrc = bsrc + 2 * N_GDN_HEADS
        rb = 256

        def copy_rows(dst, src, count, scale=None):
            for r in range(0, count, rb):
                v = wt_ref[src + r:src + r + rb, :]
                if scale is not None:
                    v = v * scale
                w_ref[dst + r:dst + r + rb, :] = v.astype(BF16)

        copy_rows(0, 0, D_FOX, FOX_HEAD_DIM ** -0.5 * LOG2E)
        copy_rows(D_FOX, D_FOX, 2 * D_FOX)
        copy_rows(g0, gsrc, 3 * D_GDN)
        copy_rows(g0 + 3 * D_GDN, zsrc, D_GDN)
        ff = wt_ref[fsrc:fsrc + N_FOX_HEADS, :]
        ba = wt_ref[bsrc:bsrc + 2 * N_GDN_HEADS, :]
        zero8 = jnp.zeros_like(ff)
        pad_b = (SMALL_B0 - FOX_REP * N_FOX_HEADS) // 8
        pad_end = (LANES - SMALL_A0 - N_GDN_HEADS) // 8
        small = jnp.concatenate([ff] * FOX_REP + [zero8] * pad_b + [ba] + [zero8] * pad_end, axis=0)
        w_ref[s0:, :] = small.astype(BF16)

    h = _rms(x_ref[...], g_ref[...]).astype(BF16)
    tm = h.shape[0]
    first = i % tiles_per_seq == 0
    slab = 2 * d
    for sl in range(4 * D_GDN // slab):
        cols = slice(sl * slab, (sl + 1) * slab)
        if sl * slab < 3 * D_FOX:
            fox_ref[:, cols] = _dot_nt(h, w_ref[cols, :]).astype(BF16)
        g = _dot_nt(h, w_ref[g0 + sl * slab:g0 + (sl + 1) * slab, :])
        if sl * slab >= 3 * D_GDN:
            gdn_ref[:, cols] = _silu(g).astype(BF16)
            continue
        win_ref[0:8, :] = jnp.where(first, 0.0, halo_ref[:, cols])
        win_ref[8:, :] = g
        halo_ref[:, cols] = g[tm - 8:, :]
        w = cw_ref[:, cols]
        y = g * w[CONV_K - 1:CONV_K, :]
        for j in range(1, CONV_K):
            shifted = win_ref[8 - j:8 - j + tm, :]
            y = y + shifted * w[CONV_K - 1 - j:CONV_K - j, :]
        act = _silu(y)
        if sl * slab >= 2 * D_GDN:
            gdn_ref[:, cols] = act.astype(BF16)
            continue
        for hd in range(slab // d):
            t = act[:, hd * d:(hd + 1) * d]
            r = lax.rsqrt(jnp.sum(t * t, axis=-1, keepdims=True) + EPS)
            if sl * slab < D_GDN:
                r = r * (d ** -0.5)
            gdn_ref[:, sl * slab + hd * d:sl * slab + (hd + 1) * d] = (t * r).astype(BF16)
    small_ref[...] = _dot_nt(h, w_ref[s0:, :])


def _in_proj(x2, gain, w_t, conv_w, seq, tm):
    n, d = x2.shape
    const = lambda i: (0, 0)
    row = lambda i: (i, 0)
    kern = functools.partial(_in_proj_kernel, tiles_per_seq=seq // tm)
    return pl.pallas_call(
        kern,
        grid=(n // tm,),
        in_specs=[
            pl.BlockSpec((tm, d), row),
            pl.BlockSpec((1, d), const),
            pl.BlockSpec(w_t.shape, const, pipeline_mode=pl.Buffered(1)),
            pl.BlockSpec(conv_w.shape, const),
        ],
        out_specs=[
            pl.BlockSpec((tm, 3 * D_FOX), row),
            pl.BlockSpec((tm, 4 * D_GDN), row),
            pl.BlockSpec((tm, LANES), row),
        ],
        out_shape=[
            jax.ShapeDtypeStruct((n, 3 * D_FOX), BF16),
            jax.ShapeDtypeStruct((n, 4 * D_GDN), BF16),
            jax.ShapeDtypeStruct((n, LANES), F32),
        ],
        scratch_shapes=[pltpu.VMEM((3 * D_FOX + 4 * D_GDN + LANES, d), BF16),
                        pltpu.VMEM((8, 3 * D_GDN), F32),
                        pltpu.VMEM((tm + 8, 2 * GDN_HEAD_DIM), F32)],
        compiler_params=pltpu.CompilerParams(
            dimension_semantics=("arbitrary",), vmem_limit_bytes=VMEM_LIMIT),
        name="in_proj",
    )(x2, gain, w_t, conv_w)


def _fox_kernel(q_ref, k_ref, v_ref, small_ref, vec_ref, o_ref,
                qx_ref, kx_ref, vaug_ref, *, seq, tq, tk, dg, cum_blk):
    p = pl.program_id(1)
    qi = pl.program_id(2)
    lane = lax.broadcasted_iota(jnp.int32, (1, LANES), 1)
    half = FOX_HEAD_DIM
    nh = N_FOX_HEADS
    head_mask = (lane < half, lane >= half)
    ones_lane = (half, 0)
    zero = jnp.zeros((), BF16)

    @pl.when((p == 0) & (qi == 0))
    def _per_sequence():
        tril = _tril_ones(cum_blk, BF16)
        carry = jnp.zeros((1, LANES), F32)
        for blk in range(seq // cum_blk):
            rows = slice(blk * cum_blk, (blk + 1) * cum_blk)
            z = small_ref[rows, :] + vec_ref[VEC_FBIAS:VEC_FBIAS + 1, :]
            c = _cumsum_rows(tril, -_softplus(-z)) + carry
            carry = c[cum_blk - 1:cum_blk, :]
            hi, mid, lo = (t.astype(F32) for t in _split3(c * LOG2E))
            qx_ref[rows, :] = jnp.where(lane < nh, hi, jnp.where(lane < 2 * nh, mid, jnp.where(
                lane < 3 * nh, lo, jnp.where(lane < 6 * nh, 1.0, 0.0)))).astype(BF16)
            kx_ref[rows, :] = jnp.where(lane < 3 * nh, 1.0, jnp.where(lane < 4 * nh, -hi, jnp.where(
                lane < 5 * nh, -mid, jnp.where(lane < 6 * nh, -lo, 0.0)))).astype(BF16)

    @pl.when(qi == 0)
    def _per_pair():
        vp = v_ref[...]
        for e in range(2):
            ones_col = jnp.where(lane == ones_lane[e], 1.0, 0.0).astype(BF16)
            vaug_ref[e] = jnp.where(head_mask[e], vp, ones_col)

    r0 = pl.multiple_of(qi * tq, tq)
    qs = q_ref[...]
    qx = qx_ref[pl.ds(r0, tq), :]
    qaug = []
    for e in range(2):
        xmask = ((lane & (nh - 1)) == 2 * p + e) & (lane < 6 * nh)
        qaug.append(jnp.concatenate(
            [jnp.where(head_mask[e], qs, zero), jnp.where(xmask, qx, zero)], axis=1))

    def block(j0, width, row_lo, carry, masked):
        kb = jnp.concatenate([k_ref[pl.ds(j0, width), :], kx_ref[pl.ds(j0, width), :]], axis=1)
        nrows = tq - row_lo
        out = []
        for e in range(2):
            m_all, acc_all = carry[e]
            m, acc = m_all[row_lo:], acc_all[row_lo:]
            vb = vaug_ref[e, pl.ds(j0, width), :]
            s = _dot_nt(qaug[e][row_lo:], kb)
            if masked:
                rr = lax.broadcasted_iota(jnp.int32, (width, width), 0)
                cc = lax.broadcasted_iota(jnp.int32, (width, width), 1)
                top = jnp.where(cc <= rr, s[:width], -jnp.inf)
                s = top if nrows == width else jnp.concatenate([top, s[width:]], axis=0)
            m_new = jnp.maximum(m, jnp.max(s, axis=-1, keepdims=True))
            alpha = jnp.exp2(m - m_new)
            pexp = jnp.exp2(s - m_new)
            acc = alpha * acc + _dot(pexp.astype(BF16), vb)
            if row_lo:
                m_new = jnp.concatenate([m_all[:row_lo], m_new], axis=0)
                acc = jnp.concatenate([acc_all[:row_lo], acc], axis=0)
            out.append((m_new, acc))
        return tuple(out)

    init = tuple((jnp.full((tq, 1), -jnp.inf, F32), jnp.zeros((tq, LANES), F32)) for _ in range(2))
    carry = lax.fori_loop(
        0, qi * (tq // tk),
        lambda j, c: block(pl.multiple_of(j * tk, tk), tk, 0, c, False), init)
    for c in range(tq // dg):
        carry = block(pl.multiple_of(r0 + c * dg, dg), dg, c * dg, carry, True)

    outs = []
    for e in range(2):
        _, acc = carry[e]
        l = jnp.sum(jnp.where(lane == ones_lane[e], acc, 0.0), axis=-1, keepdims=True)
        outs.append(acc / l)
    o = jnp.where(head_mask[0], outs[0], outs[1])
    o2 = o * o
    ss0 = jnp.sum(jnp.where(head_mask[0], o2, 0.0), axis=-1, keepdims=True)
    ss1 = jnp.sum(jnp.where(head_mask[1], o2, 0.0), axis=-1, keepdims=True)
    ms = jnp.where(head_mask[0], ss0, ss1) * (1.0 / FOX_HEAD_DIM)
    o_ref[...] = (o * lax.rsqrt(ms + EPS) * vec_ref[VEC_FNORM:VEC_FNORM + 1, :]).astype(o_ref.dtype)


def _fox(fox_qkv, small, vecs, batch, seq, tq, tk, dg):
    n = batch * seq
    nq = seq // tq
    pairs = N_FOX_HEADS // 2
    kern = functools.partial(_fox_kernel, seq=seq, tq=tq, tk=tk, dg=dg, cum_blk=256)
    return pl.pallas_call(
        kern,
        grid=(batch, pairs, nq),
        in_specs=[
            pl.BlockSpec((tq, LANES), lambda b, p, i: (b * nq + i, p)),
            pl.BlockSpec((seq, LANES), lambda b, p, i: (b, pairs + p)),
            pl.BlockSpec((seq, LANES), lambda b, p, i: (b, 2 * pairs + p)),
            pl.BlockSpec((seq, LANES), lambda b, p, i: (b, 0)),
            pl.BlockSpec(vecs.shape, lambda b, p, i: (0, 0)),
        ],
        out_specs=pl.BlockSpec((tq, LANES), lambda b, p, i: (b * nq + i, p)),
        out_shape=jax.ShapeDtypeStruct((n, D_FOX), BF16),
        scratch_shapes=[
            pltpu.VMEM((seq, LANES), BF16),
            pltpu.VMEM((seq, LANES), BF16),
            pltpu.VMEM((2, seq, LANES), BF16),
        ],
        compiler_params=pltpu.CompilerParams(
            dimension_semantics=("arbitrary", "arbitrary", "arbitrary"),
            vmem_limit_bytes=VMEM_LIMIT),
        name="fox_attention",
    )(fox_qkv, fox_qkv, fox_qkv, small, vecs)


def _merge_levels(xs, m_lows, sizes):
    c = m_lows[0].shape[0]
    row = lax.broadcasted_iota(jnp.int32, (c, c), 0)
    col = lax.broadcasted_iota(jnp.int32, (c, c), 1)
    for size in sizes:
        sh = size.bit_length() - 1
        rb = row >> sh
        below = ((rb & 1) == 1) & ((col >> sh) == rb - 1)
        xbfs = [x.astype(BF16) for x in xs]
        nxs = [_dot(jnp.where(below, m, 0.0).astype(BF16), xb).astype(BF16)
               for m, xb in zip(m_lows, xbfs)]
        xs = [x - _dot(xb, nx) for x, xb, nx in zip(xs, xbfs, nxs)]
        yield
    return xs


MERGE_SIZES = (2, 4, 8, 16, 32)


def _gdn_kernel(q_ref, k_ref, v_ref, gate_ref, small_ref, vec_ref, o_ref,
                w2_ref, rr_ref, qp_ref, op_ref, gl_ref, st_ref, *, seq, heads, group):
    c = CHUNK
    d = GDN_HEAD_DIM
    rows = group * c
    n_groups = seq // rows
    lane = lax.broadcasted_iota(jnp.int32, (1, LANES), 1)
    row = lax.broadcasted_iota(jnp.int32, (c, c), 0)
    col = lax.broadcasted_iota(jnp.int32, (c, c), 1)
    tril = _tril_ones(rows, BF16, block=c)

    def aligned(x, m):
        return x if isinstance(x, int) else pl.multiple_of(x, m)

    def prep_stages(gi):
        r0 = aligned(gi * rows, rows)
        qa = q_ref[pl.ds(r0, rows), :].astype(F32)
        ka = k_ref[pl.ds(r0, rows), :]
        va = v_ref[pl.ds(r0, rows), :].astype(F32)
        sm = small_ref[pl.ds(r0, rows), :]
        g_all = -jnp.exp(vec_ref[VEC_ALOG:VEC_ALOG + 1, :]) * _softplus(
            sm + vec_ref[VEC_DTB:VEC_DTB + 1, :])
        gc_all = _cumsum_rows(tril, g_all)
        items = []
        for h in range(heads):
            qh = qa[:, h * d:(h + 1) * d]
            kh = ka[:, h * d:(h + 1) * d]
            vh = va[:, h * d:(h + 1) * d]
            beta_h = _sigmoid(jnp.sum(jnp.where(lane == SMALL_B0 + h, sm, 0.0),
                                      axis=-1, keepdims=True))
            gc_h = jnp.sum(jnp.where(lane == SMALL_A0 + h, gc_all, 0.0),
                           axis=-1, keepdims=True)
            for g in range(group):
                sl = slice(g * c, (g + 1) * c)
                items.append(dict(h=h, g=g, q=qh[sl], k=kh[sl].astype(F32), kbf=kh[sl], v=vh[sl],
                                  beta=beta_h[sl], gc=gc_h[sl]))
        for it in items:
            gc = it["gc"]
            hi, mid, lo = (t.astype(F32) for t in _split3(gc))
            it["al"] = jnp.where(lane == 0, hi, jnp.where(lane == 1, mid, jnp.where(
                lane == 2, lo, jnp.where(lane < 6, 1.0, 0.0)))).astype(BF16)
            it["ar"] = jnp.where(lane < 3, 1.0, jnp.where(lane == 3, -hi, jnp.where(
                lane == 4, -mid, jnp.where(lane == 5, -lo, 0.0)))).astype(BF16)
            it["kb"] = it["k"] * it["beta"]
        dmats = [_dot_nt(it["al"], it["ar"]) for it in items]
        kks = [_dot_nt(it["kb"].astype(BF16), it["kbf"]) for it in items]
        qks = [_dot_nt(it["q"].astype(BF16), it["kbf"]) for it in items]
        m_lows = []
        for it, dmat, kk, qk in zip(items, dmats, kks, qks):
            decay = jnp.exp(jnp.where(col <= row, dmat, -jnp.inf))
            m_lows.append(jnp.where(col < row, kk * decay, 0.0))
            it["a"] = (qk * decay).astype(BF16)
        yield
        eye = jnp.where(row == col, 1.0, 0.0)
        pair = (row >> 1) == (col >> 1)
        xs = [eye - jnp.where(pair, m, 0.0) for m in m_lows]
        t_invs = yield from _merge_levels(xs, m_lows, MERGE_SIZES)
        wus = []
        for it, t_inv in zip(items, t_invs):
            eg = jnp.exp(it["gc"])
            it["eg"] = eg
            rhs = jnp.concatenate(
                [(it["kb"] * eg).astype(BF16), (it["v"] * it["beta"]).astype(BF16)], axis=1)
            wus.append(_dot(t_inv.astype(BF16), rhs).astype(BF16))
        yield
        kwus, awus = [], []
        for it, wu in zip(items, wus):
            gc_last = it["gc"][c - 1:c, :]
            it["gl"] = jnp.exp(gc_last)
            kd = (it["k"] * jnp.exp(gc_last - it["gc"])).astype(BF16)
            kwus.append(_dot_tn(kd, wu))
            awus.append(_dot(it["a"], wu))
        for it, kwu, awu in zip(items, kwus, awus):
            h = it["h"]
            ci = gi * group + it["g"]
            rw = aligned(ci * d, d)
            rc = aligned(ci * c, c)
            w2_ref[h, pl.ds(rw, d), :] = kwu[:, :d].astype(BF16)
            rr_ref[h, pl.ds(rw, d), :] = kwu[:, d:]
            qp_ref[h, pl.ds(rc, c), :] = (it["q"] * it["eg"] - awu[:, :d]).astype(BF16)
            op_ref[h, pl.ds(rc, c), :] = awu[:, d:]
            gl_ref[h, pl.ds(aligned(ci * 8, 8), 8), :] = jnp.broadcast_to(it["gl"], (8, d))

    def scan_chunk(ci):
        rc = aligned(ci * c, c)
        rw = aligned(ci * d, d)
        for h in range(heads):
            state = st_ref[h]
            lhs = jnp.concatenate([w2_ref[h, pl.ds(rw, d), :], qp_ref[h, pl.ds(rc, c), :]], axis=0)
            r = _dot(lhs, state.astype(BF16))
            gl = gl_ref[h, pl.ds(aligned(ci * 8, 8), 8), :][0:1, :]
            st_ref[h] = state * gl - r[:d] + rr_ref[h, pl.ds(rw, d), :]
            o = r[d:] + op_ref[h, pl.ds(rc, c), :]
            gate = gate_ref[pl.ds(rc, c), h * d:(h + 1) * d].astype(F32)
            o_ref[pl.ds(rc, c), h * d:(h + 1) * d] = (
                _rms(o, vec_ref[VEC_ONORM:VEC_ONORM + 1, :]) * gate).astype(o_ref.dtype)

    def prep_and_scan(gi, scan_group):
        pending = [] if scan_group is None else [scan_group * group + j for j in range(group)]
        for _ in prep_stages(gi):
            if pending:
                scan_chunk(pending.pop(0))
        for ci in pending:
            scan_chunk(ci)

    st_ref[...] = jnp.zeros_like(st_ref)
    prep_and_scan(0, None)

    def body(gi, _):
        prep_and_scan(gi, gi - 1)
        return 0

    lax.fori_loop(1, n_groups, body, 0)
    for j in range(group):
        scan_chunk((n_groups - 1) * group + j)


def _gdn(gdn_in, small, vecs, batch, seq, group):
    n = batch * seq
    d = GDN_HEAD_DIM
    heads = N_GDN_HEADS
    n_chunks = seq // CHUNK
    kern = functools.partial(_gdn_kernel, seq=seq, heads=heads, group=group)
    col_block = lambda part: (lambda b: (b, part))
    const = lambda b: (0, 0)
    return pl.pallas_call(
        kern,
        grid=(batch,),
        in_specs=[
            pl.BlockSpec((seq, D_GDN), col_block(0)),
            pl.BlockSpec((seq, D_GDN), col_block(1)),
            pl.BlockSpec((seq, D_GDN), col_block(2)),
            pl.BlockSpec((seq, D_GDN), col_block(3)),
            pl.BlockSpec((seq, LANES), lambda b: (b, 0)),
            pl.BlockSpec(vecs.shape, const),
        ],
        out_specs=pl.BlockSpec((seq, D_GDN), lambda b: (b, 0)),
        out_shape=jax.ShapeDtypeStruct((n, D_GDN), BF16),
        scratch_shapes=[
            pltpu.VMEM((heads, n_chunks * d, d), BF16),
            pltpu.VMEM((heads, n_chunks * d, d), F32),
            pltpu.VMEM((heads, seq, d), BF16),
            pltpu.VMEM((heads, seq, d), F32),
            pltpu.VMEM((heads, n_chunks * 8, d), F32),
            pltpu.VMEM((heads, d, d), F32),
        ],
        compiler_params=pltpu.CompilerParams(
            dimension_semantics=("arbitrary",), vmem_limit_bytes=VMEM_LIMIT),
        name="gated_deltanet",
    )(gdn_in, gdn_in, gdn_in, gdn_in, small, vecs)


def _out_mlp_kernel(fox_ref, gdn_ref, x_ref, wo_ref, pmix_ref, pre_ref,
                    wup_ref, wdn_ref, post_ref, o_ref, *, ff_blk):
    mixed = _dot(fox_ref[...], wo_ref[:D_FOX, :]) + _dot(gdn_ref[...], wo_ref[D_FOX:, :])
    x1 = x_ref[...] + _rms(mixed, pmix_ref[...])
    h = _rms(x1, pre_ref[...]).astype(BF16)
    d_ff = wup_ref.shape[1]
    y = None
    for j in range(d_ff // ff_blk):
        a = _dot(h, wup_ref[:, j * ff_blk:(j + 1) * ff_blk])
        a = jnp.square(jnp.maximum(a, 0.0)).astype(BF16)
        t = _dot(a, wdn_ref[j * ff_blk:(j + 1) * ff_blk, :])
        y = t if y is None else y + t
    o_ref[...] = x1 + _rms(y, post_ref[...])


def _out_mlp(fox_o, gdn_o, x2, wo, pmix, pre, wup, wdn, post, tm):
    n, d = x2.shape
    const = lambda i: (0, 0)
    row = lambda i: (i, 0)
    single = pl.Buffered(1)
    kern = functools.partial(_out_mlp_kernel, ff_blk=1024)
    return pl.pallas_call(
        kern,
        grid=(n // tm,),
        in_specs=[
            pl.BlockSpec((tm, fox_o.shape[1]), row),
            pl.BlockSpec((tm, gdn_o.shape[1]), row),
            pl.BlockSpec((tm, d), row),
            pl.BlockSpec(wo.shape, const, pipeline_mode=single),
            pl.BlockSpec((1, d), const),
            pl.BlockSpec((1, d), const),
            pl.BlockSpec(wup.shape, const, pipeline_mode=single),
            pl.BlockSpec(wdn.shape, const, pipeline_mode=single),
            pl.BlockSpec((1, d), const),
        ],
        out_specs=pl.BlockSpec((tm, d), row),
        out_shape=jax.ShapeDtypeStruct((n, d), F32),
        compiler_params=pltpu.CompilerParams(
            dimension_semantics=("arbitrary",), vmem_limit_bytes=VMEM_LIMIT),
        name="out_mlp",
    )(fox_o, gdn_o, x2, wo, pmix, pre, wup, wdn, post)


def _vec_row(values):
    values = values.astype(F32)
    return jnp.concatenate([values, jnp.zeros((LANES - values.shape[0],), F32)])


def kernel(x, pre_mix_norm, w_in, fox_f_bias, fox_out_norm, gdn_conv_w, gdn_a_log, gdn_dt_bias,
           gdn_out_norm, w_out, post_mix_norm, pre_mlp_norm, w_up, w_down, post_mlp_norm):
    b, s, d = x.shape
    n = b * s
    x2 = x.reshape(n, d)


    zeros_b = jnp.zeros((SMALL_A0,), F32)
    vecs = jnp.stack([
        _vec_row(jnp.tile(fox_f_bias, FOX_REP)),
        _vec_row(jnp.tile(fox_out_norm, 2)),
        _vec_row(jnp.concatenate([zeros_b, gdn_a_log.astype(F32)])),
        _vec_row(jnp.concatenate([zeros_b, gdn_dt_bias.astype(F32)])),
        _vec_row(gdn_out_norm),
        jnp.zeros((LANES,), F32), jnp.zeros((LANES,), F32), jnp.zeros((LANES,), F32)])

    fox_qkv, gdn_in, small = _in_proj(x2, pre_mix_norm.reshape(1, d).astype(F32),
                                      w_in.astype(F32).T, gdn_conv_w.astype(F32), s, tm=1024)
    fox_o = _fox(fox_qkv, small, vecs, b, s, tq=1024, tk=512, dg=512)
    gdn_o = _gdn(gdn_in, small, vecs, b, s, group=4)
    out = _out_mlp(fox_o, gdn_o, x2, w_out.astype(BF16),
                   post_mix_norm.reshape(1, d).astype(F32), pre_mlp_norm.reshape(1, d).astype(F32),
                   w_up.astype(BF16), w_down.astype(BF16), post_mlp_norm.reshape(1, d).astype(F32),
                   tm=512)
    return out.reshape(b, s, d)
```

```python
import functools

import jax
import jax.numpy as jnp
from jax import lax
from jax.experimental import pallas as pl
from jax.experimental.pallas import tpu as pltpu

F32 = jnp.float32
BF16 = jnp.bfloat16

EPS = 1e-6
LOG2E = 1.4426950408889634
LANES = 128
FOX_HEAD_DIM = 64
N_FOX_HEADS = 8
D_FOX = FOX_HEAD_DIM * N_FOX_HEADS
GDN_HEAD_DIM = 128
N_GDN_HEADS = 4
D_GDN = GDN_HEAD_DIM * N_GDN_HEADS
CHUNK = 64
CONV_K = 4

FOX_REP = 6
SMALL_B0 = 64
SMALL_A0 = SMALL_B0 + N_GDN_HEADS

VEC_FBIAS, VEC_FNORM, VEC_ALOG, VEC_DTB, VEC_ONORM = range(5)

VMEM_LIMIT = 56 * 1024 * 1024


def _dot(a, b):
    return jnp.dot(a, b, preferred_element_type=F32)


def _dot_nt(a, b):
    return lax.dot_general(a, b, (((1,), (1,)), ((), ())), preferred_element_type=F32)


def _dot_tn(a, b):
    return lax.dot_general(a, b, (((0,), (0,)), ((), ())), preferred_element_type=F32)


def _rms(x, w):
    return x * lax.rsqrt(jnp.mean(x * x, axis=-1, keepdims=True) + EPS) * w


def _split3(x):
    hi = x.astype(BF16)
    r1 = x - hi.astype(F32)
    mid = r1.astype(BF16)
    lo = (r1 - mid.astype(F32)).astype(BF16)
    return hi, mid, lo


def _tril_ones(n, dtype, block=None):
    r = lax.broadcasted_iota(jnp.int32, (n, n), 0)
    c = lax.broadcasted_iota(jnp.int32, (n, n), 1)
    keep = c <= r
    if block is not None:
        sh = block.bit_length() - 1
        keep = keep & ((r >> sh) == (c >> sh))
    return jnp.where(keep, 1.0, 0.0).astype(dtype)


def _cumsum_rows(tril_bf16, x):
    hi, mid, lo = _split3(x)
    return _dot(tril_bf16, hi) + _dot(tril_bf16, mid) + _dot(tril_bf16, lo)


def _sigmoid(x):
    return 0.5 + 0.5 * jnp.tanh(0.5 * x)


def _silu(x):
    h = 0.5 * x
    return h + h * jnp.tanh(h)


def _softplus(x):
    return jnp.maximum(x, 0.0) + jnp.log1p(jnp.exp(-jnp.abs(x)))


def _in_proj_kernel(x_ref, g_ref, wt_ref, cw_ref, fox_ref, gdn_ref, small_ref,
                    w_ref, halo_ref, win_ref, *, tiles_per_seq):
    i = pl.program_id(0)
    d = GDN_HEAD_DIM
    g0 = 3 * D_FOX
    s0 = g0 + 4 * D_GDN

    @pl.when(i == 0)
    def _regroup_weights():
        fsrc = 3 * D_FOX
        gsrc = fsrc + N_FOX_HEADS
        bsrc = gsrc + 3 * D_GDN
        zsrc = bsrc + 2 * N_GDN_HEADS
        rb = 256

        def copy_rows(dst, src, count, scale=None):
            for r in range(0, count, rb):
                v = wt_ref[src + r:src + r + rb, :]
                if scale is not None:
                    v = v * scale
                w_ref[dst + r:dst + r + rb, :] = v.astype(BF16)

        copy_rows(0, 0, D_FOX, FOX_HEAD_DIM ** -0.5 * LOG2E)
        copy_rows(D_FOX, D_FOX, 2 * D_FOX)
        copy_rows(g0, gsrc, 3 * D_GDN)
        copy_rows(g0 + 3 * D_GDN, zsrc, D_GDN)
        ff = wt_ref[fsrc:fsrc + N_FOX_HEADS, :]
        ba = wt_ref[bsrc:bsrc + 2 * N_GDN_HEADS, :]
        zero8 = jnp.zeros_like(ff)
        pad_b = (SMALL_B0 - FOX_REP * N_FOX_HEADS) // 8
        pad_end = (LANES - SMALL_A0 - N_GDN_HEADS) // 8
        small = jnp.concatenate([ff] * FOX_REP + [zero8] * pad_b + [ba] + [zero8] * pad_end, axis=0)
        w_ref[s0:, :] = small.astype(BF16)

    h = _rms(x_ref[...], g_ref[...]).astype(BF16)
    tm = h.shape[0]
    first = i % tiles_per_seq == 0
    slab = 2 * d
    for sl in range(4 * D_GDN // slab):
        cols = slice(sl * slab, (sl + 1) * slab)
        if sl * slab < 3 * D_FOX:
            fox_ref[:, cols] = _dot_nt(h, w_ref[cols, :]).astype(BF16)
        g = _dot_nt(h, w_ref[g0 + sl * slab:g0 + (sl + 1) * slab, :])
        if sl * slab >= 3 * D_GDN:
            gdn_ref[:, cols] = _silu(g).astype(BF16)
            continue
        win_ref[0:8, :] = jnp.where(first, 0.0, halo_ref[:, cols])
        win_ref[8:, :] = g
        halo_ref[:, cols] = g[tm - 8:, :]
        w = cw_ref[:, cols]
        y = g * w[CONV_K - 1:CONV_K, :]
        for j in range(1, CONV_K):
            shifted = win_ref[8 - j:8 - j + tm, :]
            y = y + shifted * w[CONV_K - 1 - j:CONV_K - j, :]
        act = _silu(y)
        if sl * slab >= 2 * D_GDN:
            gdn_ref[:, cols] = act.astype(BF16)
            continue
        for hd in range(slab // d):
            t = act[:, hd * d:(hd + 1) * d]
            r = lax.rsqrt(jnp.sum(t * t, axis=-1, keepdims=True) + EPS)
            if sl * slab < D_GDN:
                r = r * (d ** -0.5)
            gdn_ref[:, sl * slab + hd * d:sl * slab + (hd + 1) * d] = (t * r).astype(BF16)
    small_ref[...] = _dot_nt(h, w_ref[s0:, :])


def _in_proj(x2, gain, w_t, conv_w, seq, tm):
    n, d = x2.shape
    const = lambda i: (0, 0)
    row = lambda i: (i, 0)
    kern = functools.partial(_in_proj_kernel, tiles_per_seq=seq // tm)
    return pl.pallas_call(
        kern,
        grid=(n // tm,),
        in_specs=[
            pl.BlockSpec((tm, d), row),
            pl.BlockSpec((1, d), const),
            pl.BlockSpec(w_t.shape, const, pipeline_mode=pl.Buffered(1)),
            pl.BlockSpec(conv_w.shape, const),
        ],
        out_specs=[
            pl.BlockSpec((tm, 3 * D_FOX), row),
            pl.BlockSpec((tm, 4 * D_GDN), row),
            pl.BlockSpec((tm, LANES), row),
        ],
        out_shape=[
            jax.ShapeDtypeStruct((n, 3 * D_FOX), BF16),
            jax.ShapeDtypeStruct((n, 4 * D_GDN), BF16),
            jax.ShapeDtypeStruct((n, LANES), F32),
        ],
        scratch_shapes=[pltpu.VMEM((3 * D_FOX + 4 * D_GDN + LANES, d), BF16),
                        pltpu.VMEM((8, 3 * D_GDN), F32),
                        pltpu.VMEM((tm + 8, 2 * GDN_HEAD_DIM), F32)],
        compiler_params=pltpu.CompilerParams(
            dimension_semantics=("arbitrary",), vmem_limit_bytes=VMEM_LIMIT),
        name="in_proj",
    )(x2, gain, w_t, conv_w)


def _fox_kernel(q_ref, k_ref, v_ref, small_ref, vec_ref, *rest, seq, tq, tk, dg, cum_blk, n_cast):
    cast_in = rest[:n_cast]
    o_ref = rest[n_cast]
    cast_out = rest[n_cast + 1:2 * n_cast + 1]
    qx_ref, kx_ref, vaug_ref = rest[2 * n_cast + 1:]
    for src, dst in zip(cast_in, cast_out):
        dst[...] = src[...].astype(dst.dtype)
    p = pl.program_id(1)
    qi = pl.program_id(2)
    lane = lax.broadcasted_iota(jnp.int32, (1, LANES), 1)
    half = FOX_HEAD_DIM
    nh = N_FOX_HEADS
    head_mask = (lane < half, lane >= half)
    ones_lane = (half, 0)
    zero = jnp.zeros((), BF16)

    @pl.when((p == 0) & (qi == 0))
    def _per_sequence():
        tril = _tril_ones(cum_blk, BF16)
        carry = jnp.zeros((1, LANES), F32)
        for blk in range(seq // cum_blk):
            rows = slice(blk * cum_blk, (blk + 1) * cum_blk)
            z = small_ref[rows, :] + vec_ref[VEC_FBIAS:VEC_FBIAS + 1, :]
            c = _cumsum_rows(tril, -_softplus(-z)) + carry
            carry = c[cum_blk - 1:cum_blk, :]
            hi, mid, lo = (t.astype(F32) for t in _split3(c * LOG2E))
            qx_ref[rows, :] = jnp.where(lane < nh, hi, jnp.where(lane < 2 * nh, mid, jnp.where(
                lane < 3 * nh, lo, jnp.where(lane < 6 * nh, 1.0, 0.0)))).astype(BF16)
            kx_ref[rows, :] = jnp.where(lane < 3 * nh, 1.0, jnp.where(lane < 4 * nh, -hi, jnp.where(
                lane < 5 * nh, -mid, jnp.where(lane < 6 * nh, -lo, 0.0)))).astype(BF16)

    @pl.when(qi == 0)
    def _per_pair():
        vp = v_ref[...]
        for e in range(2):
            ones_col = jnp.where(lane == ones_lane[e], 1.0, 0.0).astype(BF16)
            vaug_ref[e] = jnp.where(head_mask[e], vp, ones_col)

    r0 = pl.multiple_of(qi * tq, tq)
    qs = q_ref[...]
    qx = qx_ref[pl.ds(r0, tq), :]
    qaug = []
    for e in range(2):
        xmask = ((lane & (nh - 1)) == 2 * p + e) & (lane < 6 * nh)
        qaug.append(jnp.concatenate(
            [jnp.where(head_mask[e], qs, zero), jnp.where(xmask, qx, zero)], axis=1))

    def block(j0, width, row_lo, carry, masked):
        kb = jnp.concatenate([k_ref[pl.ds(j0, width), :], kx_ref[pl.ds(j0, width), :]], axis=1)
        nrows = tq - row_lo
        out = []
        for e in range(2):
            m_all, acc_all = carry[e]
            m, acc = m_all[row_lo:], acc_all[row_lo:]
            vb = vaug_ref[e, pl.ds(j0, width), :]
            s = _dot_nt(qaug[e][row_lo:], kb)
            if masked:
                rr = lax.broadcasted_iota(jnp.int32, (width, width), 0)
                cc = lax.broadcasted_iota(jnp.int32, (width, width), 1)
                top = jnp.where(cc <= rr, s[:width], -jnp.inf)
                s = top if nrows == width else jnp.concatenate([top, s[width:]], axis=0)
            m_new = jnp.maximum(m, jnp.max(s, axis=-1, keepdims=True))
            alpha = jnp.exp2(m - m_new)
            pexp = jnp.exp2(s - m_new)
            acc = alpha * acc + _dot(pexp.astype(BF16), vb)
            if row_lo:
                m_new = jnp.concatenate([m_all[:row_lo], m_new], axis=0)
                acc = jnp.concatenate([acc_all[:row_lo], acc], axis=0)
            out.append((m_new, acc))
        return tuple(out)

    init = tuple((jnp.full((tq, 1), -jnp.inf, F32), jnp.zeros((tq, LANES), F32)) for _ in range(2))
    carry = lax.fori_loop(
        0, qi * (tq // tk),
        lambda j, c: block(pl.multiple_of(j * tk, tk), tk, 0, c, False), init)
    for c in range(tq // dg):
        carry = block(pl.multiple_of(r0 + c * dg, dg), dg, c * dg, carry, True)

    outs = []
    for e in range(2):
        _, acc = carry[e]
        l = jnp.sum(jnp.where(lane == ones_lane[e], acc, 0.0), axis=-1, keepdims=True)
        outs.append(acc / l)
    o = jnp.where(head_mask[0], outs[0], outs[1])
    o2 = o * o
    ss0 = jnp.sum(jnp.where(head_mask[0], o2, 0.0), axis=-1, keepdims=True)
    ss1 = jnp.sum(jnp.where(head_mask[1], o2, 0.0), axis=-1, keepdims=True)
    ms = jnp.where(head_mask[0], ss0, ss1) * (1.0 / FOX_HEAD_DIM)
    o_ref[...] = (o * lax.rsqrt(ms + EPS) * vec_ref[VEC_FNORM:VEC_FNORM + 1, :]).astype(o_ref.dtype)


def _fox(fox_qkv, small, vecs, cast_weights, batch, seq, tq, tk, dg):
    n = batch * seq
    nq = seq // tq
    pairs = N_FOX_HEADS // 2
    steps = batch * pairs * nq
    kern = functools.partial(_fox_kernel, seq=seq, tq=tq, tk=tk, dg=dg, cum_blk=256,
                             n_cast=len(cast_weights))
    slab = lambda b, p, i: ((b * pairs + p) * nq + i, 0)
    cast_specs = []
    for w in cast_weights:
        rows = w.shape[0] // steps
        assert rows * steps == w.shape[0] and rows % 16 == 0, w.shape
        cast_specs.append(pl.BlockSpec((rows, w.shape[1]), slab))
    outs = pl.pallas_call(
        kern,
        grid=(batch, pairs, nq),
        in_specs=[
            pl.BlockSpec((tq, LANES), lambda b, p, i: (b * nq + i, p)),
            pl.BlockSpec((seq, LANES), lambda b, p, i: (b, pairs + p)),
            pl.BlockSpec((seq, LANES), lambda b, p, i: (b, 2 * pairs + p)),
            pl.BlockSpec((seq, LANES), lambda b, p, i: (b, 0)),
            pl.BlockSpec(vecs.shape, lambda b, p, i: (0, 0)),
        ] + cast_specs,
        out_specs=[pl.BlockSpec((tq, LANES), lambda b, p, i: (b * nq + i, p))] + cast_specs,
        out_shape=[jax.ShapeDtypeStruct((n, D_FOX), BF16)]
        + [jax.ShapeDtypeStruct(w.shape, BF16) for w in cast_weights],
        scratch_shapes=[
            pltpu.VMEM((seq, LANES), BF16),
            pltpu.VMEM((seq, LANES), BF16),
            pltpu.VMEM((2, seq, LANES), BF16),
        ],
        compiler_params=pltpu.CompilerParams(
            dimension_semantics=("arbitrary", "arbitrary", "arbitrary"),
            vmem_limit_bytes=VMEM_LIMIT),
        name="fox_attention",
    )(fox_qkv, fox_qkv, fox_qkv, small, vecs, *cast_weights)
    return outs[0], outs[1:]


def _merge_levels(xs, m_lows, sizes):
    c = m_lows[0].shape[0]
    row = lax.broadcasted_iota(jnp.int32, (c, c), 0)
    col = lax.broadcasted_iota(jnp.int32, (c, c), 1)
    for size in sizes:
        sh = size.bit_length() - 1
        rb = row >> sh
        below = ((rb & 1) == 1) & ((col >> sh) == rb - 1)
        xbfs = [x.astype(BF16) for x in xs]
        nxs = [_dot(jnp.where(below, m, 0.0).astype(BF16), xb).astype(BF16)
               for m, xb in zip(m_lows, xbfs)]
        xs = [x - _dot(xb, nx) for x, xb, nx in zip(xs, xbfs, nxs)]
        yield
    return xs


MERGE_SIZES = (2, 4, 8, 16, 32)


def _gdn_kernel(q_ref, k_ref, v_ref, gate_ref, small_ref, vec_ref, o_ref,
                w2_ref, rr_ref, qp_ref, op_ref, gl_ref, st_ref, *, seq, heads, group):
    c = CHUNK
    d = GDN_HEAD_DIM
    rows = group * c
    n_groups = seq // rows
    lane = lax.broadcasted_iota(jnp.int32, (1, LANES), 1)
    row = lax.broadcasted_iota(jnp.int32, (c, c), 0)
    col = lax.broadcasted_iota(jnp.int32, (c, c), 1)
    tril = _tril_ones(rows, BF16, block=c)

    def aligned(x, m):
        return x if isinstance(x, int) else pl.multiple_of(x, m)

    def prep_stages(gi):
        r0 = aligned(gi * rows, rows)
        qa = q_ref[pl.ds(r0, rows), :].astype(F32)
        ka = k_ref[pl.ds(r0, rows), :]
        va = v_ref[pl.ds(r0, rows), :].astype(F32)
        sm = small_ref[pl.ds(r0, rows), :]
        g_all = -jnp.exp(vec_ref[VEC_ALOG:VEC_ALOG + 1, :]) * _softplus(
            sm + vec_ref[VEC_DTB:VEC_DTB + 1, :])
        gc_all = _cumsum_rows(tril, g_all)
        items = []
        for h in range(heads):
            qh = qa[:, h * d:(h + 1) * d]
            kh = ka[:, h * d:(h + 1) * d]
            vh = va[:, h * d:(h + 1) * d]
            beta_h = _sigmoid(jnp.sum(jnp.where(lane == SMALL_B0 + h, sm, 0.0),
                                      axis=-1, keepdims=True))
            gc_h = jnp.sum(jnp.where(lane == SMALL_A0 + h, gc_all, 0.0),
                           axis=-1, keepdims=True)
            for g in range(group):
                sl = slice(g * c, (g + 1) * c)
                items.append(dict(h=h, g=g, q=qh[sl], k=kh[sl].astype(F32), kbf=kh[sl], v=vh[sl],
                                  beta=beta_h[sl], gc=gc_h[sl]))
        for it in items:
            gc = it["gc"]
            hi, mid, lo = (t.astype(F32) for t in _split3(gc))
            it["al"] = jnp.where(lane == 0, hi, jnp.where(lane == 1, mid, jnp.where(
                lane == 2, lo, jnp.where(lane < 6, 1.0, 0.0)))).astype(BF16)
            it["ar"] = jnp.where(lane < 3, 1.0, jnp.where(lane == 3, -hi, jnp.where(
                lane == 4, -mid, jnp.where(lane == 5, -lo, 0.0)))).astype(BF16)
            it["kb"] = it["k"] * it["beta"]
        dmats = [_dot_nt(it["al"], it["ar"]) for it in items]
        kks = [_dot_nt(it["kb"].astype(BF16), it["kbf"]) for it in items]
        qks = [_dot_nt(it["q"].astype(BF16), it["kbf"]) for it in items]
        m_lows = []
        for it, dmat, kk, qk in zip(items, dmats, kks, qks):
            decay = jnp.exp(jnp.where(col <= row, dmat, -jnp.inf))
            m_lows.append(jnp.where(col < row, kk * decay, 0.0))
            it["a"] = (qk * decay).astype(BF16)
        yield
        eye = jnp.where(row == col, 1.0, 0.0)
        pair = (row >> 1) == (col >> 1)
        xs = [eye - jnp.where(pair, m, 0.0) for m in m_lows]
        t_invs = yield from _merge_levels(xs, m_lows, MERGE_SIZES)
        wus = []
        for it, t_inv in zip(items, t_invs):
            eg = jnp.exp(it["gc"])
            it["eg"] = eg
            rhs = jnp.concatenate(
                [(it["kb"] * eg).astype(BF16), (it["v"] * it["beta"]).astype(BF16)], axis=1)
            wus.append(_dot(t_inv.astype(BF16), rhs).astype(BF16))
        yield
        kwus, awus = [], []
        for it, wu in zip(items, wus):
            gc_last = it["gc"][c - 1:c, :]
            it["gl"] = jnp.exp(gc_last)
            kd = (it["k"] * jnp.exp(gc_last - it["gc"])).astype(BF16)
            kwus.append(_dot_tn(kd, wu))
            awus.append(_dot(it["a"], wu))
        for it, kwu, awu in zip(items, kwus, awus):
            h = it["h"]
            ci = gi * group + it["g"]
            rw = aligned(ci * d, d)
            rc = aligned(ci * c, c)
            w2_ref[h, pl.ds(rw, d), :] = kwu[:, :d].astype(BF16)
            rr_ref[h, pl.ds(rw, d), :] = kwu[:, d:]
            qp_ref[h, pl.ds(rc, c), :] = (it["q"] * it["eg"] - awu[:, :d]).astype(BF16)
            op_ref[h, pl.ds(rc, c), :] = awu[:, d:]
            gl_ref[h, pl.ds(aligned(ci * 8, 8), 8), :] = jnp.broadcast_to(it["gl"], (8, d))

    def scan_chunk(ci):
        rc = aligned(ci * c, c)
        rw = aligned(ci * d, d)
        for h in range(heads):
            state = st_ref[h]
            lhs = jnp.concatenate([w2_ref[h, pl.ds(rw, d), :], qp_ref[h, pl.ds(rc, c), :]], axis=0)
            r = _dot(lhs, state.astype(BF16))
            gl = gl_ref[h, pl.ds(aligned(ci * 8, 8), 8), :][0:1, :]
            st_ref[h] = state * gl - r[:d] + rr_ref[h, pl.ds(rw, d), :]
            o = r[d:] + op_ref[h, pl.ds(rc, c), :]
            gate = gate_ref[pl.ds(rc, c), h * d:(h + 1) * d].astype(F32)
            o_ref[pl.ds(rc, c), h * d:(h + 1) * d] = (
                _rms(o, vec_ref[VEC_ONORM:VEC_ONORM + 1, :]) * gate).astype(o_ref.dtype)

    def prep_and_scan(gi, scan_group):
        pending = [] if scan_group is None else [scan_group * group + j for j in range(group)]
        for _ in prep_stages(gi):
            if pending:
                scan_chunk(pending.pop(0))
        for ci in pending:
            scan_chunk(ci)

    st_ref[...] = jnp.zeros_like(st_ref)
    prep_and_scan(0, None)

    def body(gi, _):
        prep_and_scan(gi, gi - 1)
        return 0

    lax.fori_loop(1, n_groups, body, 0)
    for j in range(group):
        scan_chunk((n_groups - 1) * group + j)


def _gdn(gdn_in, small, vecs, batch, seq, group):
    n = batch * seq
    d = GDN_HEAD_DIM
    heads = N_GDN_HEADS
    n_chunks = seq // CHUNK
    kern = functools.partial(_gdn_kernel, seq=seq, heads=heads, group=group)
    col_block = lambda part: (lambda b: (b, part))
    const = lambda b: (0, 0)
    return pl.pallas_call(
        kern,
        grid=(batch,),
        in_specs=[
            pl.BlockSpec((seq, D_GDN), col_block(0)),
            pl.BlockSpec((seq, D_GDN), col_block(1)),
            pl.BlockSpec((seq, D_GDN), col_block(2)),
            pl.BlockSpec((seq, D_GDN), col_block(3)),
            pl.BlockSpec((seq, LANES), lambda b: (b, 0)),
            pl.BlockSpec(vecs.shape, const),
        ],
        out_specs=pl.BlockSpec((seq, D_GDN), lambda b: (b, 0)),
        out_shape=jax.ShapeDtypeStruct((n, D_GDN), BF16),
        scratch_shapes=[
            pltpu.VMEM((heads, n_chunks * d, d), BF16),
            pltpu.VMEM((heads, n_chunks * d, d), F32),
            pltpu.VMEM((heads, seq, d), BF16),
            pltpu.VMEM((heads, seq, d), F32),
            pltpu.VMEM((heads, n_chunks * 8, d), F32),
            pltpu.VMEM((heads, d, d), F32),
        ],
        compiler_params=pltpu.CompilerParams(
            dimension_semantics=("arbitrary",), vmem_limit_bytes=VMEM_LIMIT),
        name="gated_deltanet",
    )(gdn_in, gdn_in, gdn_in, gdn_in, small, vecs)


def _out_mlp_kernel(fox_ref, gdn_ref, x_ref, wo_ref, pmix_ref, pre_ref,
                    wup_ref, wdn_ref, post_ref, o_ref, *, ff_blk):
    mixed = _dot(fox_ref[...], wo_ref[:D_FOX, :]) + _dot(gdn_ref[...], wo_ref[D_FOX:, :])
    x1 = x_ref[...] + _rms(mixed, pmix_ref[...])
    h = _rms(x1, pre_ref[...]).astype(BF16)
    d_ff = wup_ref.shape[1]
    y = None
    for j in range(d_ff // ff_blk):
        a = _dot(h, wup_ref[:, j * ff_blk:(j + 1) * ff_blk])
        a = jnp.square(jnp.maximum(a, 0.0)).astype(BF16)
        t = _dot(a, wdn_ref[j * ff_blk:(j + 1) * ff_blk, :])
        y = t if y is None else y + t
    o_ref[...] = x1 + _rms(y, post_ref[...])


def _out_mlp(fox_o, gdn_o, x2, wo, pmix, pre, wup, wdn, post, tm):
    n, d = x2.shape
    const = lambda i: (0, 0)
    row = lambda i: (i, 0)
    single = pl.Buffered(1)
    kern = functools.partial(_out_mlp_kernel, ff_blk=1024)
    return pl.pallas_call(
        kern,
        grid=(n // tm,),
        in_specs=[
            pl.BlockSpec((tm, fox_o.shape[1]), row),
            pl.BlockSpec((tm, gdn_o.shape[1]), row),
            pl.BlockSpec((tm, d), row),
            pl.BlockSpec(wo.shape, const, pipeline_mode=single),
            pl.BlockSpec((1, d), const),
            pl.BlockSpec((1, d), const),
            pl.BlockSpec(wup.shape, const, pipeline_mode=single),
            pl.BlockSpec(wdn.shape, const, pipeline_mode=single),
            pl.BlockSpec((1, d), const),
        ],
        out_specs=pl.BlockSpec((tm, d), row),
        out_shape=jax.ShapeDtypeStruct((n, d), F32),
        compiler_params=pltpu.CompilerParams(
            dimension_semantics=("arbitrary",), vmem_limit_bytes=VMEM_LIMIT),
        name="out_mlp",
    )(fox_o, gdn_o, x2, wo, pmix, pre, wup, wdn, post)


def _vec_row(values):
    values = values.astype(F32)
    return jnp.concatenate([values, jnp.zeros((LANES - values.shape[0],), F32)])


def kernel(x, pre_mix_norm, w_in, fox_f_bias, fox_out_norm, gdn_conv_w, gdn_a_log, gdn_dt_bias,
           gdn_out_norm, w_out, post_mix_norm, pre_mlp_norm, w_up, w_down, post_mlp_norm):
    b, s, d = x.shape
    n = b * s
    x2 = x.reshape(n, d)


    zeros_b = jnp.zeros((SMALL_A0,), F32)
    vecs = jnp.stack([
        _vec_row(jnp.tile(fox_f_bias, FOX_REP)),
        _vec_row(jnp.tile(fox_out_norm, 2)),
        _vec_row(jnp.concatenate([zeros_b, gdn_a_log.astype(F32)])),
        _vec_row(jnp.concatenate([zeros_b, gdn_dt_bias.astype(F32)])),
        _vec_row(gdn_out_norm),
        jnp.zeros((LANES,), F32), jnp.zeros((LANES,), F32), jnp.zeros((LANES,), F32)])

    fox_qkv, gdn_in, small = _in_proj(x2, pre_mix_norm.reshape(1, d).astype(F32),
                                      w_in.astype(F32).T, gdn_conv_w.astype(F32), s, tm=1024)
    fox_o, (wo, wup, wdn) = _fox(
        fox_qkv, small, vecs, [w_out.astype(F32), w_up.astype(F32), w_down.astype(F32)],
        b, s, tq=1024, tk=512, dg=512)
    gdn_o = _gdn(gdn_in, small, vecs, b, s, group=4)
    out = _out_mlp(fox_o, gdn_o, x2, wo,
                   post_mix_norm.reshape(1, d).astype(F32), pre_mlp_norm.reshape(1, d).astype(F32),
                   wup, wdn, post_mlp_norm.reshape(1, d).astype(F32), tm=512)
    return out.reshape(b, s, d)
```

```python
import functools

import jax
import jax.numpy as jnp
from jax import lax
from jax.experimental import pallas as pl
from jax.experimental.pallas import tpu as pltpu

F32 = jnp.float32
BF16 = jnp.bfloat16

EPS = 1e-6
LOG2E = 1.4426950408889634
LANES = 128
FOX_HEAD_DIM = 64
N_FOX_HEADS = 8
D_FOX = FOX_HEAD_DIM * N_FOX_HEADS
GDN_HEAD_DIM = 128
N_GDN_HEADS = 4
D_GDN = GDN_HEAD_DIM * N_GDN_HEADS
CHUNK = 64
CONV_K = 4

FOX_REP = 6
SMALL_B0 = 64
SMALL_A0 = SMALL_B0 + N_GDN_HEADS

VEC_FBIAS, VEC_FNORM, VEC_ALOG, VEC_DTB, VEC_ONORM = range(5)

VMEM_LIMIT = 56 * 1024 * 1024


def _dot(a, b):
    return jnp.dot(a, b, preferred_element_type=F32)


def _dot_nt(a, b):
    return lax.dot_general(a, b, (((1,), (1,)), ((), ())), preferred_element_type=F32)


def _dot_tn(a, b):
    return lax.dot_general(a, b, (((0,), (0,)), ((), ())), preferred_element_type=F32)


def _rms(x, w):
    return x * lax.rsqrt(jnp.mean(x * x, axis=-1, keepdims=True) + EPS) * w


def _split3(x):
    hi = x.astype(BF16)
    r1 = x - hi.astype(F32)
    mid = r1.astype(BF16)
    lo = (r1 - mid.astype(F32)).astype(BF16)
    return hi, mid, lo


def _tril_ones(n, dtype, block=None):
    r = lax.broadcasted_iota(jnp.int32, (n, n), 0)
    c = lax.broadcasted_iota(jnp.int32, (n, n), 1)
    keep = c <= r
    if block is not None:
        sh = block.bit_length() - 1
        keep = keep & ((r >> sh) == (c >> sh))
    return jnp.where(keep, 1.0, 0.0).astype(dtype)


def _cumsum_rows(tril_bf16, x):
    hi, mid, lo = _split3(x)
    return _dot(tril_bf16, hi) + _dot(tril_bf16, mid) + _dot(tril_bf16, lo)


def _sigmoid(x):
    return 0.5 + 0.5 * jnp.tanh(0.5 * x)


def _silu(x):
    h = 0.5 * x
    return h + h * jnp.tanh(h)


def _softplus(x):
    return jnp.maximum(x, 0.0) + jnp.log1p(jnp.exp(-jnp.abs(x)))


def _in_proj_kernel(x_ref, g_ref, wt_ref, cw_ref, fox_ref, gdn_ref, small_ref,
                    w_ref, halo_ref, win_ref, *, tiles_per_seq):
    i = pl.program_id(0)
    d = GDN_HEAD_DIM
    g0 = 3 * D_FOX
    s0 = g0 + 4 * D_GDN

    @pl.when(i == 0)
    def _regroup_weights():
        fsrc = 3 * D_FOX
        gsrc = fsrc + N_FOX_HEADS
        bsrc = gsrc + 3 * D_GDN
        zsrc = bsrc + 2 * N_GDN_HEADS
        rb = 256

        def copy_rows(dst, src, count, scale=None):
            for r in range(0, count, rb):
                v = wt_ref[src + r:src + r + rb, :]
                if scale is not None:
                    v = v * scale
                w_ref[dst + r:dst + r + rb, :] = v.astype(BF16)

        copy_rows(0, 0, D_FOX, FOX_HEAD_DIM ** -0.5 * LOG2E)
        copy_rows(D_FOX, D_FOX, 2 * D_FOX)
        copy_rows(g0, gsrc, 3 * D_GDN)
        copy_rows(g0 + 3 * D_GDN, zsrc, D_GDN)
        ff = wt_ref[fsrc:fsrc + N_FOX_HEADS, :]
        ba = wt_ref[bsrc:bsrc + 2 * N_GDN_HEADS, :]
        zero8 = jnp.zeros_like(ff)
        pad_b = (SMALL_B0 - FOX_REP * N_FOX_HEADS) // 8
        pad_end = (LANES - SMALL_A0 - N_GDN_HEADS) // 8
        small = jnp.concatenate([ff] * FOX_REP + [zero8] * pad_b + [ba] + [zero8] * pad_end, axis=0)
        w_ref[s0:, :] = small.astype(BF16)

    h = _rms(x_ref[...], g_ref[...]).astype(BF16)
    tm = h.shape[0]
    first = i % tiles_per_seq == 0
    slab = 2 * d
    for sl in range(4 * D_GDN // slab):
        cols = slice(sl * slab, (sl + 1) * slab)
        if sl * slab < 3 * D_FOX:
            fox_ref[:, cols] = _dot_nt(h, w_ref[cols, :]).astype(BF16)
        g = _dot_nt(h, w_ref[g0 + sl * slab:g0 + (sl + 1) * slab, :])
        if sl * slab >= 3 * D_GDN:
            gdn_ref[:, cols] = _silu(g).astype(BF16)
            continue
        win_ref[0:8, :] = jnp.where(first, 0.0, halo_ref[:, cols])
        win_ref[8:, :] = g
        halo_ref[:, cols] = g[tm - 8:, :]
        w = cw_ref[:, cols]
        y = g * w[CONV_K - 1:CONV_K, :]
        for j in range(1, CONV_K):
            shifted = win_ref[8 - j:8 - j + tm, :]
            y = y + shifted * w[CONV_K - 1 - j:CONV_K - j, :]
        act = _silu(y)
        if sl * slab >= 2 * D_GDN:
            gdn_ref[:, cols] = act.astype(BF16)
            continue
        for hd in range(slab // d):
            t = act[:, hd * d:(hd + 1) * d]
            r = lax.rsqrt(jnp.sum(t * t, axis=-1, keepdims=True) + EPS)
            if sl * slab < D_GDN:
                r = r * (d ** -0.5)
            gdn_ref[:, sl * slab + hd * d:sl * slab + (hd + 1) * d] = (t * r).astype(BF16)
    small_ref[...] = _dot_nt(h, w_ref[s0:, :])


def _in_proj(x2, gain, w_t, conv_w, seq, tm):
    n, d = x2.shape
    const = lambda i: (0, 0)
    row = lambda i: (i, 0)
    kern = functools.partial(_in_proj_kernel, tiles_per_seq=seq // tm)
    return pl.pallas_call(
        kern,
        grid=(n // tm,),
        in_specs=[
            pl.BlockSpec((tm, d), row),
            pl.BlockSpec((1, d), const),
            pl.BlockSpec(w_t.shape, const, pipeline_mode=pl.Buffered(1)),
            pl.BlockSpec(conv_w.shape, const),
        ],
        out_specs=[
            pl.BlockSpec((tm, 3 * D_FOX), row),
            pl.BlockSpec((tm, 4 * D_GDN), row),
            pl.BlockSpec((tm, LANES), row),
        ],
        out_shape=[
            jax.ShapeDtypeStruct((n, 3 * D_FOX), BF16),
            jax.ShapeDtypeStruct((n, 4 * D_GDN), BF16),
            jax.ShapeDtypeStruct((n, LANES), F32),
        ],
        scratch_shapes=[pltpu.VMEM((3 * D_FOX + 4 * D_GDN + LANES, d), BF16),
                        pltpu.VMEM((8, 3 * D_GDN), F32),
                        pltpu.VMEM((tm + 8, 2 * GDN_HEAD_DIM), F32)],
        compiler_params=pltpu.CompilerParams(
            dimension_semantics=("arbitrary",), vmem_limit_bytes=VMEM_LIMIT),
        name="in_proj",
    )(x2, gain, w_t, conv_w)


def _fox_kernel(q_ref, k_ref, v_ref, small_ref, vec_ref, *rest, seq, tq, tk, dg, cum_blk, n_cast):
    cast_in = rest[:n_cast]
    o_ref = rest[n_cast]
    cast_out = rest[n_cast + 1:2 * n_cast + 1]
    qx_ref, kx_ref, vaug_ref = rest[2 * n_cast + 1:]
    for src, dst in zip(cast_in, cast_out):
        dst[...] = src[...].astype(dst.dtype)
    p = pl.program_id(1)
    qi = pl.program_id(2)
    lane = lax.broadcasted_iota(jnp.int32, (1, LANES), 1)
    half = FOX_HEAD_DIM
    nh = N_FOX_HEADS
    head_mask = (lane < half, lane >= half)
    ones_lane = (half, 0)
    zero = jnp.zeros((), BF16)

    @pl.when((p == 0) & (qi == 0))
    def _per_sequence():
        tril = _tril_ones(cum_blk, BF16)
        carry = jnp.zeros((1, LANES), F32)
        for blk in range(seq // cum_blk):
            rows = slice(blk * cum_blk, (blk + 1) * cum_blk)
            z = small_ref[rows, :] + vec_ref[VEC_FBIAS:VEC_FBIAS + 1, :]
            c = _cumsum_rows(tril, -_softplus(-z)) + carry
            carry = c[cum_blk - 1:cum_blk, :]
            hi, mid, lo = (t.astype(F32) for t in _split3(c * LOG2E))
            qx_ref[rows, :] = jnp.where(lane < nh, hi, jnp.where(lane < 2 * nh, mid, jnp.where(
                lane < 3 * nh, lo, jnp.where(lane < 6 * nh, 1.0, 0.0)))).astype(BF16)
            kx_ref[rows, :] = jnp.where(lane < 3 * nh, 1.0, jnp.where(lane < 4 * nh, -hi, jnp.where(
                lane < 5 * nh, -mid, jnp.where(lane < 6 * nh, -lo, 0.0)))).astype(BF16)

    @pl.when(qi == 0)
    def _per_pair():
        vp = v_ref[...]
        for e in range(2):
            ones_col = jnp.where(lane == ones_lane[e], 1.0, 0.0).astype(BF16)
            vaug_ref[e] = jnp.where(head_mask[e], vp, ones_col)

    r0 = pl.multiple_of(qi * tq, tq)
    qs = q_ref[...]
    qx = qx_ref[pl.ds(r0, tq), :]
    qaug = []
    for e in range(2):
        xmask = ((lane & (nh - 1)) == 2 * p + e) & (lane < 6 * nh)
        qaug.append(jnp.concatenate(
            [jnp.where(head_mask[e], qs, zero), jnp.where(xmask, qx, zero)], axis=1))

    def block(j0, width, row_lo, carry, masked):
        kb = jnp.concatenate([k_ref[pl.ds(j0, width), :], kx_ref[pl.ds(j0, width), :]], axis=1)
        nrows = tq - row_lo
        out = []
        for e in range(2):
            m_all, acc_all = carry[e]
            m, acc = m_all[row_lo:], acc_all[row_lo:]
            vb = vaug_ref[e, pl.ds(j0, width), :]
            s = _dot_nt(qaug[e][row_lo:], kb)
            if masked:
                rr = lax.broadcasted_iota(jnp.int32, (width, width), 0)
                cc = lax.broadcasted_iota(jnp.int32, (width, width), 1)
                top = jnp.where(cc <= rr, s[:width], -jnp.inf)
                s = top if nrows == width else jnp.concatenate([top, s[width:]], axis=0)
            m_new = jnp.maximum(m, jnp.max(s, axis=-1, keepdims=True))
            alpha = jnp.exp2(m - m_new)
            pexp = jnp.exp2(s - m_new)
            acc = alpha * acc + _dot(pexp.astype(BF16), vb)
            if row_lo:
                m_new = jnp.concatenate([m_all[:row_lo], m_new], axis=0)
                acc = jnp.concatenate([acc_all[:row_lo], acc], axis=0)
            out.append((m_new, acc))
        return tuple(out)

    init = tuple((jnp.full((tq, 1), -jnp.inf, F32), jnp.zeros((tq, LANES), F32)) for _ in range(2))
    carry = lax.fori_loop(
        0, qi * (tq // tk),
        lambda j, c: block(pl.multiple_of(j * tk, tk), tk, 0, c, False), init)
    for c in range(tq // dg):
        carry = block(pl.multiple_of(r0 + c * dg, dg), dg, c * dg, carry, True)

    outs = []
    for e in range(2):
        _, acc = carry[e]
        l = jnp.sum(jnp.where(lane == ones_lane[e], acc, 0.0), axis=-1, keepdims=True)
        outs.append(acc / l)
    o = jnp.where(head_mask[0], outs[0], outs[1])
    o2 = o * o
    ss0 = jnp.sum(jnp.where(head_mask[0], o2, 0.0), axis=-1, keepdims=True)
    ss1 = jnp.sum(jnp.where(head_mask[1], o2, 0.0), axis=-1, keepdims=True)
    ms = jnp.where(head_mask[0], ss0, ss1) * (1.0 / FOX_HEAD_DIM)
    o_ref[...] = (o * lax.rsqrt(ms + EPS) * vec_ref[VEC_FNORM:VEC_FNORM + 1, :]).astype(o_ref.dtype)


def _fox(fox_qkv, small, vecs, cast_weights, batch, seq, tq, tk, dg):
    n = batch * seq
    nq = seq // tq
    pairs = N_FOX_HEADS // 2
    steps = batch * pairs * nq
    kern = functools.partial(_fox_kernel, seq=seq, tq=tq, tk=tk, dg=dg, cum_blk=256,
                             n_cast=len(cast_weights))
    slab = lambda b, p, i: ((b * pairs + p) * nq + i, 0)
    cast_specs = []
    for w in cast_weights:
        rows = w.shape[0] // steps
        assert rows * steps == w.shape[0] and rows % 16 == 0, w.shape
        cast_specs.append(pl.BlockSpec((rows, w.shape[1]), slab))
    outs = pl.pallas_call(
        kern,
        grid=(batch, pairs, nq),
        in_specs=[
            pl.BlockSpec((tq, LANES), lambda b, p, i: (b * nq + i, p)),
            pl.BlockSpec((seq, LANES), lambda b, p, i: (b, pairs + p)),
            pl.BlockSpec((seq, LANES), lambda b, p, i: (b, 2 * pairs + p)),
            pl.BlockSpec((seq, LANES), lambda b, p, i: (b, 0)),
            pl.BlockSpec(vecs.shape, lambda b, p, i: (0, 0)),
        ] + cast_specs,
        out_specs=[pl.BlockSpec((tq, LANES), lambda b, p, i: (b * nq + i, p))] + cast_specs,
        out_shape=[jax.ShapeDtypeStruct((n, D_FOX), BF16)]
        + [jax.ShapeDtypeStruct(w.shape, BF16) for w in cast_weights],
        scratch_shapes=[
            pltpu.VMEM((seq, LANES), BF16),
            pltpu.VMEM((seq, LANES), BF16),
            pltpu.VMEM((2, seq, LANES), BF16),
        ],
        compiler_params=pltpu.CompilerParams(
            dimension_semantics=("arbitrary", "arbitrary", "arbitrary"),
            vmem_limit_bytes=VMEM_LIMIT),
        name="fox_attention",
    )(fox_qkv, fox_qkv, fox_qkv, small, vecs, *cast_weights)
    return outs[0], outs[1:]


def _merge_levels(xs, m_lows, sizes):
    c = m_lows[0].shape[0]
    row = lax.broadcasted_iota(jnp.int32, (c, c), 0)
    col = lax.broadcasted_iota(jnp.int32, (c, c), 1)
    for size in sizes:
        sh = size.bit_length() - 1
        rb = row >> sh
        below = ((rb & 1) == 1) & ((col >> sh) == rb - 1)
        xbfs = [x.astype(BF16) for x in xs]
        nxs = [_dot(jnp.where(below, m, 0.0).astype(BF16), xb).astype(BF16)
               for m, xb in zip(m_lows, xbfs)]
        xs = [x - _dot(xb, nx) for x, xb, nx in zip(xs, xbfs, nxs)]
        yield
    return xs


MERGE_SIZES = (2, 4, 8, 16, 32)


def _gdn_kernel(q_ref, k_ref, v_ref, gate_ref, small_ref, vec_ref, o_ref,
                w2_ref, rr_ref, qp_ref, op_ref, gl_ref, st_ref, *, seq, heads, group):
    c = CHUNK
    d = GDN_HEAD_DIM
    rows = group * c
    n_groups = seq // rows
    lane = lax.broadcasted_iota(jnp.int32, (1, LANES), 1)
    row = lax.broadcasted_iota(jnp.int32, (c, c), 0)
    col = lax.broadcasted_iota(jnp.int32, (c, c), 1)
    tril = _tril_ones(rows, BF16, block=c)

    def aligned(x, m):
        return x if isinstance(x, int) else pl.multiple_of(x, m)

    def prep_stages(gi):
        r0 = aligned(gi * rows, rows)
        qa = q_ref[pl.ds(r0, rows), :].astype(F32)
        ka = k_ref[pl.ds(r0, rows), :]
        va = v_ref[pl.ds(r0, rows), :].astype(F32)
        sm = small_ref[pl.ds(r0, rows), :]
        g_all = -jnp.exp(vec_ref[VEC_ALOG:VEC_ALOG + 1, :]) * _softplus(
            sm + vec_ref[VEC_DTB:VEC_DTB + 1, :])
        gc_all = _cumsum_rows(tril, g_all)
        gc_t = gc_all.T
        items = []
        for h in range(heads):
            qh = qa[:, h * d:(h + 1) * d]
            kh = ka[:, h * d:(h + 1) * d]
            vh = va[:, h * d:(h + 1) * d]
            beta_h = _sigmoid(jnp.sum(jnp.where(lane == SMALL_B0 + h, sm, 0.0),
                                      axis=-1, keepdims=True))
            gc_h = jnp.sum(jnp.where(lane == SMALL_A0 + h, gc_all, 0.0),
                           axis=-1, keepdims=True)
            gc_row_h = gc_t[SMALL_A0 + h:SMALL_A0 + h + 1, :]
            for g in range(group):
                sl = slice(g * c, (g + 1) * c)
                items.append(dict(h=h, g=g, q=qh[sl], k=kh[sl].astype(F32), kbf=kh[sl], v=vh[sl],
                                  beta=beta_h[sl], gc=gc_h[sl], gc_row=gc_row_h[:, sl]))
        for it in items:
            it["kb"] = it["k"] * it["beta"]
        kqs = [_dot_nt(jnp.concatenate([it["kb"].astype(BF16), it["q"].astype(BF16)], axis=0),
                       it["kbf"]) for it in items]
        m_lows = []
        for it, kq in zip(items, kqs):
            dmat = it["gc"] - it["gc_row"]
            decay = jnp.exp(jnp.where(col <= row, dmat, -jnp.inf))
            m_lows.append(jnp.where(col < row, kq[:c] * decay, 0.0))
            it["a"] = (kq[c:] * decay).astype(BF16)
        yield
        eye = jnp.where(row == col, 1.0, 0.0)
        pair = (row >> 1) == (col >> 1)
        xs = [eye - jnp.where(pair, m, 0.0) for m in m_lows]
        t_invs = yield from _merge_levels(xs, m_lows, MERGE_SIZES)
        wus = []
        for it, t_inv in zip(items, t_invs):
            eg = jnp.exp(it["gc"])
            it["eg"] = eg
            rhs = jnp.concatenate(
                [(it["kb"] * eg).astype(BF16), (it["v"] * it["beta"]).astype(BF16)], axis=1)
            wus.append(_dot(t_inv.astype(BF16), rhs).astype(BF16))
        yield
        kwus, awus = [], []
        for it, wu in zip(items, wus):
            gc_last = it["gc"][c - 1:c, :]
            it["gl"] = jnp.exp(gc_last)
            kd = (it["k"] * jnp.exp(gc_last - it["gc"])).astype(BF16)
            kwus.append(_dot_tn(kd, wu))
            awus.append(_dot(it["a"], wu))
        for it, kwu, awu in zip(items, kwus, awus):
            h = it["h"]
            ci = gi * group + it["g"]
            rw = aligned(ci * d, d)
            rc = aligned(ci * c, c)
            w2_ref[h, pl.ds(rw, d), :] = kwu[:, :d].astype(BF16)
            rr_ref[h, pl.ds(rw, d), :] = kwu[:, d:]
            qp_ref[h, pl.ds(rc, c), :] = (it["q"] * it["eg"] - awu[:, :d]).astype(BF16)
            op_ref[h, pl.ds(rc, c), :] = awu[:, d:]
            gl_ref[h, pl.ds(aligned(ci * 8, 8), 8), :] = jnp.broadcast_to(it["gl"], (8, d))

    def scan_chunk(ci):
        rc = aligned(ci * c, c)
        rw = aligned(ci * d, d)
        for h in range(heads):
            state = st_ref[h]
            lhs = jnp.concatenate([w2_ref[h, pl.ds(rw, d), :], qp_ref[h, pl.ds(rc, c), :]], axis=0)
            r = _dot(lhs, state.astype(BF16))
            gl = gl_ref[h, pl.ds(aligned(ci * 8, 8), 8), :][0:1, :]
            st_ref[h] = state * gl - r[:d] + rr_ref[h, pl.ds(rw, d), :]
            o = r[d:] + op_ref[h, pl.ds(rc, c), :]
            gate = gate_ref[pl.ds(rc, c), h * d:(h + 1) * d].astype(F32)
            o_ref[pl.ds(rc, c), h * d:(h + 1) * d] = (
                _rms(o, vec_ref[VEC_ONORM:VEC_ONORM + 1, :]) * gate).astype(o_ref.dtype)

    def prep_and_scan(gi, scan_group):
        pending = [] if scan_group is None else [scan_group * group + j for j in range(group)]
        for _ in prep_stages(gi):
            if pending:
                scan_chunk(pending.pop(0))
        for ci in pending:
            scan_chunk(ci)

    st_ref[...] = jnp.zeros_like(st_ref)
    prep_and_scan(0, None)

    def body(gi, _):
        prep_and_scan(gi, gi - 1)
        return 0

    lax.fori_loop(1, n_groups, body, 0)
    for j in range(group):
        scan_chunk((n_groups - 1) * group + j)


def _gdn(gdn_in, small, vecs, batch, seq, group):
    n = batch * seq
    d = GDN_HEAD_DIM
    heads = N_GDN_HEADS
    n_chunks = seq // CHUNK
    kern = functools.partial(_gdn_kernel, seq=seq, heads=heads, group=group)
    col_block = lambda part: (lambda b: (b, part))
    const = lambda b: (0, 0)
    return pl.pallas_call(
        kern,
        grid=(batch,),
        in_specs=[
            pl.BlockSpec((seq, D_GDN), col_block(0)),
            pl.BlockSpec((seq, D_GDN), col_block(1)),
            pl.BlockSpec((seq, D_GDN), col_block(2)),
            pl.BlockSpec((seq, D_GDN), col_block(3)),
            pl.BlockSpec((seq, LANES), lambda b: (b, 0)),
            pl.BlockSpec(vecs.shape, const),
        ],
        out_specs=pl.BlockSpec((seq, D_GDN), lambda b: (b, 0)),
        out_shape=jax.ShapeDtypeStruct((n, D_GDN), BF16),
        scratch_shapes=[
            pltpu.VMEM((heads, n_chunks * d, d), BF16),
            pltpu.VMEM((heads, n_chunks * d, d), F32),
            pltpu.VMEM((heads, seq, d), BF16),
            pltpu.VMEM((heads, seq, d), F32),
            pltpu.VMEM((heads, n_chunks * 8, d), F32),
            pltpu.VMEM((heads, d, d), F32),
        ],
        compiler_params=pltpu.CompilerParams(
            dimension_semantics=("arbitrary",), vmem_limit_bytes=VMEM_LIMIT),
        name="gated_deltanet",
    )(gdn_in, gdn_in, gdn_in, gdn_in, small, vecs)


def _out_mlp_kernel(fox_ref, gdn_ref, x_ref, wo_ref, pmix_ref, pre_ref,
                    wup_ref, wdn_ref, post_ref, o_ref, *, ff_blk):
    mixed = _dot(fox_ref[...], wo_ref[:D_FOX, :]) + _dot(gdn_ref[...], wo_ref[D_FOX:, :])
    x1 = x_ref[...] + _rms(mixed, pmix_ref[...])
    h = _rms(x1, pre_ref[...]).astype(BF16)
    d_ff = wup_ref.shape[1]
    y = None
    for j in range(d_ff // ff_blk):
        a = _dot(h, wup_ref[:, j * ff_blk:(j + 1) * ff_blk])
        a = jnp.square(jnp.maximum(a, 0.0)).astype(BF16)
        t = _dot(a, wdn_ref[j * ff_blk:(j + 1) * ff_blk, :])
        y = t if y is None else y + t
    o_ref[...] = x1 + _rms(y, post_ref[...])


def _out_mlp(fox_o, gdn_o, x2, wo, pmix, pre, wup, wdn, post, tm):
    n, d = x2.shape
    const = lambda i: (0, 0)
    row = lambda i: (i, 0)
    single = pl.Buffered(1)
    kern = functools.partial(_out_mlp_kernel, ff_blk=1024)
    return pl.pallas_call(
        kern,
        grid=(n // tm,),
        in_specs=[
            pl.BlockSpec((tm, fox_o.shape[1]), row),
            pl.BlockSpec((tm, gdn_o.shape[1]), row),
            pl.BlockSpec((tm, d), row),
            pl.BlockSpec(wo.shape, const, pipeline_mode=single),
            pl.BlockSpec((1, d), const),
            pl.BlockSpec((1, d), const),
            pl.BlockSpec(wup.shape, const, pipeline_mode=single),
            pl.BlockSpec(wdn.shape, const, pipeline_mode=single),
            pl.BlockSpec((1, d), const),
        ],
        out_specs=pl.BlockSpec((tm, d), row),
        out_shape=jax.ShapeDtypeStruct((n, d), F32),
        compiler_params=pltpu.CompilerParams(
            dimension_semantics=("arbitrary",), vmem_limit_bytes=VMEM_LIMIT),
        name="out_mlp",
    )(fox_o, gdn_o, x2, wo, pmix, pre, wup, wdn, post)


def _vec_row(values):
    values = values.astype(F32)
    return jnp.concatenate([values, jnp.zeros((LANES - values.shape[0],), F32)])


def kernel(x, pre_mix_norm, w_in, fox_f_bias, fox_out_norm, gdn_conv_w, gdn_a_log, gdn_dt_bias,
           gdn_out_norm, w_out, post_mix_norm, pre_mlp_norm, w_up, w_down, post_mlp_norm):
    b, s, d = x.shape
    n = b * s
    x2 = x.reshape(n, d)


    zeros_b = jnp.zeros((SMALL_A0,), F32)
    vecs = jnp.stack([
        _vec_row(jnp.tile(fox_f_bias, FOX_REP)),
        _vec_row(jnp.tile(fox_out_norm, 2)),
        _vec_row(jnp.concatenate([zeros_b, gdn_a_log.astype(F32)])),
        _vec_row(jnp.concatenate([zeros_b, gdn_dt_bias.astype(F32)])),
        _vec_row(gdn_out_norm),
        jnp.zeros((LANES,), F32), jnp.zeros((LANES,), F32), jnp.zeros((LANES,), F32)])

    fox_qkv, gdn_in, small = _in_proj(x2, pre_mix_norm.reshape(1, d).astype(F32),
                                      w_in.astype(F32).T, gdn_conv_w.astype(F32), s, tm=1024)
    fox_o, (wo, wup, wdn) = _fox(
        fox_qkv, small, vecs, [w_out.astype(F32), w_up.astype(F32), w_down.astype(F32)],
        b, s, tq=1024, tk=512, dg=512)
    gdn_o = _gdn(gdn_in, small, vecs, b, s, group=4)
    out = _out_mlp(fox_o, gdn_o, x2, wo,
                   post_mix_norm.reshape(1, d).astype(F32), pre_mlp_norm.reshape(1, d).astype(F32),
                   wup, wdn, post_mlp_norm.reshape(1, d).astype(F32), tm=512)
    return out.reshape(b, s, d)
```

```python
import functools

import jax
import jax.numpy as jnp
from jax import lax
from jax.experimental import pallas as pl
from jax.experimental.pallas import tpu as pltpu

F32 = jnp.float32
BF16 = jnp.bfloat16

EPS = 1e-6
LOG2E = 1.4426950408889634
LANES = 128
FOX_HEAD_DIM = 64
N_FOX_HEADS = 8
D_FOX = FOX_HEAD_DIM * N_FOX_HEADS
GDN_HEAD_DIM = 128
N_GDN_HEADS = 4
D_GDN = GDN_HEAD_DIM * N_GDN_HEADS
CHUNK = 64
CONV_K = 4

FOX_REP = 6
SMALL_B0 = 64
SMALL_A0 = SMALL_B0 + N_GDN_HEADS

VEC_FBIAS, VEC_FNORM, VEC_ALOG, VEC_DTB, VEC_ONORM = range(5)

VMEM_LIMIT = 56 * 1024 * 1024


def _dot(a, b):
    return jnp.dot(a, b, preferred_element_type=F32)


def _dot_nt(a, b):
    return lax.dot_general(a, b, (((1,), (1,)), ((), ())), preferred_element_type=F32)


def _dot_tn(a, b):
    return lax.dot_general(a, b, (((0,), (0,)), ((), ())), preferred_element_type=F32)


def _rms(x, w):
    return x * lax.rsqrt(jnp.mean(x * x, axis=-1, keepdims=True) + EPS) * w


def _split3(x):
    hi = x.astype(BF16)
    r1 = x - hi.astype(F32)
    mid = r1.astype(BF16)
    lo = (r1 - mid.astype(F32)).astype(BF16)
    return hi, mid, lo


def _tril_ones(n, dtype, block=None):
    r = lax.broadcasted_iota(jnp.int32, (n, n), 0)
    c = lax.broadcasted_iota(jnp.int32, (n, n), 1)
    keep = c <= r
    if block is not None:
        sh = block.bit_length() - 1
        keep = keep & ((r >> sh) == (c >> sh))
    return jnp.where(keep, 1.0, 0.0).astype(dtype)


def _cumsum_rows(tril_bf16, x):
    hi, mid, lo = _split3(x)
    return _dot(tril_bf16, hi) + _dot(tril_bf16, mid) + _dot(tril_bf16, lo)


def _sigmoid(x):
    return 0.5 + 0.5 * jnp.tanh(0.5 * x)


def _silu(x):
    h = 0.5 * x
    return h + h * jnp.tanh(h)


def _softplus(x):
    return jnp.maximum(x, 0.0) + jnp.log1p(jnp.exp(-jnp.abs(x)))


def _in_proj_kernel(x_ref, g_ref, wt_ref, cw_ref, fox_ref, gdn_ref, small_ref,
                    w_ref, halo_ref, win_ref, *, tiles_per_seq):
    i = pl.program_id(0)
    d = GDN_HEAD_DIM
    g0 = 3 * D_FOX
    s0 = g0 + 4 * D_GDN

    @pl.when(i == 0)
    def _regroup_weights():
        fsrc = 3 * D_FOX
        gsrc = fsrc + N_FOX_HEADS
        bsrc = gsrc + 3 * D_GDN
        zsrc = bsrc + 2 * N_GDN_HEADS
        rb = 256

        def copy_rows(dst, src, count, scale=None):
            for r in range(0, count, rb):
                v = wt_ref[src + r:src + r + rb, :]
                if scale is not None:
                    v = v * scale
                w_ref[dst + r:dst + r + rb, :] = v.astype(BF16)

        copy_rows(0, 0, D_FOX, FOX_HEAD_DIM ** -0.5 * LOG2E)
        copy_rows(D_FOX, D_FOX, 2 * D_FOX)
        copy_rows(g0, gsrc, 3 * D_GDN)
        copy_rows(g0 + 3 * D_GDN, zsrc, D_GDN)
        ff = wt_ref[fsrc:fsrc + N_FOX_HEADS, :]
        ba = wt_ref[bsrc:bsrc + 2 * N_GDN_HEADS, :]
        zero8 = jnp.zeros_like(ff)
        pad_b = (SMALL_B0 - FOX_REP * N_FOX_HEADS) // 8
        pad_end = (LANES - SMALL_A0 - N_GDN_HEADS) // 8
        small = jnp.concatenate([ff] * FOX_REP + [zero8] * pad_b + [ba] + [zero8] * pad_end, axis=0)
        w_ref[s0:, :] = small.astype(BF16)

    h = _rms(x_ref[...], g_ref[...]).astype(BF16)
    tm = h.shape[0]
    first = i % tiles_per_seq == 0
    slab = 2 * d
    for sl in range(4 * D_GDN // slab):
        cols = slice(sl * slab, (sl + 1) * slab)
        if sl * slab < 3 * D_FOX:
            fox_ref[:, cols] = _dot_nt(h, w_ref[cols, :]).astype(BF16)
        g = _dot_nt(h, w_ref[g0 + sl * slab:g0 + (sl + 1) * slab, :])
        if sl * slab >= 3 * D_GDN:
            gdn_ref[:, cols] = _silu(g).astype(BF16)
            continue
        win_ref[0:8, :] = jnp.where(first, 0.0, halo_ref[:, cols])
        win_ref[8:, :] = g
        halo_ref[:, cols] = g[tm - 8:, :]
        w = cw_ref[:, cols]
        y = g * w[CONV_K - 1:CONV_K, :]
        for j in range(1, CONV_K):
            shifted = win_ref[8 - j:8 - j + tm, :]
            y = y + shifted * w[CONV_K - 1 - j:CONV_K - j, :]
        act = _silu(y)
        if sl * slab >= 2 * D_GDN:
            gdn_ref[:, cols] = act.astype(BF16)
            continue
        for hd in range(slab // d):
            t = act[:, hd * d:(hd + 1) * d]
            r = lax.rsqrt(jnp.sum(t * t, axis=-1, keepdims=True) + EPS)
            if sl * slab < D_GDN:
                r = r * (d ** -0.5)
            gdn_ref[:, sl * slab + hd * d:sl * slab + (hd + 1) * d] = (t * r).astype(BF16)
    small_ref[...] = _dot_nt(h, w_ref[s0:, :])


def _in_proj(x2, gain, w_t, conv_w, seq, tm):
    n, d = x2.shape
    const = lambda i: (0, 0)
    row = lambda i: (i, 0)
    kern = functools.partial(_in_proj_kernel, tiles_per_seq=seq // tm)
    return pl.pallas_call(
        kern,
        grid=(n // tm,),
        in_specs=[
            pl.BlockSpec((tm, d), row),
            pl.BlockSpec((1, d), const),
            pl.BlockSpec(w_t.shape, const, pipeline_mode=pl.Buffered(1)),
            pl.BlockSpec(conv_w.shape, const),
        ],
        out_specs=[
            pl.BlockSpec((tm, 3 * D_FOX), row),
            pl.BlockSpec((tm, 4 * D_GDN), row),
            pl.BlockSpec((tm, LANES), row),
        ],
        out_shape=[
            jax.ShapeDtypeStruct((n, 3 * D_FOX), BF16),
            jax.ShapeDtypeStruct((n, 4 * D_GDN), BF16),
            jax.ShapeDtypeStruct((n, LANES), F32),
        ],
        scratch_shapes=[pltpu.VMEM((3 * D_FOX + 4 * D_GDN + LANES, d), BF16),
                        pltpu.VMEM((8, 3 * D_GDN), F32),
                        pltpu.VMEM((tm + 8, 2 * GDN_HEAD_DIM), F32)],
        compiler_params=pltpu.CompilerParams(
            dimension_semantics=("arbitrary",), vmem_limit_bytes=VMEM_LIMIT),
        name="in_proj",
    )(x2, gain, w_t, conv_w)


def _fox_kernel(q_ref, k_ref, v_ref, small_ref, vec_ref, *rest, seq, tq, tk, dg, cum_blk, n_cast):
    cast_in = rest[:n_cast]
    o_ref = rest[n_cast]
    cast_out = rest[n_cast + 1:2 * n_cast + 1]
    qx_ref, kx_ref, vaug_ref = rest[2 * n_cast + 1:]
    for src, dst in zip(cast_in, cast_out):
        dst[...] = src[...].astype(dst.dtype)
    p = pl.program_id(1)
    lane = lax.broadcasted_iota(jnp.int32, (1, LANES), 1)
    half = FOX_HEAD_DIM
    nh = N_FOX_HEADS
    head_mask = (lane < half, lane >= half)
    ones_lane = (half, 0)
    zero = jnp.zeros((), BF16)

    @pl.when(p == 0)
    def _per_sequence():
        tril = _tril_ones(cum_blk, BF16)
        carry = jnp.zeros((1, LANES), F32)
        for blk in range(seq // cum_blk):
            rows = slice(blk * cum_blk, (blk + 1) * cum_blk)
            z = small_ref[rows, :] + vec_ref[VEC_FBIAS:VEC_FBIAS + 1, :]
            c = _cumsum_rows(tril, -_softplus(-z)) + carry
            carry = c[cum_blk - 1:cum_blk, :]
            hi, mid, lo = (t.astype(F32) for t in _split3(c * LOG2E))
            qx_ref[rows, :] = jnp.where(lane < nh, hi, jnp.where(lane < 2 * nh, mid, jnp.where(
                lane < 3 * nh, lo, jnp.where(lane < 6 * nh, 1.0, 0.0)))).astype(BF16)
            kx_ref[rows, :] = jnp.where(lane < 3 * nh, 1.0, jnp.where(lane < 4 * nh, -hi, jnp.where(
                lane < 5 * nh, -mid, jnp.where(lane < 6 * nh, -lo, 0.0)))).astype(BF16)

    vp = v_ref[...]
    for e in range(2):
        ones_col = jnp.where(lane == ones_lane[e], 1.0, 0.0).astype(BF16)
        vaug_ref[e] = jnp.where(head_mask[e], vp, ones_col)

    def block(qaug, j0, width, row_lo, carry, masked):
        kb = jnp.concatenate([k_ref[pl.ds(j0, width), :], kx_ref[pl.ds(j0, width), :]], axis=1)
        nrows = tq - row_lo
        out = []
        for e in range(2):
            m_all, acc_all = carry[e]
            m, acc = m_all[row_lo:], acc_all[row_lo:]
            vb = vaug_ref[e, pl.ds(j0, width), :]
            s = _dot_nt(qaug[e][row_lo:], kb)
            if masked:
                rr = lax.broadcasted_iota(jnp.int32, (width, width), 0)
                cc = lax.broadcasted_iota(jnp.int32, (width, width), 1)
                top = jnp.where(cc <= rr, s[:width], -jnp.inf)
                s = top if nrows == width else jnp.concatenate([top, s[width:]], axis=0)
            m_new = jnp.maximum(m, jnp.max(s, axis=-1, keepdims=True))
            alpha = jnp.exp2(m - m_new)
            pexp = jnp.exp2(s - m_new)
            acc = alpha * acc + _dot(pexp.astype(BF16), vb)
            if row_lo:
                m_new = jnp.concatenate([m_all[:row_lo], m_new], axis=0)
                acc = jnp.concatenate([acc_all[:row_lo], acc], axis=0)
            out.append((m_new, acc))
        return tuple(out)

    for qi in range(seq // tq):
        r0 = qi * tq
        qs = q_ref[r0:r0 + tq, :]
        qx = qx_ref[r0:r0 + tq, :]
        qaug = []
        for e in range(2):
            xmask = ((lane & (nh - 1)) == 2 * p + e) & (lane < 6 * nh)
            qaug.append(jnp.concatenate(
                [jnp.where(head_mask[e], qs, zero), jnp.where(xmask, qx, zero)], axis=1))
        carry = tuple((jnp.full((tq, 1), -jnp.inf, F32), jnp.zeros((tq, LANES), F32))
                      for _ in range(2))
        for j in range(r0 // tk):
            carry = block(qaug, j * tk, tk, 0, carry, False)
        for c in range(tq // dg):
            carry = block(qaug, r0 + c * dg, dg, c * dg, carry, True)

        outs = []
        for e in range(2):
            _, acc = carry[e]
            l = jnp.sum(jnp.where(lane == ones_lane[e], acc, 0.0), axis=-1, keepdims=True)
            outs.append(acc / l)
        o = jnp.where(head_mask[0], outs[0], outs[1])
        o2 = o * o
        ss0 = jnp.sum(jnp.where(head_mask[0], o2, 0.0), axis=-1, keepdims=True)
        ss1 = jnp.sum(jnp.where(head_mask[1], o2, 0.0), axis=-1, keepdims=True)
        ms = jnp.where(head_mask[0], ss0, ss1) * (1.0 / FOX_HEAD_DIM)
        o_ref[r0:r0 + tq, :] = (o * lax.rsqrt(ms + EPS)
                                * vec_ref[VEC_FNORM:VEC_FNORM + 1, :]).astype(o_ref.dtype)


def _fox(fox_qkv, small, vecs, cast_weights, batch, seq, tq, tk, dg):
    n = batch * seq
    pairs = N_FOX_HEADS // 2
    steps = batch * pairs
    kern = functools.partial(_fox_kernel, seq=seq, tq=tq, tk=tk, dg=dg, cum_blk=256,
                             n_cast=len(cast_weights))
    slab = lambda b, p: (b * pairs + p, 0)
    cast_specs = []
    for w in cast_weights:
        rows = w.shape[0] // steps
        assert rows * steps == w.shape[0] and rows % 16 == 0, w.shape
        cast_specs.append(pl.BlockSpec((rows, w.shape[1]), slab))
    outs = pl.pallas_call(
        kern,
        grid=(batch, pairs),
        in_specs=[
            pl.BlockSpec((seq, LANES), lambda b, p: (b, p)),
            pl.BlockSpec((seq, LANES), lambda b, p: (b, pairs + p)),
            pl.BlockSpec((seq, LANES), lambda b, p: (b, 2 * pairs + p)),
            pl.BlockSpec((seq, LANES), lambda b, p: (b, 0)),
            pl.BlockSpec(vecs.shape, lambda b, p: (0, 0)),
        ] + cast_specs,
        out_specs=[pl.BlockSpec((seq, LANES), lambda b, p: (b, p))] + cast_specs,
        out_shape=[jax.ShapeDtypeStruct((n, D_FOX), BF16)]
        + [jax.ShapeDtypeStruct(w.shape, BF16) for w in cast_weights],
        scratch_shapes=[
            pltpu.VMEM((seq, LANES), BF16),
            pltpu.VMEM((seq, LANES), BF16),
            pltpu.VMEM((2, seq, LANES), BF16),
        ],
        compiler_params=pltpu.CompilerParams(
            dimension_semantics=("arbitrary", "arbitrary"),
            vmem_limit_bytes=VMEM_LIMIT),
        name="fox_attention",
    )(fox_qkv, fox_qkv, fox_qkv, small, vecs, *cast_weights)
    return outs[0], outs[1:]


def _merge_levels(xs, m_lows, sizes):
    c = m_lows[0].shape[0]
    row = lax.broadcasted_iota(jnp.int32, (c, c), 0)
    col = lax.broadcasted_iota(jnp.int32, (c, c), 1)
    for size in sizes:
        sh = size.bit_length() - 1
        rb = row >> sh
        below = ((rb & 1) == 1) & ((col >> sh) == rb - 1)
        xbfs = [x.astype(BF16) for x in xs]
        nxs = [_dot(jnp.where(below, m, 0.0).astype(BF16), xb).astype(BF16)
               for m, xb in zip(m_lows, xbfs)]
        xs = [x - _dot(xb, nx) for x, xb, nx in zip(xs, xbfs, nxs)]
        yield
    return xs


MERGE_SIZES = (2, 4, 8, 16, 32)


def _gdn_kernel(q_ref, k_ref, v_ref, gate_ref, small_ref, vec_ref, o_ref,
                w2_ref, rr_ref, qp_ref, op_ref, gl_ref, st_ref, *, seq, heads, group):
    c = CHUNK
    d = GDN_HEAD_DIM
    rows = group * c
    n_groups = seq // rows
    lane = lax.broadcasted_iota(jnp.int32, (1, LANES), 1)
    row = lax.broadcasted_iota(jnp.int32, (c, c), 0)
    col = lax.broadcasted_iota(jnp.int32, (c, c), 1)
    tril = _tril_ones(rows, BF16, block=c)

    def aligned(x, m):
        return x if isinstance(x, int) else pl.multiple_of(x, m)

    def prep_stages(gi):
        r0 = aligned(gi * rows, rows)
        qa = q_ref[pl.ds(r0, rows), :].astype(F32)
        ka = k_ref[pl.ds(r0, rows), :]
        va = v_ref[pl.ds(r0, rows), :].astype(F32)
        sm = small_ref[pl.ds(r0, rows), :]
        g_all = -jnp.exp(vec_ref[VEC_ALOG:VEC_ALOG + 1, :]) * _softplus(
            sm + vec_ref[VEC_DTB:VEC_DTB + 1, :])
        gc_all = _cumsum_rows(tril, g_all)
        items = []
        for h in range(heads):
            qh = qa[:, h * d:(h + 1) * d]
            kh = ka[:, h * d:(h + 1) * d]
            vh = va[:, h * d:(h + 1) * d]
            beta_h = _sigmoid(jnp.sum(jnp.where(lane == SMALL_B0 + h, sm, 0.0),
                                      axis=-1, keepdims=True))
            gc_h = jnp.sum(jnp.where(lane == SMALL_A0 + h, gc_all, 0.0),
                           axis=-1, keepdims=True)
            for g in range(group):
                sl = slice(g * c, (g + 1) * c)
                items.append(dict(h=h, g=g, q=qh[sl], k=kh[sl].astype(F32), kbf=kh[sl], v=vh[sl],
                                  beta=beta_h[sl], gc=gc_h[sl]))
        for it in items:
            gc = it["gc"]
            hi, mid, lo = (t.astype(F32) for t in _split3(gc))
            it["al"] = jnp.where(lane == 0, hi, jnp.where(lane == 1, mid, jnp.where(
                lane == 2, lo, jnp.where(lane < 6, 1.0, 0.0)))).astype(BF16)
            it["ar"] = jnp.where(lane < 3, 1.0, jnp.where(lane == 3, -hi, jnp.where(
                lane == 4, -mid, jnp.where(lane == 5, -lo, 0.0)))).astype(BF16)
            it["kb"] = it["k"] * it["beta"]
        dmats = [_dot_nt(it["al"], it["ar"]) for it in items]
        kks = [_dot_nt(it["kb"].astype(BF16), it["kbf"]) for it in items]
        qks = [_dot_nt(it["q"].astype(BF16), it["kbf"]) for it in items]
        m_lows = []
        for it, dmat, kk, qk in zip(items, dmats, kks, qks):
            decay = jnp.exp(jnp.where(col <= row, dmat, -jnp.inf))
            m_lows.append(jnp.where(col < row, kk * decay, 0.0))
            it["a"] = (qk * decay).astype(BF16)
        yield
        eye = jnp.where(row == col, 1.0, 0.0)
        pair = (row >> 1) == (col >> 1)
        xs = [eye - jnp.where(pair, m, 0.0) for m in m_lows]
        t_invs = yield from _merge_levels(xs, m_lows, MERGE_SIZES)
        wus = []
        for it, t_inv in zip(items, t_invs):
            eg = jnp.exp(it["gc"])
            it["eg"] = eg
            rhs = jnp.concatenate(
                [(it["kb"] * eg).astype(BF16), (it["v"] * it["beta"]).astype(BF16)], axis=1)
            wus.append(_dot(t_inv.astype(BF16), rhs).astype(BF16))
        yield
        kwus, awus = [], []
        for it, wu in zip(items, wus):
            gc_last = it["gc"][c - 1:c, :]
            it["gl"] = jnp.exp(gc_last)
            kd = (it["k"] * jnp.exp(gc_last - it["gc"])).astype(BF16)
            kwus.append(_dot_tn(kd, wu))
            awus.append(_dot(it["a"], wu))
        for it, kwu, awu in zip(items, kwus, awus):
            h = it["h"]
            ci = gi * group + it["g"]
            rw = aligned(ci * d, d)
            rc = aligned(ci * c, c)
            w2_ref[h, pl.ds(rw, d), :] = kwu[:, :d].astype(BF16)
            rr_ref[h, pl.ds(rw, d), :] = kwu[:, d:]
            qp_ref[h, pl.ds(rc, c), :] = (it["q"] * it["eg"] - awu[:, :d]).astype(BF16)
            op_ref[h, pl.ds(rc, c), :] = awu[:, d:]
            gl_ref[h, pl.ds(aligned(ci * 8, 8), 8), :] = jnp.broadcast_to(it["gl"], (8, d))

    def scan_chunk(ci):
        rc = aligned(ci * c, c)
        rw = aligned(ci * d, d)
        for h in range(heads):
            state = st_ref[h]
            lhs = jnp.concatenate([w2_ref[h, pl.ds(rw, d), :], qp_ref[h, pl.ds(rc, c), :]], axis=0)
            r = _dot(lhs, state.astype(BF16))
            gl = gl_ref[h, pl.ds(aligned(ci * 8, 8), 8), :][0:1, :]
            st_ref[h] = state * gl - r[:d] + rr_ref[h, pl.ds(rw, d), :]
            o = r[d:] + op_ref[h, pl.ds(rc, c), :]
            gate = gate_ref[pl.ds(rc, c), h * d:(h + 1) * d].astype(F32)
            o_ref[pl.ds(rc, c), h * d:(h + 1) * d] = (
                _rms(o, vec_ref[VEC_ONORM:VEC_ONORM + 1, :]) * gate).astype(o_ref.dtype)

    def prep_and_scan(gi, scan_group):
        pending = [] if scan_group is None else [scan_group * group + j for j in range(group)]
        for _ in prep_stages(gi):
            if pending:
                scan_chunk(pending.pop(0))
        for ci in pending:
            scan_chunk(ci)

    st_ref[...] = jnp.zeros_like(st_ref)
    prep_and_scan(0, None)

    def body(gi, _):
        prep_and_scan(gi, gi - 1)
        return 0

    lax.fori_loop(1, n_groups, body, 0)
    for j in range(group):
        scan_chunk((n_groups - 1) * group + j)


def _gdn(gdn_in, small, vecs, batch, seq, group):
    n = batch * seq
    d = GDN_HEAD_DIM
    heads = N_GDN_HEADS
    n_chunks = seq // CHUNK
    kern = functools.partial(_gdn_kernel, seq=seq, heads=heads, group=group)
    col_block = lambda part: (lambda b: (b, part))
    const = lambda b: (0, 0)
    return pl.pallas_call(
        kern,
        grid=(batch,),
        in_specs=[
            pl.BlockSpec((seq, D_GDN), col_block(0)),
            pl.BlockSpec((seq, D_GDN), col_block(1)),
            pl.BlockSpec((seq, D_GDN), col_block(2)),
            pl.BlockSpec((seq, D_GDN), col_block(3)),
            pl.BlockSpec((seq, LANES), lambda b: (b, 0)),
            pl.BlockSpec(vecs.shape, const),
        ],
        out_specs=pl.BlockSpec((seq, D_GDN), lambda b: (b, 0)),
        out_shape=jax.ShapeDtypeStruct((n, D_GDN), BF16),
        scratch_shapes=[
            pltpu.VMEM((heads, n_chunks * d, d), BF16),
            pltpu.VMEM((heads, n_chunks * d, d), F32),
            pltpu.VMEM((heads, seq, d), BF16),
            pltpu.VMEM((heads, seq, d), F32),
            pltpu.VMEM((heads, n_chunks * 8, d), F32),
            pltpu.VMEM((heads, d, d), F32),
        ],
        compiler_params=pltpu.CompilerParams(
            dimension_semantics=("arbitrary",), vmem_limit_bytes=VMEM_LIMIT),
        name="gated_deltanet",
    )(gdn_in, gdn_in, gdn_in, gdn_in, small, vecs)


def _out_mlp_kernel(fox_ref, gdn_ref, x_ref, wo_ref, pmix_ref, pre_ref,
                    wup_ref, wdn_ref, post_ref, o_ref, *, ff_blk):
    mixed = _dot(fox_ref[...], wo_ref[:D_FOX, :]) + _dot(gdn_ref[...], wo_ref[D_FOX:, :])
    x1 = x_ref[...] + _rms(mixed, pmix_ref[...])
    h = _rms(x1, pre_ref[...]).astype(BF16)
    d_ff = wup_ref.shape[1]
    y = None
    for j in range(d_ff // ff_blk):
        a = _dot(h, wup_ref[:, j * ff_blk:(j + 1) * ff_blk])
        a = jnp.square(jnp.maximum(a, 0.0)).astype(BF16)
        t = _dot(a, wdn_ref[j * ff_blk:(j + 1) * ff_blk, :])
        y = t if y is None else y + t
    o_ref[...] = x1 + _rms(y, post_ref[...])


def _out_mlp(fox_o, gdn_o, x2, wo, pmix, pre, wup, wdn, post, tm):
    n, d = x2.shape
    const = lambda i: (0, 0)
    row = lambda i: (i, 0)
    single = pl.Buffered(1)
    kern = functools.partial(_out_mlp_kernel, ff_blk=1024)
    return pl.pallas_call(
        kern,
        grid=(n // tm,),
        in_specs=[
            pl.BlockSpec((tm, fox_o.shape[1]), row),
            pl.BlockSpec((tm, gdn_o.shape[1]), row),
            pl.BlockSpec((tm, d), row),
            pl.BlockSpec(wo.shape, const, pipeline_mode=single),
            pl.BlockSpec((1, d), const),
            pl.BlockSpec((1, d), const),
            pl.BlockSpec(wup.shape, const, pipeline_mode=single),
            pl.BlockSpec(wdn.shape, const, pipeline_mode=single),
            pl.BlockSpec((1, d), const),
        ],
        out_specs=pl.BlockSpec((tm, d), row),
        out_shape=jax.ShapeDtypeStruct((n, d), F32),
        compiler_params=pltpu.CompilerParams(
            dimension_semantics=("arbitrary",), vmem_limit_bytes=VMEM_LIMIT),
        name="out_mlp",
    )(fox_o, gdn_o, x2, wo, pmix, pre, wup, wdn, post)


def _vec_row(values):
    values = values.astype(F32)
    return jnp.concatenate([values, jnp.zeros((LANES - values.shape[0],), F32)])


def kernel(x, pre_mix_norm, w_in, fox_f_bias, fox_out_norm, gdn_conv_w, gdn_a_log, gdn_dt_bias,
           gdn_out_norm, w_out, post_mix_norm, pre_mlp_norm, w_up, w_down, post_mlp_norm):
    b, s, d = x.shape
    n = b * s
    x2 = x.reshape(n, d)


    zeros_b = jnp.zeros((SMALL_A0,), F32)
    vecs = jnp.stack([
        _vec_row(jnp.tile(fox_f_bias, FOX_REP)),
        _vec_row(jnp.tile(fox_out_norm, 2)),
        _vec_row(jnp.concatenate([zeros_b, gdn_a_log.astype(F32)])),
        _vec_row(jnp.concatenate([zeros_b, gdn_dt_bias.astype(F32)])),
        _vec_row(gdn_out_norm),
        jnp.zeros((LANES,), F32), jnp.zeros((LANES,), F32), jnp.zeros((LANES,), F32)])

    fox_qkv, gdn_in, small = _in_proj(x2, pre_mix_norm.reshape(1, d).astype(F32),
                                      w_in.astype(F32).T, gdn_conv_w.astype(F32), s, tm=1024)
    fox_o, (wo, wup, wdn) = _fox(
        fox_qkv, small, vecs, [w_out.astype(F32), w_up.astype(F32), w_down.astype(F32)],
        b, s, tq=1024, tk=512, dg=512)
    gdn_o = _gdn(gdn_in, small, vecs, b, s, group=4)
    out = _out_mlp(fox_o, gdn_o, x2, wo,
                   post_mix_norm.reshape(1, d).astype(F32), pre_mlp_norm.reshape(1, d).astype(F32),
                   wup, wdn, post_mlp_norm.reshape(1, d).astype(F32), tm=512)
    return out.reshape(b, s, d)
```

```python
import functools

import jax
import jax.numpy as jnp
from jax import lax
from jax.experimental import pallas as pl
from jax.experimental.pallas import tpu as pltpu

F32 = jnp.float32
BF16 = jnp.bfloat16

EPS = 1e-6
LOG2E = 1.4426950408889634
LANES = 128
FOX_HEAD_DIM = 64
N_FOX_HEADS = 8
D_FOX = FOX_HEAD_DIM * N_FOX_HEADS
GDN_HEAD_DIM = 128
N_GDN_HEADS = 4
D_GDN = GDN_HEAD_DIM * N_GDN_HEADS
CHUNK = 64
CONV_K = 4

FOX_REP = 6
SMALL_B0 = 64
SMALL_A0 = SMALL_B0 + N_GDN_HEADS

VEC_FBIAS, VEC_FNORM, VEC_ALOG, VEC_DTB, VEC_ONORM = range(5)

VMEM_LIMIT = 56 * 1024 * 1024
VT_ROWS = FOX_HEAD_DIM + 16


def _dot(a, b):
    return jnp.dot(a, b, preferred_element_type=F32)


def _dot_nt(a, b):
    return lax.dot_general(a, b, (((1,), (1,)), ((), ())), preferred_element_type=F32)


def _dot_tn(a, b):
    return lax.dot_general(a, b, (((0,), (0,)), ((), ())), preferred_element_type=F32)


def _rms(x, w):
    return x * lax.rsqrt(jnp.mean(x * x, axis=-1, keepdims=True) + EPS) * w


def _split3(x):
    hi = x.astype(BF16)
    r1 = x - hi.astype(F32)
    mid = r1.astype(BF16)
    lo = (r1 - mid.astype(F32)).astype(BF16)
    return hi, mid, lo


def _tril_ones(n, dtype, block=None):
    r = lax.broadcasted_iota(jnp.int32, (n, n), 0)
    c = lax.broadcasted_iota(jnp.int32, (n, n), 1)
    keep = c <= r
    if block is not None:
        sh = block.bit_length() - 1
        keep = keep & ((r >> sh) == (c >> sh))
    return jnp.where(keep, 1.0, 0.0).astype(dtype)


def _cumsum_rows(tril_bf16, x):
    hi, mid, lo = _split3(x)
    return _dot(tril_bf16, hi) + _dot(tril_bf16, mid) + _dot(tril_bf16, lo)


def _sigmoid(x):
    return 0.5 + 0.5 * jnp.tanh(0.5 * x)


def _silu(x):
    h = 0.5 * x
    return h + h * jnp.tanh(h)


def _softplus(x):
    return jnp.maximum(x, 0.0) + jnp.log1p(jnp.exp(-jnp.abs(x)))


def _in_proj_kernel(x_ref, g_ref, wt_ref, cw_ref, fox_ref, gdn_ref, small_ref,
                    w_ref, halo_ref, win_ref, *, tiles_per_seq):
    i = pl.program_id(0)
    d = GDN_HEAD_DIM
    g0 = 3 * D_FOX
    s0 = g0 + 4 * D_GDN

    @pl.when(i == 0)
    def _regroup_weights():
        fsrc = 3 * D_FOX
        gsrc = fsrc + N_FOX_HEADS
        bsrc = gsrc + 3 * D_GDN
        zsrc = bsrc + 2 * N_GDN_HEADS
        rb = 256

        def copy_rows(dst, src, count, scale=None):
            for r in range(0, count, rb):
                v = wt_ref[src + r:src + r + rb, :]
                if scale is not None:
                    v = v * scale
                w_ref[dst + r:dst + r + rb, :] = v.astype(BF16)

        copy_rows(0, 0, D_FOX, FOX_HEAD_DIM ** -0.5 * LOG2E)
        copy_rows(D_FOX, D_FOX, 2 * D_FOX)
        copy_rows(g0, gsrc, 3 * D_GDN)
        copy_rows(g0 + 3 * D_GDN, zsrc, D_GDN)
        ff = wt_ref[fsrc:fsrc + N_FOX_HEADS, :]
        ba = wt_ref[bsrc:bsrc + 2 * N_GDN_HEADS, :]
        zero8 = jnp.zeros_like(ff)
        pad_b = (SMALL_B0 - FOX_REP * N_FOX_HEADS) // 8
        pad_end = (LANES - SMALL_A0 - N_GDN_HEADS) // 8
        small = jnp.concatenate([ff] * FOX_REP + [zero8] * pad_b + [ba] + [zero8] * pad_end, axis=0)
        w_ref[s0:, :] = small.astype(BF16)

    h = _rms(x_ref[...], g_ref[...]).astype(BF16)
    tm = h.shape[0]
    first = i % tiles_per_seq == 0
    slab = 2 * d
    for sl in range(4 * D_GDN // slab):
        cols = slice(sl * slab, (sl + 1) * slab)
        if sl * slab < 3 * D_FOX:
            fox_ref[:, cols] = _dot_nt(h, w_ref[cols, :]).astype(BF16)
        g = _dot_nt(h, w_ref[g0 + sl * slab:g0 + (sl + 1) * slab, :])
        if sl * slab >= 3 * D_GDN:
            gdn_ref[:, cols] = _silu(g).astype(BF16)
            continue
        win_ref[0:8, :] = jnp.where(first, 0.0, halo_ref[:, cols])
        win_ref[8:, :] = g
        halo_ref[:, cols] = g[tm - 8:, :]
        w = cw_ref[:, cols]
        y = g * w[CONV_K - 1:CONV_K, :]
        for j in range(1, CONV_K):
            shifted = win_ref[8 - j:8 - j + tm, :]
            y = y + shifted * w[CONV_K - 1 - j:CONV_K - j, :]
        act = _silu(y)
        if sl * slab >= 2 * D_GDN:
            gdn_ref[:, cols] = act.astype(BF16)
            continue
        for hd in range(slab // d):
            t = act[:, hd * d:(hd + 1) * d]
            r = lax.rsqrt(jnp.sum(t * t, axis=-1, keepdims=True) + EPS)
            if sl * slab < D_GDN:
                r = r * (d ** -0.5)
            gdn_ref[:, sl * slab + hd * d:sl * slab + (hd + 1) * d] = (t * r).astype(BF16)
    small_ref[...] = _dot_nt(h, w_ref[s0:, :])


def _in_proj(x2, gain, w_t, conv_w, seq, tm):
    n, d = x2.shape
    const = lambda i: (0, 0)
    row = lambda i: (i, 0)
    kern = functools.partial(_in_proj_kernel, tiles_per_seq=seq // tm)
    return pl.pallas_call(
        kern,
        grid=(n // tm,),
        in_specs=[
            pl.BlockSpec((tm, d), row),
            pl.BlockSpec((1, d), const),
            pl.BlockSpec(w_t.shape, const, pipeline_mode=pl.Buffered(1)),
            pl.BlockSpec(conv_w.shape, const),
        ],
        out_specs=[
            pl.BlockSpec((tm, 3 * D_FOX), row),
            pl.BlockSpec((tm, 4 * D_GDN), row),
            pl.BlockSpec((tm, LANES), row),
        ],
        out_shape=[
            jax.ShapeDtypeStruct((n, 3 * D_FOX), BF16),
            jax.ShapeDtypeStruct((n, 4 * D_GDN), BF16),
            jax.ShapeDtypeStruct((n, LANES), F32),
        ],
        scratch_shapes=[pltpu.VMEM((3 * D_FOX + 4 * D_GDN + LANES, d), BF16),
                        pltpu.VMEM((8, 3 * D_GDN), F32),
                        pltpu.VMEM((tm + 8, 2 * GDN_HEAD_DIM), F32)],
        compiler_params=pltpu.CompilerParams(
            dimension_semantics=("arbitrary",), vmem_limit_bytes=VMEM_LIMIT),
        name="in_proj",
    )(x2, gain, w_t, conv_w)


def _fox_kernel(q_ref, k_ref, v_ref, small_ref, vec_ref, *rest, seq, tq, tk, dg, cum_blk, n_cast):
    cast_in = rest[:n_cast]
    o_ref = rest[n_cast]
    cast_out = rest[n_cast + 1:2 * n_cast + 1]
    qx_ref, kx_ref, vaug_ref = rest[2 * n_cast + 1:]
    for src, dst in zip(cast_in, cast_out):
        dst[...] = src[...].astype(dst.dtype)
    p = pl.program_id(1)
    lane = lax.broadcasted_iota(jnp.int32, (1, LANES), 1)
    half = FOX_HEAD_DIM
    nh = N_FOX_HEADS
    head_mask = (lane < half, lane >= half)
    ones_lane = (half, 0)
    zero = jnp.zeros((), BF16)

    @pl.when(p == 0)
    def _per_sequence():
        tril = _tril_ones(cum_blk, BF16)
        carry = jnp.zeros((1, LANES), F32)
        for blk in range(seq // cum_blk):
            rows = slice(blk * cum_blk, (blk + 1) * cum_blk)
            z = small_ref[rows, :] + vec_ref[VEC_FBIAS:VEC_FBIAS + 1, :]
            c = _cumsum_rows(tril, -_softplus(-z)) + carry
            carry = c[cum_blk - 1:cum_blk, :]
            hi, mid, lo = (t.astype(F32) for t in _split3(c * LOG2E))
            qx_ref[rows, :] = jnp.where(lane < nh, hi, jnp.where(lane < 2 * nh, mid, jnp.where(
                lane < 3 * nh, lo, jnp.where(lane < 6 * nh, 1.0, 0.0)))).astype(BF16)
            kx_ref[rows, :] = jnp.where(lane < 3 * nh, 1.0, jnp.where(lane < 4 * nh, -hi, jnp.where(
                lane < 5 * nh, -mid, jnp.where(lane < 6 * nh, -lo, 0.0)))).astype(BF16)

    tb = 256
    ones_blk = jnp.where(lax.broadcasted_iota(jnp.int32, (VT_ROWS - half, tb), 0) == 0, 1.0, 0.0)
    for r in range(0, seq, tb):
        vt = v_ref[r:r + tb, :].astype(F32).T
        for e in range(2):
            vaug_ref[e, :, r:r + tb] = jnp.concatenate(
                [vt[e * half:(e + 1) * half], ones_blk], axis=0).astype(BF16)

    def block(qaug, j0, width, q_lo, carry, masked):
        kb = jnp.concatenate([k_ref[pl.ds(j0, width), :], kx_ref[pl.ds(j0, width), :]], axis=1)
        nq = tq - q_lo
        out = []
        for e in range(2):
            m_all, acc_all = carry[e]
            m, acc = m_all[:, q_lo:], acc_all[:, q_lo:]
            vt = vaug_ref[e, :, pl.ds(j0, width)]
            s = _dot_nt(kb, qaug[e][q_lo:])
            if masked:
                kk = lax.broadcasted_iota(jnp.int32, (width, width), 0)
                qq = lax.broadcasted_iota(jnp.int32, (width, width), 1)
                left = jnp.where(kk <= qq, s[:, :width], -jnp.inf)
                s = left if nq == width else jnp.concatenate([left, s[:, width:]], axis=1)
            m_new = jnp.maximum(m, jnp.max(s, axis=0, keepdims=True))
            alpha = jnp.exp2(m - m_new)
            pexp = jnp.exp2(s - m_new)
            acc = alpha * acc + _dot(vt, pexp.astype(BF16))
            if q_lo:
                m_new = jnp.concatenate([m_all[:, :q_lo], m_new], axis=1)
                acc = jnp.concatenate([acc_all[:, :q_lo], acc], axis=1)
            out.append((m_new, acc))
        return tuple(out)

    for qi in range(seq // tq):
        r0 = qi * tq
        qs = q_ref[r0:r0 + tq, :]
        qx = qx_ref[r0:r0 + tq, :]
        qaug = []
        for e in range(2):
            xmask = ((lane & (nh - 1)) == 2 * p + e) & (lane < 6 * nh)
            qaug.append(jnp.concatenate(
                [jnp.where(head_mask[e], qs, zero), jnp.where(xmask, qx, zero)], axis=1))
        carry = tuple((jnp.full((1, tq), -jnp.inf, F32), jnp.zeros((VT_ROWS, tq), F32))
                      for _ in range(2))
        for j in range(r0 // tk):
            carry = block(qaug, j * tk, tk, 0, carry, False)
        for c in range(tq // dg):
            carry = block(qaug, r0 + c * dg, dg, c * dg, carry, True)

        outs = []
        for e in range(2):
            _, acc = carry[e]
            l = acc[half:half + 1, :]
            outs.append(acc[:half] / l)
        o = jnp.concatenate(outs, axis=0).T
        o2 = o * o
        ss0 = jnp.sum(jnp.where(head_mask[0], o2, 0.0), axis=-1, keepdims=True)
        ss1 = jnp.sum(jnp.where(head_mask[1], o2, 0.0), axis=-1, keepdims=True)
        ms = jnp.where(head_mask[0], ss0, ss1) * (1.0 / FOX_HEAD_DIM)
        o_ref[r0:r0 + tq, :] = (o * lax.rsqrt(ms + EPS)
                                * vec_ref[VEC_FNORM:VEC_FNORM + 1, :]).astype(o_ref.dtype)


def _fox(fox_qkv, small, vecs, cast_weights, batch, seq, tq, tk, dg):
    n = batch * seq
    pairs = N_FOX_HEADS // 2
    steps = batch * pairs
    kern = functools.partial(_fox_kernel, seq=seq, tq=tq, tk=tk, dg=dg, cum_blk=256,
                             n_cast=len(cast_weights))
    slab = lambda b, p: (b * pairs + p, 0)
    cast_specs = []
    for w in cast_weights:
        rows = w.shape[0] // steps
        assert rows * steps == w.shape[0] and rows % 16 == 0, w.shape
        cast_specs.append(pl.BlockSpec((rows, w.shape[1]), slab))
    outs = pl.pallas_call(
        kern,
        grid=(batch, pairs),
        in_specs=[
            pl.BlockSpec((seq, LANES), lambda b, p: (b, p)),
            pl.BlockSpec((seq, LANES), lambda b, p: (b, pairs + p)),
            pl.BlockSpec((seq, LANES), lambda b, p: (b, 2 * pairs + p)),
            pl.BlockSpec((seq, LANES), lambda b, p: (b, 0)),
            pl.BlockSpec(vecs.shape, lambda b, p: (0, 0)),
        ] + cast_specs,
        out_specs=[pl.BlockSpec((seq, LANES), lambda b, p: (b, p))] + cast_specs,
        out_shape=[jax.ShapeDtypeStruct((n, D_FOX), BF16)]
        + [jax.ShapeDtypeStruct(w.shape, BF16) for w in cast_weights],
        scratch_shapes=[
            pltpu.VMEM((seq, LANES), BF16),
            pltpu.VMEM((seq, LANES), BF16),
            pltpu.VMEM((2, VT_ROWS, seq), BF16),
        ],
        compiler_params=pltpu.CompilerParams(
            dimension_semantics=("arbitrary", "arbitrary"),
            vmem_limit_bytes=VMEM_LIMIT),
        name="fox_attention",
    )(fox_qkv, fox_qkv, fox_qkv, small, vecs, *cast_weights)
    return outs[0], outs[1:]


def _merge_levels(xs, m_lows, sizes):
    c = m_lows[0].shape[0]
    row = lax.broadcasted_iota(jnp.int32, (c, c), 0)
    col = lax.broadcasted_iota(jnp.int32, (c, c), 1)
    for size in sizes:
        sh = size.bit_length() - 1
        rb = row >> sh
        below = ((rb & 1) == 1) & ((col >> sh) == rb - 1)
        xbfs = [x.astype(BF16) for x in xs]
        nxs = [_dot(jnp.where(below, m, 0.0).astype(BF16), xb).astype(BF16)
               for m, xb in zip(m_lows, xbfs)]
        xs = [x - _dot(xb, nx) for x, xb, nx in zip(xs, xbfs, nxs)]
        yield
    return xs


MERGE_SIZES = (2, 4, 8, 16, 32)


def _gdn_kernel(q_ref, k_ref, v_ref, gate_ref, small_ref, vec_ref, o_ref,
                w2_ref, rr_ref, qp_ref, op_ref, gl_ref, st_ref, *, seq, heads, group):
    c = CHUNK
    d = GDN_HEAD_DIM
    rows = group * c
    n_groups = seq // rows
    lane = lax.broadcasted_iota(jnp.int32, (1, LANES), 1)
    row = lax.broadcasted_iota(jnp.int32, (c, c), 0)
    col = lax.broadcasted_iota(jnp.int32, (c, c), 1)
    tril = _tril_ones(rows, BF16, block=c)

    def aligned(x, m):
        return x if isinstance(x, int) else pl.multiple_of(x, m)

    def prep_stages(gi):
        r0 = aligned(gi * rows, rows)
        qa = q_ref[pl.ds(r0, rows), :].astype(F32)
        ka = k_ref[pl.ds(r0, rows), :]
        va = v_ref[pl.ds(r0, rows), :].astype(F32)
        sm = small_ref[pl.ds(r0, rows), :]
        g_all = -jnp.exp(vec_ref[VEC_ALOG:VEC_ALOG + 1, :]) * _softplus(
            sm + vec_ref[VEC_DTB:VEC_DTB + 1, :])
        gc_all = _cumsum_rows(tril, g_all)
        items = []
        for h in range(heads):
            qh = qa[:, h * d:(h + 1) * d]
            kh = ka[:, h * d:(h + 1) * d]
            vh = va[:, h * d:(h + 1) * d]
            beta_h = _sigmoid(jnp.sum(jnp.where(lane == SMALL_B0 + h, sm, 0.0),
                                      axis=-1, keepdims=True))
            gc_h = jnp.sum(jnp.where(lane == SMALL_A0 + h, gc_all, 0.0),
                           axis=-1, keepdims=True)
            for g in range(group):
                sl = slice(g * c, (g + 1) * c)
                items.append(dict(h=h, g=g, q=qh[sl], k=kh[sl].astype(F32), kbf=kh[sl], v=vh[sl],
                                  beta=beta_h[sl], gc=gc_h[sl]))
        for it in items:
            gc = it["gc"]
            hi, mid, lo = (t.astype(F32) for t in _split3(gc))
            it["al"] = jnp.where(lane == 0, hi, jnp.where(lane == 1, mid, jnp.where(
                lane == 2, lo, jnp.where(lane < 6, 1.0, 0.0)))).astype(BF16)
            it["ar"] = jnp.where(lane < 3, 1.0, jnp.where(lane == 3, -hi, jnp.where(
                lane == 4, -mid, jnp.where(lane == 5, -lo, 0.0)))).astype(BF16)
            it["kb"] = it["k"] * it["beta"]
        dmats = [_dot_nt(it["al"], it["ar"]) for it in items]
        kks = [_dot_nt(it["kb"].astype(BF16), it["kbf"]) for it in items]
        qks = [_dot_nt(it["q"].astype(BF16), it["kbf"]) for it in items]
        m_lows = []
        for it, dmat, kk, qk in zip(items, dmats, kks, qks):
            decay = jnp.exp(jnp.where(col <= row, dmat, -jnp.inf))
            m_lows.append(jnp.where(col < row, kk * decay, 0.0))
            it["a"] = (qk * decay).astype(BF16)
        yield
        eye = jnp.where(row == col, 1.0, 0.0)
        pair = (row >> 1) == (col >> 1)
        xs = [eye - jnp.where(pair, m, 0.0) for m in m_lows]
        t_invs = yield from _merge_levels(xs, m_lows, MERGE_SIZES)
        wus = []
        for it, t_inv in zip(items, t_invs):
            eg = jnp.exp(it["gc"])
            it["eg"] = eg
            rhs = jnp.concatenate(
                [(it["kb"] * eg).astype(BF16), (it["v"] * it["beta"]).astype(BF16)], axis=1)
            wus.append(_dot(t_inv.astype(BF16), rhs).astype(BF16))
        yield
        kwus, awus = [], []
        for it, wu in zip(items, wus):
            gc_last = it["gc"][c - 1:c, :]
            it["gl"] = jnp.exp(gc_last)
            kd = (it["k"] * jnp.exp(gc_last - it["gc"])).astype(BF16)
            kwus.append(_dot_tn(kd, wu))
            awus.append(_dot(it["a"], wu))
        for it, kwu, awu in zip(items, kwus, awus):
            h = it["h"]
            ci = gi * group + it["g"]
            rw = aligned(ci * d, d)
            rc = aligned(ci * c, c)
            w2_ref[h, pl.ds(rw, d), :] = kwu[:, :d].astype(BF16)
            rr_ref[h, pl.ds(rw, d), :] = kwu[:, d:]
            qp_ref[h, pl.ds(rc, c), :] = (it["q"] * it["eg"] - awu[:, :d]).astype(BF16)
            op_ref[h, pl.ds(rc, c), :] = awu[:, d:]
            gl_ref[h, pl.ds(aligned(ci * 8, 8), 8), :] = jnp.broadcast_to(it["gl"], (8, d))

    def scan_chunk(ci):
        rc = aligned(ci * c, c)
        rw = aligned(ci * d, d)
        for h in range(heads):
            state = st_ref[h]
            lhs = jnp.concatenate([w2_ref[h, pl.ds(rw, d), :], qp_ref[h, pl.ds(rc, c), :]], axis=0)
            r = _dot(lhs, state.astype(BF16))
            gl = gl_ref[h, pl.ds(aligned(ci * 8, 8), 8), :][0:1, :]
            st_ref[h] = state * gl - r[:d] + rr_ref[h, pl.ds(rw, d), :]
            o = r[d:] + op_ref[h, pl.ds(rc, c), :]
            gate = gate_ref[pl.ds(rc, c), h * d:(h + 1) * d].astype(F32)
            o_ref[pl.ds(rc, c), h * d:(h + 1) * d] = (
                _rms(o, vec_ref[VEC_ONORM:VEC_ONORM + 1, :]) * gate).astype(o_ref.dtype)

    def prep_and_scan(gi, scan_group):
        pending = [] if scan_group is None else [scan_group * group + j for j in range(group)]
        for _ in prep_stages(gi):
            if pending:
                scan_chunk(pending.pop(0))
        for ci in pending:
            scan_chunk(ci)

    st_ref[...] = jnp.zeros_like(st_ref)
    prep_and_scan(0, None)

    def body(gi, _):
        prep_and_scan(gi, gi - 1)
        return 0

    lax.fori_loop(1, n_groups, body, 0)
    for j in range(group):
        scan_chunk((n_groups - 1) * group + j)


def _gdn(gdn_in, small, vecs, batch, seq, group):
    n = batch * seq
    d = GDN_HEAD_DIM
    heads = N_GDN_HEADS
    n_chunks = seq // CHUNK
    kern = functools.partial(_gdn_kernel, seq=seq, heads=heads, group=group)
    col_block = lambda part: (lambda b: (b, part))
    const = lambda b: (0, 0)
    return pl.pallas_call(
        kern,
        grid=(batch,),
        in_specs=[
            pl.BlockSpec((seq, D_GDN), col_block(0)),
            pl.BlockSpec((seq, D_GDN), col_block(1)),
            pl.BlockSpec((seq, D_GDN), col_block(2)),
            pl.BlockSpec((seq, D_GDN), col_block(3)),
            pl.BlockSpec((seq, LANES), lambda b: (b, 0)),
            pl.BlockSpec(vecs.shape, const),
        ],
        out_specs=pl.BlockSpec((seq, D_GDN), lambda b: (b, 0)),
        out_shape=jax.ShapeDtypeStruct((n, D_GDN), BF16),
        scratch_shapes=[
            pltpu.VMEM((heads, n_chunks * d, d), BF16),
            pltpu.VMEM((heads, n_chunks * d, d), F32),
            pltpu.VMEM((heads, seq, d), BF16),
            pltpu.VMEM((heads, seq, d), F32),
            pltpu.VMEM((heads, n_chunks * 8, d), F32),
            pltpu.VMEM((heads, d, d), F32),
        ],
        compiler_params=pltpu.CompilerParams(
            dimension_semantics=("arbitrary",), vmem_limit_bytes=VMEM_LIMIT),
        name="gated_deltanet",
    )(gdn_in, gdn_in, gdn_in, gdn_in, small, vecs)


def _out_mlp_kernel(fox_ref, gdn_ref, x_ref, wo_ref, pmix_ref, pre_ref,
                    wup_ref, wdn_ref, post_ref, o_ref, *, ff_blk):
    mixed = _dot(fox_ref[...], wo_ref[:D_FOX, :]) + _dot(gdn_ref[...], wo_ref[D_FOX:, :])
    x1 = x_ref[...] + _rms(mixed, pmix_ref[...])
    h = _rms(x1, pre_ref[...]).astype(BF16)
    d_ff = wup_ref.shape[1]
    y = None
    for j in range(d_ff // ff_blk):
        a = _dot(h, wup_ref[:, j * ff_blk:(j + 1) * ff_blk])
        a = jnp.square(jnp.maximum(a, 0.0)).astype(BF16)
        t = _dot(a, wdn_ref[j * ff_blk:(j + 1) * ff_blk, :])
        y = t if y is None else y + t
    o_ref[...] = x1 + _rms(y, post_ref[...])


def _out_mlp(fox_o, gdn_o, x2, wo, pmix, pre, wup, wdn, post, tm):
    n, d = x2.shape
    const = lambda i: (0, 0)
    row = lambda i: (i, 0)
    single = pl.Buffered(1)
    kern = functools.partial(_out_mlp_kernel, ff_blk=1024)
    return pl.pallas_call(
        kern,
        grid=(n // tm,),
        in_specs=[
            pl.BlockSpec((tm, fox_o.shape[1]), row),
            pl.BlockSpec((tm, gdn_o.shape[1]), row),
            pl.BlockSpec((tm, d), row),
            pl.BlockSpec(wo.shape, const, pipeline_mode=single),
            pl.BlockSpec((1, d), const),
            pl.BlockSpec((1, d), const),
            pl.BlockSpec(wup.shape, const, pipeline_mode=single),
            pl.BlockSpec(wdn.shape, const, pipeline_mode=single),
            pl.BlockSpec((1, d), const),
        ],
        out_specs=pl.BlockSpec((tm, d), row),
        out_shape=jax.ShapeDtypeStruct((n, d), F32),
        compiler_params=pltpu.CompilerParams(
            dimension_semantics=("arbitrary",), vmem_limit_bytes=VMEM_LIMIT),
        name="out_mlp",
    )(fox_o, gdn_o, x2, wo, pmix, pre, wup, wdn, post)


def _vec_row(values):
    values = values.astype(F32)
    return jnp.concatenate([values, jnp.zeros((LANES - values.shape[0],), F32)])


def kernel(x, pre_mix_norm, w_in, fox_f_bias, fox_out_norm, gdn_conv_w, gdn_a_log, gdn_dt_bias,
           gdn_out_norm, w_out, post_mix_norm, pre_mlp_norm, w_up, w_down, post_mlp_norm):
    b, s, d = x.shape
    n = b * s
    x2 = x.reshape(n, d)


    zeros_b = jnp.zeros((SMALL_A0,), F32)
    vecs = jnp.stack([
        _vec_row(jnp.tile(fox_f_bias, FOX_REP)),
        _vec_row(jnp.tile(fox_out_norm, 2)),
        _vec_row(jnp.concatenate([zeros_b, gdn_a_log.astype(F32)])),
        _vec_row(jnp.concatenate([zeros_b, gdn_dt_bias.astype(F32)])),
        _vec_row(gdn_out_norm),
        jnp.zeros((LANES,), F32), jnp.zeros((LANES,), F32), jnp.zeros((LANES,), F32)])

    fox_qkv, gdn_in, small = _in_proj(x2, pre_mix_norm.reshape(1, d).astype(F32),
                                      w_in.astype(F32).T, gdn_conv_w.astype(F32), s, tm=1024)
    fox_o, (wo, wup, wdn) = _fox(
        fox_qkv, small, vecs, [w_out.astype(F32), w_up.astype(F32), w_down.astype(F32)],
        b, s, tq=1024, tk=512, dg=512)
    gdn_o = _gdn(gdn_in, small, vecs, b, s, group=4)
    out = _out_mlp(fox_o, gdn_o, x2, wo,
                   post_mix_norm.reshape(1, d).astype(F32), pre_mlp_norm.reshape(1, d).astype(F32),
                   wup, wdn, post_mlp_norm.reshape(1, d).astype(F32), tm=512)
    return out.reshape(b, s, d)
```

```python
import functools

import jax
import jax.numpy as jnp
from jax import lax
from jax.experimental import pallas as pl
from jax.experimental.pallas import tpu as pltpu

F32 = jnp.float32
BF16 = jnp.bfloat16

EPS = 1e-6
LOG2E = 1.4426950408889634
LANES = 128
FOX_HEAD_DIM = 64
N_FOX_HEADS = 8
D_FOX = FOX_HEAD_DIM * N_FOX_HEADS
GDN_HEAD_DIM = 128
N_GDN_HEADS = 4
D_GDN = GDN_HEAD_DIM * N_GDN_HEADS
CHUNK = 128
CONV_K = 4

FOX_REP = 6
SMALL_B0 = 64
SMALL_A0 = SMALL_B0 + N_GDN_HEADS

VEC_FBIAS, VEC_FNORM, VEC_ALOG, VEC_DTB, VEC_ONORM = range(5)

VMEM_LIMIT = 56 * 1024 * 1024


def _dot(a, b):
    return jnp.dot(a, b, preferred_element_type=F32)


def _dot_nt(a, b):
    return lax.dot_general(a, b, (((1,), (1,)), ((), ())), preferred_element_type=F32)


def _dot_tn(a, b):
    return lax.dot_general(a, b, (((0,), (0,)), ((), ())), preferred_element_type=F32)


def _rms(x, w):
    return x * lax.rsqrt(jnp.mean(x * x, axis=-1, keepdims=True) + EPS) * w


def _split3(x):
    hi = x.astype(BF16)
    r1 = x - hi.astype(F32)
    mid = r1.astype(BF16)
    lo = (r1 - mid.astype(F32)).astype(BF16)
    return hi, mid, lo


def _tril_ones(n, dtype, block=None):
    r = lax.broadcasted_iota(jnp.int32, (n, n), 0)
    c = lax.broadcasted_iota(jnp.int32, (n, n), 1)
    keep = c <= r
    if block is not None:
        sh = block.bit_length() - 1
        keep = keep & ((r >> sh) == (c >> sh))
    return jnp.where(keep, 1.0, 0.0).astype(dtype)


def _cumsum_rows(tril_bf16, x):
    hi, mid, lo = _split3(x)
    return _dot(tril_bf16, hi) + _dot(tril_bf16, mid) + _dot(tril_bf16, lo)


def _sigmoid(x):
    return 0.5 + 0.5 * jnp.tanh(0.5 * x)


def _silu(x):
    h = 0.5 * x
    return h + h * jnp.tanh(h)


def _softplus(x):
    return jnp.maximum(x, 0.0) + jnp.log1p(jnp.exp(-jnp.abs(x)))


def _in_proj_kernel(x_ref, g_ref, wt_ref, cw_ref, fox_ref, gdn_ref, small_ref,
                    w_ref, halo_ref, win_ref, *, tiles_per_seq):
    i = pl.program_id(0)
    d = GDN_HEAD_DIM
    g0 = 3 * D_FOX
    s0 = g0 + 4 * D_GDN

    @pl.when(i == 0)
    def _regroup_weights():
        fsrc = 3 * D_FOX
        gsrc = fsrc + N_FOX_HEADS
        bsrc = gsrc + 3 * D_GDN
        zsrc = bsrc + 2 * N_GDN_HEADS
        rb = 256

        def copy_rows(dst, src, count, scale=None):
            for r in range(0, count, rb):
                v = wt_ref[src + r:src + r + rb, :]
                if scale is not None:
                    v = v * scale
                w_ref[dst + r:dst + r + rb, :] = v.astype(BF16)

        copy_rows(0, 0, D_FOX, FOX_HEAD_DIM ** -0.5 * LOG2E)
        copy_rows(D_FOX, D_FOX, 2 * D_FOX)
        copy_rows(g0, gsrc, 3 * D_GDN)
        copy_rows(g0 + 3 * D_GDN, zsrc, D_GDN)
        ff = wt_ref[fsrc:fsrc + N_FOX_HEADS, :]
        ba = wt_ref[bsrc:bsrc + 2 * N_GDN_HEADS, :]
        zero8 = jnp.zeros_like(ff)
        pad_b = (SMALL_B0 - FOX_REP * N_FOX_HEADS) // 8
        pad_end = (LANES - SMALL_A0 - N_GDN_HEADS) // 8
        small = jnp.concatenate([ff] * FOX_REP + [zero8] * pad_b + [ba] + [zero8] * pad_end, axis=0)
        w_ref[s0:, :] = small.astype(BF16)

    h = _rms(x_ref[...], g_ref[...]).astype(BF16)
    tm = h.shape[0]
    first = i % tiles_per_seq == 0
    slab = 2 * d
    for sl in range(4 * D_GDN // slab):
        cols = slice(sl * slab, (sl + 1) * slab)
        if sl * slab < 3 * D_FOX:
            fox_ref[:, cols] = _dot_nt(h, w_ref[cols, :]).astype(BF16)
        g = _dot_nt(h, w_ref[g0 + sl * slab:g0 + (sl + 1) * slab, :])
        if sl * slab >= 3 * D_GDN:
            gdn_ref[:, cols] = _silu(g).astype(BF16)
            continue
        win_ref[0:8, :] = jnp.where(first, 0.0, halo_ref[:, cols])
        win_ref[8:, :] = g
        halo_ref[:, cols] = g[tm - 8:, :]
        w = cw_ref[:, cols]
        y = g * w[CONV_K - 1:CONV_K, :]
        for j in range(1, CONV_K):
            shifted = win_ref[8 - j:8 - j + tm, :]
            y = y + shifted * w[CONV_K - 1 - j:CONV_K - j, :]
        act = _silu(y)
        if sl * slab >= 2 * D_GDN:
            gdn_ref[:, cols] = act.astype(BF16)
            continue
        for hd in range(slab // d):
            t = act[:, hd * d:(hd + 1) * d]
            r = lax.rsqrt(jnp.sum(t * t, axis=-1, keepdims=True) + EPS)
            if sl * slab < D_GDN:
                r = r * (d ** -0.5)
            gdn_ref[:, sl * slab + hd * d:sl * slab + (hd + 1) * d] = (t * r).astype(BF16)
    small_ref[...] = _dot_nt(h, w_ref[s0:, :])


def _in_proj(x2, gain, w_t, conv_w, seq, tm):
    n, d = x2.shape
    const = lambda i: (0, 0)
    row = lambda i: (i, 0)
    kern = functools.partial(_in_proj_kernel, tiles_per_seq=seq // tm)
    return pl.pallas_call(
        kern,
        grid=(n // tm,),
        in_specs=[
            pl.BlockSpec((tm, d), row),
            pl.BlockSpec((1, d), const),
            pl.BlockSpec(w_t.shape, const, pipeline_mode=pl.Buffered(1)),
            pl.BlockSpec(conv_w.shape, const),
        ],
        out_specs=[
            pl.BlockSpec((tm, 3 * D_FOX), row),
            pl.BlockSpec((tm, 4 * D_GDN), row),
            pl.BlockSpec((tm, LANES), row),
        ],
        out_shape=[
            jax.ShapeDtypeStruct((n, 3 * D_FOX), BF16),
            jax.ShapeDtypeStruct((n, 4 * D_GDN), BF16),
            jax.ShapeDtypeStruct((n, LANES), F32),
        ],
        scratch_shapes=[pltpu.VMEM((3 * D_FOX + 4 * D_GDN + LANES, d), BF16),
                        pltpu.VMEM((8, 3 * D_GDN), F32),
                        pltpu.VMEM((tm + 8, 2 * GDN_HEAD_DIM), F32)],
        compiler_params=pltpu.CompilerParams(
            dimension_semantics=("arbitrary",), vmem_limit_bytes=VMEM_LIMIT),
        name="in_proj",
    )(x2, gain, w_t, conv_w)


def _fox_kernel(q_ref, k_ref, v_ref, small_ref, vec_ref, *rest, seq, tq, tk, dg, cum_blk, n_cast):
    cast_in = rest[:n_cast]
    o_ref = rest[n_cast]
    cast_out = rest[n_cast + 1:2 * n_cast + 1]
    qx_ref, kx_ref, vaug_ref = rest[2 * n_cast + 1:]
    for src, dst in zip(cast_in, cast_out):
        dst[...] = src[...].astype(dst.dtype)
    p = pl.program_id(1)
    lane = lax.broadcasted_iota(jnp.int32, (1, LANES), 1)
    half = FOX_HEAD_DIM
    nh = N_FOX_HEADS
    head_mask = (lane < half, lane >= half)
    ones_lane = (half, 0)
    zero = jnp.zeros((), BF16)

    @pl.when(p == 0)
    def _per_sequence():
        tril = _tril_ones(cum_blk, BF16)
        carry = jnp.zeros((1, LANES), F32)
        for blk in range(seq // cum_blk):
            rows = slice(blk * cum_blk, (blk + 1) * cum_blk)
            z = small_ref[rows, :] + vec_ref[VEC_FBIAS:VEC_FBIAS + 1, :]
            c = _cumsum_rows(tril, -_softplus(-z)) + carry
            carry = c[cum_blk - 1:cum_blk, :]
            hi, mid, lo = (t.astype(F32) for t in _split3(c * LOG2E))
            qx_ref[rows, :] = jnp.where(lane < nh, hi, jnp.where(lane < 2 * nh, mid, jnp.where(
                lane < 3 * nh, lo, jnp.where(lane < 6 * nh, 1.0, 0.0)))).astype(BF16)
            kx_ref[rows, :] = jnp.where(lane < 3 * nh, 1.0, jnp.where(lane < 4 * nh, -hi, jnp.where(
                lane < 5 * nh, -mid, jnp.where(lane < 6 * nh, -lo, 0.0)))).astype(BF16)

    vp = v_ref[...]
    for e in range(2):
        ones_col = jnp.where(lane == ones_lane[e], 1.0, 0.0).astype(BF16)
        vaug_ref[e] = jnp.where(head_mask[e], vp, ones_col)

    def block(qaug, j0, width, row_lo, carry, masked):
        kb = jnp.concatenate([k_ref[pl.ds(j0, width), :], kx_ref[pl.ds(j0, width), :]], axis=1)
        nrows = tq - row_lo
        out = []
        for e in range(2):
            m_all, acc_all = carry[e]
            m, acc = m_all[row_lo:], acc_all[row_lo:]
            vb = vaug_ref[e, pl.ds(j0, width), :]
            s = _dot_nt(qaug[e][row_lo:], kb)
            if masked:
                rr = lax.broadcasted_iota(jnp.int32, (width, width), 0)
                cc = lax.broadcasted_iota(jnp.int32, (width, width), 1)
                top = jnp.where(cc <= rr, s[:width], -jnp.inf)
                s = top if nrows == width else jnp.concatenate([top, s[width:]], axis=0)
            m_new = jnp.maximum(m, jnp.max(s, axis=-1, keepdims=True))
            alpha = jnp.exp2(m - m_new)
            pexp = jnp.exp2(s - m_new)
            acc = alpha * acc + _dot(pexp.astype(BF16), vb)
            if row_lo:
                m_new = jnp.concatenate([m_all[:row_lo], m_new], axis=0)
                acc = jnp.concatenate([acc_all[:row_lo], acc], axis=0)
            out.append((m_new, acc))
        return tuple(out)

    for qi in range(seq // tq):
        r0 = qi * tq
        qs = q_ref[r0:r0 + tq, :]
        qx = qx_ref[r0:r0 + tq, :]
        qaug = []
        for e in range(2):
            xmask = ((lane & (nh - 1)) == 2 * p + e) & (lane < 6 * nh)
            qaug.append(jnp.concatenate(
                [jnp.where(head_mask[e], qs, zero), jnp.where(xmask, qx, zero)], axis=1))
        carry = tuple((jnp.full((tq, 1), -jnp.inf, F32), jnp.zeros((tq, LANES), F32))
                      for _ in range(2))
        for j in range(r0 // tk):
            carry = block(qaug, j * tk, tk, 0, carry, False)
        for c in range(tq // dg):
            carry = block(qaug, r0 + c * dg, dg, c * dg, carry, True)

        outs = []
        for e in range(2):
            _, acc = carry[e]
            l = jnp.sum(jnp.where(lane == ones_lane[e], acc, 0.0), axis=-1, keepdims=True)
            outs.append(acc / l)
        o = jnp.where(head_mask[0], outs[0], outs[1])
        o2 = o * o
        ss0 = jnp.sum(jnp.where(head_mask[0], o2, 0.0), axis=-1, keepdims=True)
        ss1 = jnp.sum(jnp.where(head_mask[1], o2, 0.0), axis=-1, keepdims=True)
        ms = jnp.where(head_mask[0], ss0, ss1) * (1.0 / FOX_HEAD_DIM)
        o_ref[r0:r0 + tq, :] = (o * lax.rsqrt(ms + EPS)
                                * vec_ref[VEC_FNORM:VEC_FNORM + 1, :]).astype(o_ref.dtype)


def _fox(fox_qkv, small, vecs, cast_weights, batch, seq, tq, tk, dg):
    n = batch * seq
    pairs = N_FOX_HEADS // 2
    steps = batch * pairs
    kern = functools.partial(_fox_kernel, seq=seq, tq=tq, tk=tk, dg=dg, cum_blk=256,
                             n_cast=len(cast_weights))
    slab = lambda b, p: (b * pairs + p, 0)
    cast_specs = []
    for w in cast_weights:
        rows = w.shape[0] // steps
        assert rows * steps == w.shape[0] and rows % 16 == 0, w.shape
        cast_specs.append(pl.BlockSpec((rows, w.shape[1]), slab))
    outs = pl.pallas_call(
        kern,
        grid=(batch, pairs),
        in_specs=[
            pl.BlockSpec((seq, LANES), lambda b, p: (b, p)),
            pl.BlockSpec((seq, LANES), lambda b, p: (b, pairs + p)),
            pl.BlockSpec((seq, LANES), lambda b, p: (b, 2 * pairs + p)),
            pl.BlockSpec((seq, LANES), lambda b, p: (b, 0)),
            pl.BlockSpec(vecs.shape, lambda b, p: (0, 0)),
        ] + cast_specs,
        out_specs=[pl.BlockSpec((seq, LANES), lambda b, p: (b, p))] + cast_specs,
        out_shape=[jax.ShapeDtypeStruct((n, D_FOX), BF16)]
        + [jax.ShapeDtypeStruct(w.shape, BF16) for w in cast_weights],
        scratch_shapes=[
            pltpu.VMEM((seq, LANES), BF16),
            pltpu.VMEM((seq, LANES), BF16),
            pltpu.VMEM((2, seq, LANES), BF16),
        ],
        compiler_params=pltpu.CompilerParams(
            dimension_semantics=("arbitrary", "arbitrary"),
            vmem_limit_bytes=VMEM_LIMIT),
        name="fox_attention",
    )(fox_qkv, fox_qkv, fox_qkv, small, vecs, *cast_weights)
    return outs[0], outs[1:]


def _merge_levels(xs, m_lows, sizes):
    c = m_lows[0].shape[0]
    row = lax.broadcasted_iota(jnp.int32, (c, c), 0)
    col = lax.broadcasted_iota(jnp.int32, (c, c), 1)
    for size in sizes:
        sh = size.bit_length() - 1
        rb = row >> sh
        below = ((rb & 1) == 1) & ((col >> sh) == rb - 1)
        xbfs = [x.astype(BF16) for x in xs]
        nxs = [_dot(jnp.where(below, m, 0.0).astype(BF16), xb).astype(BF16)
               for m, xb in zip(m_lows, xbfs)]
        xs = [x - _dot(xb, nx) for x, xb, nx in zip(xs, xbfs, nxs)]
        yield
    return xs


MERGE_SIZES = (2, 4, 8, 16, 32, 64)


def _gdn_kernel(q_ref, k_ref, v_ref, gate_ref, small_ref, vec_ref, o_ref,
                w2_ref, rr_ref, qp_ref, op_ref, gl_ref, st_ref, *, seq, heads, group):
    c = CHUNK
    d = GDN_HEAD_DIM
    rows = group * c
    n_groups = seq // rows
    lane = lax.broadcasted_iota(jnp.int32, (1, LANES), 1)
    row = lax.broadcasted_iota(jnp.int32, (c, c), 0)
    col = lax.broadcasted_iota(jnp.int32, (c, c), 1)
    tril = _tril_ones(rows, BF16, block=c)

    def aligned(x, m):
        return x if isinstance(x, int) else pl.multiple_of(x, m)

    def prep_stages(gi):
        r0 = aligned(gi * rows, rows)
        qa = q_ref[pl.ds(r0, rows), :].astype(F32)
        ka = k_ref[pl.ds(r0, rows), :]
        va = v_ref[pl.ds(r0, rows), :].astype(F32)
        sm = small_ref[pl.ds(r0, rows), :]
        g_all = -jnp.exp(vec_ref[VEC_ALOG:VEC_ALOG + 1, :]) * _softplus(
            sm + vec_ref[VEC_DTB:VEC_DTB + 1, :])
        gc_all = _cumsum_rows(tril, g_all)
        items = []
        for h in range(heads):
            qh = qa[:, h * d:(h + 1) * d]
            kh = ka[:, h * d:(h + 1) * d]
            vh = va[:, h * d:(h + 1) * d]
            beta_h = _sigmoid(jnp.sum(jnp.where(lane == SMALL_B0 + h, sm, 0.0),
                                      axis=-1, keepdims=True))
            gc_h = jnp.sum(jnp.where(lane == SMALL_A0 + h, gc_all, 0.0),
                           axis=-1, keepdims=True)
            for g in range(group):
                sl = slice(g * c, (g + 1) * c)
                items.append(dict(h=h, g=g, q=qh[sl], k=kh[sl].astype(F32), kbf=kh[sl], v=vh[sl],
                                  beta=beta_h[sl], gc=gc_h[sl]))
        for it in items:
            gc = it["gc"]
            hi, mid, lo = (t.astype(F32) for t in _split3(gc))
            it["al"] = jnp.where(lane == 0, hi, jnp.where(lane == 1, mid, jnp.where(
                lane == 2, lo, jnp.where(lane < 6, 1.0, 0.0)))).astype(BF16)
            it["ar"] = jnp.where(lane < 3, 1.0, jnp.where(lane == 3, -hi, jnp.where(
                lane == 4, -mid, jnp.where(lane == 5, -lo, 0.0)))).astype(BF16)
            it["kb"] = it["k"] * it["beta"]
        dmats = [_dot_nt(it["al"], it["ar"]) for it in items]
        kks = [_dot_nt(it["kb"].astype(BF16), it["kbf"]) for it in items]
        qks = [_dot_nt(it["q"].astype(BF16), it["kbf"]) for it in items]
        m_lows = []
        for it, dmat, kk, qk in zip(items, dmats, kks, qks):
            decay = jnp.exp(jnp.where(col <= row, dmat, -jnp.inf))
            m_lows.append(jnp.where(col < row, kk * decay, 0.0))
            it["a"] = (qk * decay).astype(BF16)
        yield
        eye = jnp.where(row == col, 1.0, 0.0)
        pair = (row >> 1) == (col >> 1)
        xs = [eye - jnp.where(pair, m, 0.0) for m in m_lows]
        t_invs = yield from _merge_levels(xs, m_lows, MERGE_SIZES)
        wus = []
        for it, t_inv in zip(items, t_invs):
            eg = jnp.exp(it["gc"])
            it["eg"] = eg
            rhs = jnp.concatenate(
                [(it["kb"] * eg).astype(BF16), (it["v"] * it["beta"]).astype(BF16)], axis=1)
            wus.append(_dot(t_inv.astype(BF16), rhs).astype(BF16))
        yield
        kwus, awus = [], []
        for it, wu in zip(items, wus):
            gc_last = it["gc"][c - 1:c, :]
            it["gl"] = jnp.exp(gc_last)
            kd = (it["k"] * jnp.exp(gc_last - it["gc"])).astype(BF16)
            kwus.append(_dot_tn(kd, wu))
            awus.append(_dot(it["a"], wu))
        for it, kwu, awu in zip(items, kwus, awus):
            h = it["h"]
            ci = gi * group + it["g"]
            rw = aligned(ci * d, d)
            rc = aligned(ci * c, c)
            w2_ref[h, pl.ds(rw, d), :] = kwu[:, :d].astype(BF16)
            rr_ref[h, pl.ds(rw, d), :] = kwu[:, d:]
            qp_ref[h, pl.ds(rc, c), :] = (it["q"] * it["eg"] - awu[:, :d]).astype(BF16)
            op_ref[h, pl.ds(rc, c), :] = awu[:, d:]
            gl_ref[h, pl.ds(aligned(ci * 8, 8), 8), :] = jnp.broadcast_to(it["gl"], (8, d))

    def scan_chunk(ci):
        rc = aligned(ci * c, c)
        rw = aligned(ci * d, d)
        for h in range(heads):
            state = st_ref[h]
            lhs = jnp.concatenate([w2_ref[h, pl.ds(rw, d), :], qp_ref[h, pl.ds(rc, c), :]], axis=0)
            r = _dot(lhs, state.astype(BF16))
            gl = gl_ref[h, pl.ds(aligned(ci * 8, 8), 8), :][0:1, :]
            st_ref[h] = state * gl - r[:d] + rr_ref[h, pl.ds(rw, d), :]
            o = r[d:] + op_ref[h, pl.ds(rc, c), :]
            gate = gate_ref[pl.ds(rc, c), h * d:(h + 1) * d].astype(F32)
            o_ref[pl.ds(rc, c), h * d:(h + 1) * d] = (
                _rms(o, vec_ref[VEC_ONORM:VEC_ONORM + 1, :]) * gate).astype(o_ref.dtype)

    def prep_and_scan(gi, scan_group):
        pending = [] if scan_group is None else [scan_group * group + j for j in range(group)]
        for _ in prep_stages(gi):
            if pending:
                scan_chunk(pending.pop(0))
        for ci in pending:
            scan_chunk(ci)

    st_ref[...] = jnp.zeros_like(st_ref)
    prep_and_scan(0, None)

    def body(gi, _):
        prep_and_scan(gi, gi - 1)
        return 0

    lax.fori_loop(1, n_groups, body, 0)
    for j in range(group):
        scan_chunk((n_groups - 1) * group + j)


def _gdn(gdn_in, small, vecs, batch, seq, group):
    n = batch * seq
    d = GDN_HEAD_DIM
    heads = N_GDN_HEADS
    n_chunks = seq // CHUNK
    kern = functools.partial(_gdn_kernel, seq=seq, heads=heads, group=group)
    col_block = lambda part: (lambda b: (b, part))
    const = lambda b: (0, 0)
    return pl.pallas_call(
        kern,
        grid=(batch,),
        in_specs=[
            pl.BlockSpec((seq, D_GDN), col_block(0)),
            pl.BlockSpec((seq, D_GDN), col_block(1)),
            pl.BlockSpec((seq, D_GDN), col_block(2)),
            pl.BlockSpec((seq, D_GDN), col_block(3)),
            pl.BlockSpec((seq, LANES), lambda b: (b, 0)),
            pl.BlockSpec(vecs.shape, const),
        ],
        out_specs=pl.BlockSpec((seq, D_GDN), lambda b: (b, 0)),
        out_shape=jax.ShapeDtypeStruct((n, D_GDN), BF16),
        scratch_shapes=[
            pltpu.VMEM((heads, n_chunks * d, d), BF16),
            pltpu.VMEM((heads, n_chunks * d, d), F32),
            pltpu.VMEM((heads, seq, d), BF16),
            pltpu.VMEM((heads, seq, d), F32),
            pltpu.VMEM((heads, n_chunks * 8, d), F32),
            pltpu.VMEM((heads, d, d), F32),
        ],
        compiler_params=pltpu.CompilerParams(
            dimension_semantics=("arbitrary",), vmem_limit_bytes=VMEM_LIMIT),
        name="gated_deltanet",
    )(gdn_in, gdn_in, gdn_in, gdn_in, small, vecs)


def _out_mlp_kernel(fox_ref, gdn_ref, x_ref, wo_ref, pmix_ref, pre_ref,
                    wup_ref, wdn_ref, post_ref, o_ref, *, ff_blk):
    mixed = _dot(fox_ref[...], wo_ref[:D_FOX, :]) + _dot(gdn_ref[...], wo_ref[D_FOX:, :])
    x1 = x_ref[...] + _rms(mixed, pmix_ref[...])
    h = _rms(x1, pre_ref[...]).astype(BF16)
    d_ff = wup_ref.shape[1]
    y = None
    for j in range(d_ff // ff_blk):
        a = _dot(h, wup_ref[:, j * ff_blk:(j + 1) * ff_blk])
        a = jnp.square(jnp.maximum(a, 0.0)).astype(BF16)
        t = _dot(a, wdn_ref[j * ff_blk:(j + 1) * ff_blk, :])
        y = t if y is None else y + t
    o_ref[...] = x1 + _rms(y, post_ref[...])


def _out_mlp(fox_o, gdn_o, x2, wo, pmix, pre, wup, wdn, post, tm):
    n, d = x2.shape
    const = lambda i: (0, 0)
    row = lambda i: (i, 0)
    single = pl.Buffered(1)
    kern = functools.partial(_out_mlp_kernel, ff_blk=1024)
    return pl.pallas_call(
        kern,
        grid=(n // tm,),
        in_specs=[
            pl.BlockSpec((tm, fox_o.shape[1]), row),
            pl.BlockSpec((tm, gdn_o.shape[1]), row),
            pl.BlockSpec((tm, d), row),
            pl.BlockSpec(wo.shape, const, pipeline_mode=single),
            pl.BlockSpec((1, d), const),
            pl.BlockSpec((1, d), const),
            pl.BlockSpec(wup.shape, const, pipeline_mode=single),
            pl.BlockSpec(wdn.shape, const, pipeline_mode=single),
            pl.BlockSpec((1, d), const),
        ],
        out_specs=pl.BlockSpec((tm, d), row),
        out_shape=jax.ShapeDtypeStruct((n, d), F32),
        compiler_params=pltpu.CompilerParams(
            dimension_semantics=("arbitrary",), vmem_limit_bytes=VMEM_LIMIT),
        name="out_mlp",
    )(fox_o, gdn_o, x2, wo, pmix, pre, wup, wdn, post)


def _vec_row(values):
    values = values.astype(F32)
    return jnp.concatenate([values, jnp.zeros((LANES - values.shape[0],), F32)])


def kernel(x, pre_mix_norm, w_in, fox_f_bias, fox_out_norm, gdn_conv_w, gdn_a_log, gdn_dt_bias,
           gdn_out_norm, w_out, post_mix_norm, pre_mlp_norm, w_up, w_down, post_mlp_norm):
    b, s, d = x.shape
    n = b * s
    x2 = x.reshape(n, d)


    zeros_b = jnp.zeros((SMALL_A0,), F32)
    vecs = jnp.stack([
        _vec_row(jnp.tile(fox_f_bias, FOX_REP)),
        _vec_row(jnp.tile(fox_out_norm, 2)),
        _vec_row(jnp.concatenate([zeros_b, gdn_a_log.astype(F32)])),
        _vec_row(jnp.concatenate([zeros_b, gdn_dt_bias.astype(F32)])),
        _vec_row(gdn_out_norm),
        jnp.zeros((LANES,), F32), jnp.zeros((LANES,), F32), jnp.zeros((LANES,), F32)])

    fox_qkv, gdn_in, small = _in_proj(x2, pre_mix_norm.reshape(1, d).astype(F32),
                                      w_in.astype(F32).T, gdn_conv_w.astype(F32), s, tm=1024)
    fox_o, (wo, wup, wdn) = _fox(
        fox_qkv, small, vecs, [w_out.astype(F32), w_up.astype(F32), w_down.astype(F32)],
        b, s, tq=1024, tk=512, dg=512)
    gdn_o = _gdn(gdn_in, small, vecs, b, s, group=4)
    out = _out_mlp(fox_o, gdn_o, x2, wo,
                   post_mix_norm.reshape(1, d).astype(F32), pre_mlp_norm.reshape(1, d).astype(F32),
                   wup, wdn, post_mlp_norm.reshape(1, d).astype(F32), tm=512)
    return out.reshape(b, s, d)
```

```python
import functools

import jax
import jax.numpy as jnp
from jax import lax
from jax.experimental import pallas as pl
from jax.experimental.pallas import tpu as pltpu

F32 = jnp.float32
BF16 = jnp.bfloat16

EPS = 1e-6
LOG2E = 1.4426950408889634
LANES = 128
FOX_HEAD_DIM = 64
N_FOX_HEADS = 8
D_FOX = FOX_HEAD_DIM * N_FOX_HEADS
GDN_HEAD_DIM = 128
N_GDN_HEADS = 4
D_GDN = GDN_HEAD_DIM * N_GDN_HEADS
CHUNK = 128
CONV_K = 4

FOX_REP = 6
SMALL_B0 = 64
SMALL_A0 = SMALL_B0 + N_GDN_HEADS

VEC_FBIAS, VEC_FNORM, VEC_ALOG, VEC_DTB, VEC_ONORM = range(5)

VMEM_LIMIT = 56 * 1024 * 1024


def _dot(a, b):
    return jnp.dot(a, b, preferred_element_type=F32)


def _dot_nt(a, b):
    return lax.dot_general(a, b, (((1,), (1,)), ((), ())), preferred_element_type=F32)


def _dot_tn(a, b):
    return lax.dot_general(a, b, (((0,), (0,)), ((), ())), preferred_element_type=F32)


def _rms(x, w):
    return x * lax.rsqrt(jnp.mean(x * x, axis=-1, keepdims=True) + EPS) * w


def _split3(x):
    hi = x.astype(BF16)
    r1 = x - hi.astype(F32)
    mid = r1.astype(BF16)
    lo = (r1 - mid.astype(F32)).astype(BF16)
    return hi, mid, lo


def _tril_ones(n, dtype, block=None):
    r = lax.broadcasted_iota(jnp.int32, (n, n), 0)
    c = lax.broadcasted_iota(jnp.int32, (n, n), 1)
    keep = c <= r
    if block is not None:
        sh = block.bit_length() - 1
        keep = keep & ((r >> sh) == (c >> sh))
    return jnp.where(keep, 1.0, 0.0).astype(dtype)


def _cumsum_rows(tril_bf16, x):
    hi, mid, lo = _split3(x)
    return _dot(tril_bf16, hi) + _dot(tril_bf16, mid) + _dot(tril_bf16, lo)


def _sigmoid(x):
    return 0.5 + 0.5 * jnp.tanh(0.5 * x)


def _silu(x):
    h = 0.5 * x
    return h + h * jnp.tanh(h)


def _softplus(x):
    return jnp.maximum(x, 0.0) + jnp.log1p(jnp.exp(-jnp.abs(x)))


def _in_proj_kernel(x_ref, g_ref, wt_ref, cw_ref, fox_ref, gdn_ref, small_ref,
                    w_ref, halo_ref, win_ref, *, tiles_per_seq):
    i = pl.program_id(0)
    d = GDN_HEAD_DIM
    g0 = 3 * D_FOX
    s0 = g0 + 4 * D_GDN

    @pl.when(i == 0)
    def _regroup_weights():
        fsrc = 3 * D_FOX
        gsrc = fsrc + N_FOX_HEADS
        bsrc = gsrc + 3 * D_GDN
        zsrc = bsrc + 2 * N_GDN_HEADS
        rb = 256

        def copy_rows(dst, src, count, scale=None):
            for r in range(0, count, rb):
                v = wt_ref[src + r:src + r + rb, :]
                if scale is not None:
                    v = v * scale
                w_ref[dst + r:dst + r + rb, :] = v.astype(BF16)

        copy_rows(0, 0, D_FOX, FOX_HEAD_DIM ** -0.5 * LOG2E)
        copy_rows(D_FOX, D_FOX, 2 * D_FOX)
        copy_rows(g0, gsrc, 3 * D_GDN)
        copy_rows(g0 + 3 * D_GDN, zsrc, D_GDN)
        ff = wt_ref[fsrc:fsrc + N_FOX_HEADS, :]
        ba = wt_ref[bsrc:bsrc + 2 * N_GDN_HEADS, :]
        zero8 = jnp.zeros_like(ff)
        pad_b = (SMALL_B0 - FOX_REP * N_FOX_HEADS) // 8
        pad_end = (LANES - SMALL_A0 - N_GDN_HEADS) // 8
        small = jnp.concatenate([ff] * FOX_REP + [zero8] * pad_b + [ba] + [zero8] * pad_end, axis=0)
        w_ref[s0:, :] = small.astype(BF16)

    h = _rms(x_ref[...], g_ref[...]).astype(BF16)
    tm = h.shape[0]
    first = i % tiles_per_seq == 0
    slab = 2 * d
    for sl in range(4 * D_GDN // slab):
        cols = slice(sl * slab, (sl + 1) * slab)
        if sl * slab < 3 * D_FOX:
            fox_ref[:, cols] = _dot_nt(h, w_ref[cols, :]).astype(BF16)
        g = _dot_nt(h, w_ref[g0 + sl * slab:g0 + (sl + 1) * slab, :])
        if sl * slab >= 3 * D_GDN:
            gdn_ref[:, cols] = _silu(g).astype(BF16)
            continue
        win_ref[0:8, :] = jnp.where(first, 0.0, halo_ref[:, cols])
        win_ref[8:, :] = g
        halo_ref[:, cols] = g[tm - 8:, :]
        w = cw_ref[:, cols]
        y = g * w[CONV_K - 1:CONV_K, :]
        for j in range(1, CONV_K):
            shifted = win_ref[8 - j:8 - j + tm, :]
            y = y + shifted * w[CONV_K - 1 - j:CONV_K - j, :]
        act = _silu(y)
        if sl * slab >= 2 * D_GDN:
            gdn_ref[:, cols] = act.astype(BF16)
            continue
        for hd in range(slab // d):
            t = act[:, hd * d:(hd + 1) * d]
            r = lax.rsqrt(jnp.sum(t * t, axis=-1, keepdims=True) + EPS)
            if sl * slab < D_GDN:
                r = r * (d ** -0.5)
            gdn_ref[:, sl * slab + hd * d:sl * slab + (hd + 1) * d] = (t * r).astype(BF16)
    small_ref[...] = _dot_nt(h, w_ref[s0:, :])


def _in_proj(x2, gain, w_t, conv_w, seq, tm):
    n, d = x2.shape
    const = lambda i: (0, 0)
    row = lambda i: (i, 0)
    kern = functools.partial(_in_proj_kernel, tiles_per_seq=seq // tm)
    return pl.pallas_call(
        kern,
        grid=(n // tm,),
        in_specs=[
            pl.BlockSpec((tm, d), row),
            pl.BlockSpec((1, d), const),
            pl.BlockSpec(w_t.shape, const, pipeline_mode=pl.Buffered(1)),
            pl.BlockSpec(conv_w.shape, const),
        ],
        out_specs=[
            pl.BlockSpec((tm, 3 * D_FOX), row),
            pl.BlockSpec((tm, 4 * D_GDN), row),
            pl.BlockSpec((tm, LANES), row),
        ],
        out_shape=[
            jax.ShapeDtypeStruct((n, 3 * D_FOX), BF16),
            jax.ShapeDtypeStruct((n, 4 * D_GDN), BF16),
            jax.ShapeDtypeStruct((n, LANES), F32),
        ],
        scratch_shapes=[pltpu.VMEM((3 * D_FOX + 4 * D_GDN + LANES, d), BF16),
                        pltpu.VMEM((8, 3 * D_GDN), F32),
                        pltpu.VMEM((tm + 8, 2 * GDN_HEAD_DIM), F32)],
        compiler_params=pltpu.CompilerParams(
            dimension_semantics=("arbitrary",), vmem_limit_bytes=VMEM_LIMIT),
        name="in_proj",
    )(x2, gain, w_t, conv_w)


def _fox_kernel(q_ref, k_ref, v_ref, small_ref, vec_ref, *rest, seq, tq, tk, dg, cum_blk, n_cast):
    cast_in = rest[:n_cast]
    o_ref = rest[n_cast]
    cast_out = rest[n_cast + 1:2 * n_cast + 1]
    qx_ref, kx_ref, vaug_ref = rest[2 * n_cast + 1:]
    for src, dst in zip(cast_in, cast_out):
        dst[...] = src[...].astype(dst.dtype)
    p = pl.program_id(1)
    lane = lax.broadcasted_iota(jnp.int32, (1, LANES), 1)
    half = FOX_HEAD_DIM
    nh = N_FOX_HEADS
    head_mask = (lane < half, lane >= half)
    ones_lane = (half, 0)
    zero = jnp.zeros((), BF16)

    @pl.when(p == 0)
    def _per_sequence():
        gr = 16
        gi = lax.broadcasted_iota(jnp.int32, (gr, LANES), 0)
        gl = lax.broadcasted_iota(jnp.int32, (gr, LANES), 1)
        onehot = (gi == gl) & (gi < nh)
        sel = jnp.where(onehot, 1.0, 0.0).astype(BF16)
        bias_t = jnp.sum(jnp.where(onehot, vec_ref[VEC_FBIAS:VEC_FBIAS + 1, :], 0.0),
                         axis=-1, keepdims=True)
        tj = lax.broadcasted_iota(jnp.int32, (cum_blk, cum_blk), 0)
        ti = lax.broadcasted_iota(jnp.int32, (cum_blk, cum_blk), 1)
        triu = jnp.where(tj <= ti, 1.0, 0.0).astype(BF16)
        pr = lax.broadcasted_iota(jnp.int32, (4 * gr, LANES), 0)
        pc = lax.broadcasted_iota(jnp.int32, (4 * gr, LANES), 1)
        grp, g = pr // gr, pr % gr
        piece = (grp < 3) & (g < nh)
        ones_row = (grp == 3) & (g == 0)
        pq = jnp.where(piece & (pc == grp * nh + g), 1.0, jnp.where(
            ones_row & (pc >= 3 * nh) & (pc < 6 * nh), 1.0, 0.0)).astype(BF16)
        pk = jnp.where(piece & (pc == (3 + grp) * nh + g), -1.0, jnp.where(
            ones_row & (pc < 3 * nh), 1.0, 0.0)).astype(BF16)
        ones_t = jnp.ones((gr, cum_blk), BF16)
        n_blk = seq // cum_blk
        zts = []
        for blk in range(n_blk):
            z = small_ref[blk * cum_blk:(blk + 1) * cum_blk, :]
            zh = z.astype(BF16)
            zm = (z - zh.astype(F32)).astype(BF16)
            zts.append(_dot_nt(sel, zh) + _dot_nt(sel, zm))
        parts = [_split3(-_softplus(-(zt + bias_t))) for zt in zts]
        locs = [_dot(hi, triu) + _dot(mid, triu) + _dot(lo, triu) for hi, mid, lo in parts]
        totals = [loc[:, cum_blk - 1:cum_blk] for loc in locs]
        carry = jnp.zeros((gr, 1), F32)
        stacks = []
        for loc, tot in zip(locs, totals):
            c = loc + carry
            carry = carry + tot
            stacks.append(jnp.concatenate(list(_split3(c * LOG2E)) + [ones_t], axis=0))
        for blk, pieces in enumerate(stacks):
            rows = slice(blk * cum_blk, (blk + 1) * cum_blk)
            qx_ref[rows, :] = _dot_tn(pieces, pq).astype(BF16)
            kx_ref[rows, :] = _dot_tn(pieces, pk).astype(BF16)

    vp = v_ref[...]
    for e in range(2):
        ones_col = jnp.where(lane == ones_lane[e], 1.0, 0.0).astype(BF16)
        vaug_ref[e] = jnp.where(head_mask[e], vp, ones_col)

    def block(qaug, j0, width, row_lo, carry, masked):
        kb = jnp.concatenate([k_ref[pl.ds(j0, width), :], kx_ref[pl.ds(j0, width), :]], axis=1)
        nrows = tq - row_lo
        out = []
        for e in range(2):
            m_all, acc_all = carry[e]
            m, acc = m_all[row_lo:], acc_all[row_lo:]
            vb = vaug_ref[e, pl.ds(j0, width), :]
            s = _dot_nt(qaug[e][row_lo:], kb)
            if masked:
                rr = lax.broadcasted_iota(jnp.int32, (width, width), 0)
                cc = lax.broadcasted_iota(jnp.int32, (width, width), 1)
                top = jnp.where(cc <= rr, s[:width], -jnp.inf)
                s = top if nrows == width else jnp.concatenate([top, s[width:]], axis=0)
            m_new = jnp.maximum(m, jnp.max(s, axis=-1, keepdims=True))
            alpha = jnp.exp2(m - m_new)
            pexp = jnp.exp2(s - m_new)
            acc = alpha * acc + _dot(pexp.astype(BF16), vb)
            if row_lo:
                m_new = jnp.concatenate([m_all[:row_lo], m_new], axis=0)
                acc = jnp.concatenate([acc_all[:row_lo], acc], axis=0)
            out.append((m_new, acc))
        return tuple(out)

    for qi in range(seq // tq):
        r0 = qi * tq
        qs = q_ref[r0:r0 + tq, :]
        qx = qx_ref[r0:r0 + tq, :]
        qaug = []
        for e in range(2):
            xmask = ((lane & (nh - 1)) == 2 * p + e) & (lane < 6 * nh)
            qaug.append(jnp.concatenate(
                [jnp.where(head_mask[e], qs, zero), jnp.where(xmask, qx, zero)], axis=1))
        carry = tuple((jnp.full((tq, 1), -jnp.inf, F32), jnp.zeros((tq, LANES), F32))
                      for _ in range(2))
        for j in range(r0 // tk):
            carry = block(qaug, j * tk, tk, 0, carry, False)
        for c in range(tq // dg):
            carry = block(qaug, r0 + c * dg, dg, c * dg, carry, True)

        outs = []
        for e in range(2):
            _, acc = carry[e]
            l = jnp.sum(jnp.where(lane == ones_lane[e], acc, 0.0), axis=-1, keepdims=True)
            outs.append(acc / l)
        o = jnp.where(head_mask[0], outs[0], outs[1])
        o2 = o * o
        ss0 = jnp.sum(jnp.where(head_mask[0], o2, 0.0), axis=-1, keepdims=True)
        ss1 = jnp.sum(jnp.where(head_mask[1], o2, 0.0), axis=-1, keepdims=True)
        ms = jnp.where(head_mask[0], ss0, ss1) * (1.0 / FOX_HEAD_DIM)
        o_ref[r0:r0 + tq, :] = (o * lax.rsqrt(ms + EPS)
                                * vec_ref[VEC_FNORM:VEC_FNORM + 1, :]).astype(o_ref.dtype)


def _fox(fox_qkv, small, vecs, cast_weights, batch, seq, tq, tk, dg):
    n = batch * seq
    pairs = N_FOX_HEADS // 2
    steps = batch * pairs
    kern = functools.partial(_fox_kernel, seq=seq, tq=tq, tk=tk, dg=dg, cum_blk=256,
                             n_cast=len(cast_weights))
    slab = lambda b, p: (b * pairs + p, 0)
    cast_specs = []
    for w in cast_weights:
        rows = w.shape[0] // steps
        assert rows * steps == w.shape[0] and rows % 16 == 0, w.shape
        cast_specs.append(pl.BlockSpec((rows, w.shape[1]), slab))
    outs = pl.pallas_call(
        kern,
        grid=(batch, pairs),
        in_specs=[
            pl.BlockSpec((seq, LANES), lambda b, p: (b, p)),
            pl.BlockSpec((seq, LANES), lambda b, p: (b, pairs + p)),
            pl.BlockSpec((seq, LANES), lambda b, p: (b, 2 * pairs + p)),
            pl.BlockSpec((seq, LANES), lambda b, p: (b, 0)),
            pl.BlockSpec(vecs.shape, lambda b, p: (0, 0)),
        ] + cast_specs,
        out_specs=[pl.BlockSpec((seq, LANES), lambda b, p: (b, p))] + cast_specs,
        out_shape=[jax.ShapeDtypeStruct((n, D_FOX), BF16)]
        + [jax.ShapeDtypeStruct(w.shape, BF16) for w in cast_weights],
        scratch_shapes=[
            pltpu.VMEM((seq, LANES), BF16),
            pltpu.VMEM((seq, LANES), BF16),
            pltpu.VMEM((2, seq, LANES), BF16),
        ],
        compiler_params=pltpu.CompilerParams(
            dimension_semantics=("arbitrary", "arbitrary"),
            vmem_limit_bytes=VMEM_LIMIT),
        name="fox_attention",
    )(fox_qkv, fox_qkv, fox_qkv, small, vecs, *cast_weights)
    return outs[0], outs[1:]


def _merge_levels(xs, m_lows, sizes):
    c = m_lows[0].shape[0]
    row = lax.broadcasted_iota(jnp.int32, (c, c), 0)
    col = lax.broadcasted_iota(jnp.int32, (c, c), 1)
    for size in sizes:
        sh = size.bit_length() - 1
        rb = row >> sh
        below = ((rb & 1) == 1) & ((col >> sh) == rb - 1)
        xbfs = [x.astype(BF16) for x in xs]
        nxs = [_dot(jnp.where(below, m, 0.0).astype(BF16), xb).astype(BF16)
               for m, xb in zip(m_lows, xbfs)]
        xs = [x - _dot(xb, nx) for x, xb, nx in zip(xs, xbfs, nxs)]
        yield
    return xs


MERGE_SIZES = (2, 4, 8, 16, 32, 64)


def _gdn_kernel(q_ref, k_ref, v_ref, gate_ref, small_ref, vec_ref, o_ref,
                w2_ref, rr_ref, qp_ref, op_ref, gl_ref, st_ref, *, seq, heads, group):
    c = CHUNK
    d = GDN_HEAD_DIM
    rows = group * c
    n_groups = seq // rows
    lane = lax.broadcasted_iota(jnp.int32, (1, LANES), 1)
    row = lax.broadcasted_iota(jnp.int32, (c, c), 0)
    col = lax.broadcasted_iota(jnp.int32, (c, c), 1)
    tril = _tril_ones(rows, BF16, block=c)

    def aligned(x, m):
        return x if isinstance(x, int) else pl.multiple_of(x, m)

    def prep_stages(gi):
        r0 = aligned(gi * rows, rows)
        qa = q_ref[pl.ds(r0, rows), :].astype(F32)
        ka = k_ref[pl.ds(r0, rows), :]
        va = v_ref[pl.ds(r0, rows), :].astype(F32)
        sm = small_ref[pl.ds(r0, rows), :]
        g_all = -jnp.exp(vec_ref[VEC_ALOG:VEC_ALOG + 1, :]) * _softplus(
            sm + vec_ref[VEC_DTB:VEC_DTB + 1, :])
        gc_all = _cumsum_rows(tril, g_all)
        items = []
        for h in range(heads):
            qh = qa[:, h * d:(h + 1) * d]
            kh = ka[:, h * d:(h + 1) * d]
            vh = va[:, h * d:(h + 1) * d]
            beta_h = _sigmoid(jnp.sum(jnp.where(lane == SMALL_B0 + h, sm, 0.0),
                                      axis=-1, keepdims=True))
            gc_h = jnp.sum(jnp.where(lane == SMALL_A0 + h, gc_all, 0.0),
                           axis=-1, keepdims=True)
            for g in range(group):
                sl = slice(g * c, (g + 1) * c)
                items.append(dict(h=h, g=g, q=qh[sl], k=kh[sl].astype(F32), kbf=kh[sl], v=vh[sl],
                                  beta=beta_h[sl], gc=gc_h[sl]))
        for it in items:
            gc = it["gc"]
            hi, mid, lo = (t.astype(F32) for t in _split3(gc))
            it["al"] = jnp.where(lane == 0, hi, jnp.where(lane == 1, mid, jnp.where(
                lane == 2, lo, jnp.where(lane < 6, 1.0, 0.0)))).astype(BF16)
            it["ar"] = jnp.where(lane < 3, 1.0, jnp.where(lane == 3, -hi, jnp.where(
                lane == 4, -mid, jnp.where(lane == 5, -lo, 0.0)))).astype(BF16)
            it["kb"] = it["k"] * it["beta"]
        dmats = [_dot_nt(it["al"], it["ar"]) for it in items]
        kks = [_dot_nt(it["kb"].astype(BF16), it["kbf"]) for it in items]
        qks = [_dot_nt(it["q"].astype(BF16), it["kbf"]) for it in items]
        m_lows = []
        for it, dmat, kk, qk in zip(items, dmats, kks, qks):
            decay = jnp.exp(jnp.where(col <= row, dmat, -jnp.inf))
            m_lows.append(jnp.where(col < row, kk * decay, 0.0))
            it["a"] = (qk * decay).astype(BF16)
        yield
        eye = jnp.where(row == col, 1.0, 0.0)
        pair = (row >> 1) == (col >> 1)
        xs = [eye - jnp.where(pair, m, 0.0) for m in m_lows]
        t_invs = yield from _merge_levels(xs, m_lows, MERGE_SIZES)
        wus = []
        for it, t_inv in zip(items, t_invs):
            eg = jnp.exp(it["gc"])
            it["eg"] = eg
            rhs = jnp.concatenate(
                [(it["kb"] * eg).astype(BF16), (it["v"] * it["beta"]).astype(BF16)], axis=1)
            wus.append(_dot(t_inv.astype(BF16), rhs).astype(BF16))
        yield
        kwus, awus = [], []
        for it, wu in zip(items, wus):
            gc_last = it["gc"][c - 1:c, :]
            it["gl"] = jnp.exp(gc_last)
            kd = (it["k"] * jnp.exp(gc_last - it["gc"])).astype(BF16)
            kwus.append(_dot_tn(kd, wu))
            awus.append(_dot(it["a"], wu))
        for it, kwu, awu in zip(items, kwus, awus):
            h = it["h"]
            ci = gi * group + it["g"]
            rw = aligned(ci * d, d)
            rc = aligned(ci * c, c)
            w2_ref[h, pl.ds(rw, d), :] = kwu[:, :d].astype(BF16)
            rr_ref[h, pl.ds(rw, d), :] = kwu[:, d:]
            qp_ref[h, pl.ds(rc, c), :] = (it["q"] * it["eg"] - awu[:, :d]).astype(BF16)
            op_ref[h, pl.ds(rc, c), :] = awu[:, d:]
            gl_ref[h, pl.ds(aligned(ci * 8, 8), 8), :] = jnp.broadcast_to(it["gl"], (8, d))

    def scan_chunk(ci):
        rc = aligned(ci * c, c)
        rw = aligned(ci * d, d)
        for h in range(heads):
            state = st_ref[h]
            lhs = jnp.concatenate([w2_ref[h, pl.ds(rw, d), :], qp_ref[h, pl.ds(rc, c), :]], axis=0)
            r = _dot(lhs, state.astype(BF16))
            gl = gl_ref[h, pl.ds(aligned(ci * 8, 8), 8), :][0:1, :]
            st_ref[h] = state * gl - r[:d] + rr_ref[h, pl.ds(rw, d), :]
            o = r[d:] + op_ref[h, pl.ds(rc, c), :]
            gate = gate_ref[pl.ds(rc, c), h * d:(h + 1) * d].astype(F32)
            o_ref[pl.ds(rc, c), h * d:(h + 1) * d] = (
                _rms(o, vec_ref[VEC_ONORM:VEC_ONORM + 1, :]) * gate).astype(o_ref.dtype)

    def prep_and_scan(gi, scan_group):
        pending = [] if scan_group is None else [scan_group * group + j for j in range(group)]
        for _ in prep_stages(gi):
            if pending:
                scan_chunk(pending.pop(0))
        for ci in pending:
            scan_chunk(ci)

    st_ref[...] = jnp.zeros_like(st_ref)
    prep_and_scan(0, None)

    def body(gi, _):
        prep_and_scan(gi, gi - 1)
        return 0

    lax.fori_loop(1, n_groups, body, 0)
    for j in range(group):
        scan_chunk((n_groups - 1) * group + j)


def _gdn(gdn_in, small, vecs, batch, seq, group):
    n = batch * seq
    d = GDN_HEAD_DIM
    heads = N_GDN_HEADS
    n_chunks = seq // CHUNK
    kern = functools.partial(_gdn_kernel, seq=seq, heads=heads, group=group)
    col_block = lambda part: (lambda b: (b, part))
    const = lambda b: (0, 0)
    return pl.pallas_call(
        kern,
        grid=(batch,),
        in_specs=[
            pl.BlockSpec((seq, D_GDN), col_block(0)),
            pl.BlockSpec((seq, D_GDN), col_block(1)),
            pl.BlockSpec((seq, D_GDN), col_block(2)),
            pl.BlockSpec((seq, D_GDN), col_block(3)),
            pl.BlockSpec((seq, LANES), lambda b: (b, 0)),
            pl.BlockSpec(vecs.shape, const),
        ],
        out_specs=pl.BlockSpec((seq, D_GDN), lambda b: (b, 0)),
        out_shape=jax.ShapeDtypeStruct((n, D_GDN), BF16),
        scratch_shapes=[
            pltpu.VMEM((heads, n_chunks * d, d), BF16),
            pltpu.VMEM((heads, n_chunks * d, d), F32),
            pltpu.VMEM((heads, seq, d), BF16),
            pltpu.VMEM((heads, seq, d), F32),
            pltpu.VMEM((heads, n_chunks * 8, d), F32),
            pltpu.VMEM((heads, d, d), F32),
        ],
        compiler_params=pltpu.CompilerParams(
            dimension_semantics=("arbitrary",), vmem_limit_bytes=VMEM_LIMIT),
        name="gated_deltanet",
    )(gdn_in, gdn_in, gdn_in, gdn_in, small, vecs)


def _out_mlp_kernel(fox_ref, gdn_ref, x_ref, wo_ref, pmix_ref, pre_ref,
                    wup_ref, wdn_ref, post_ref, o_ref, *, ff_blk):
    mixed = _dot(fox_ref[...], wo_ref[:D_FOX, :]) + _dot(gdn_ref[...], wo_ref[D_FOX:, :])
    x1 = x_ref[...] + _rms(mixed, pmix_ref[...])
    h = _rms(x1, pre_ref[...]).astype(BF16)
    d_ff = wup_ref.shape[1]
    y = None
    for j in range(d_ff // ff_blk):
        a = _dot(h, wup_ref[:, j * ff_blk:(j + 1) * ff_blk])
        a = jnp.square(jnp.maximum(a, 0.0)).astype(BF16)
        t = _dot(a, wdn_ref[j * ff_blk:(j + 1) * ff_blk, :])
        y = t if y is None else y + t
    o_ref[...] = x1 + _rms(y, post_ref[...])


def _out_mlp(fox_o, gdn_o, x2, wo, pmix, pre, wup, wdn, post, tm):
    n, d = x2.shape
    const = lambda i: (0, 0)
    row = lambda i: (i, 0)
    single = pl.Buffered(1)
    kern = functools.partial(_out_mlp_kernel, ff_blk=1024)
    return pl.pallas_call(
        kern,
        grid=(n // tm,),
        in_specs=[
            pl.BlockSpec((tm, fox_o.shape[1]), row),
            pl.BlockSpec((tm, gdn_o.shape[1]), row),
            pl.BlockSpec((tm, d), row),
            pl.BlockSpec(wo.shape, const, pipeline_mode=single),
            pl.BlockSpec((1, d), const),
            pl.BlockSpec((1, d), const),
            pl.BlockSpec(wup.shape, const, pipeline_mode=single),
            pl.BlockSpec(wdn.shape, const, pipeline_mode=single),
            pl.BlockSpec((1, d), const),
        ],
        out_specs=pl.BlockSpec((tm, d), row),
        out_shape=jax.ShapeDtypeStruct((n, d), F32),
        compiler_params=pltpu.CompilerParams(
            dimension_semantics=("arbitrary",), vmem_limit_bytes=VMEM_LIMIT),
        name="out_mlp",
    )(fox_o, gdn_o, x2, wo, pmix, pre, wup, wdn, post)


def _vec_row(values):
    values = values.astype(F32)
    return jnp.concatenate([values, jnp.zeros((LANES - values.shape[0],), F32)])


def kernel(x, pre_mix_norm, w_in, fox_f_bias, fox_out_norm, gdn_conv_w, gdn_a_log, gdn_dt_bias,
           gdn_out_norm, w_out, post_mix_norm, pre_mlp_norm, w_up, w_down, post_mlp_norm):
    b, s, d = x.shape
    n = b * s
    x2 = x.reshape(n, d)


    zeros_b = jnp.zeros((SMALL_A0,), F32)
    vecs = jnp.stack([
        _vec_row(jnp.tile(fox_f_bias, FOX_REP)),
        _vec_row(jnp.tile(fox_out_norm, 2)),
        _vec_row(jnp.concatenate([zeros_b, gdn_a_log.astype(F32)])),
        _vec_row(jnp.concatenate([zeros_b, gdn_dt_bias.astype(F32)])),
        _vec_row(gdn_out_norm),
        jnp.zeros((LANES,), F32), jnp.zeros((LANES,), F32), jnp.zeros((LANES,), F32)])

    fox_qkv, gdn_in, small = _in_proj(x2, pre_mix_norm.reshape(1, d).astype(F32),
                                      w_in.astype(F32).T, gdn_conv_w.astype(F32), s, tm=1024)
    fox_o, (wo, wup, wdn) = _fox(
        fox_qkv, small, vecs, [w_out.astype(F32), w_up.astype(F32), w_down.astype(F32)],
        b, s, tq=1024, tk=512, dg=512)
    gdn_o = _gdn(gdn_in, small, vecs, b, s, group=4)
    out = _out_mlp(fox_o, gdn_o, x2, wo,
                   post_mix_norm.reshape(1, d).astype(F32), pre_mlp_norm.reshape(1, d).astype(F32),
                   wup, wdn, post_mlp_norm.reshape(1, d).astype(F32), tm=512)
    return out.reshape(b, s, d)
```

```python
import functools

import jax
import jax.numpy as jnp
from jax import lax
from jax.experimental import pallas as pl
from jax.experimental.pallas import tpu as pltpu

F32 = jnp.float32
BF16 = jnp.bfloat16

EPS = 1e-6
LOG2E = 1.4426950408889634
LANES = 128
FOX_HEAD_DIM = 64
N_FOX_HEADS = 8
D_FOX = FOX_HEAD_DIM * N_FOX_HEADS
GDN_HEAD_DIM = 128
N_GDN_HEADS = 4
D_GDN = GDN_HEAD_DIM * N_GDN_HEADS
CHUNK = 128
CONV_K = 4

FOX_REP = 6
SMALL_B0 = 64
SMALL_A0 = SMALL_B0 + N_GDN_HEADS

VEC_FBIAS, VEC_FNORM, VEC_ALOG, VEC_DTB, VEC_ONORM = range(5)

VMEM_LIMIT = 56 * 1024 * 1024


def _dot(a, b):
    return jnp.dot(a, b, preferred_element_type=F32)


def _dot_nt(a, b):
    return lax.dot_general(a, b, (((1,), (1,)), ((), ())), preferred_element_type=F32)


def _dot_tn(a, b):
    return lax.dot_general(a, b, (((0,), (0,)), ((), ())), preferred_element_type=F32)


def _rms(x, w):
    return x * lax.rsqrt(jnp.mean(x * x, axis=-1, keepdims=True) + EPS) * w


def _split3(x):
    hi = x.astype(BF16)
    r1 = x - hi.astype(F32)
    mid = r1.astype(BF16)
    lo = (r1 - mid.astype(F32)).astype(BF16)
    return hi, mid, lo


def _tril_ones(n, dtype, block=None):
    r = lax.broadcasted_iota(jnp.int32, (n, n), 0)
    c = lax.broadcasted_iota(jnp.int32, (n, n), 1)
    keep = c <= r
    if block is not None:
        sh = block.bit_length() - 1
        keep = keep & ((r >> sh) == (c >> sh))
    return jnp.where(keep, 1.0, 0.0).astype(dtype)


def _cumsum_rows(tril_bf16, x):
    hi, mid, lo = _split3(x)
    return _dot(tril_bf16, hi) + _dot(tril_bf16, mid) + _dot(tril_bf16, lo)


def _sigmoid(x):
    return 0.5 + 0.5 * jnp.tanh(0.5 * x)


def _silu(x):
    h = 0.5 * x
    return h + h * jnp.tanh(h)


def _softplus(x):
    return jnp.maximum(x, 0.0) + jnp.log1p(jnp.exp(-jnp.abs(x)))


def _in_proj_kernel(x_ref, g_ref, wt_ref, cw_ref, fox_ref, gdn_ref, small_ref,
                    w_ref, halo_ref, win_ref, *, tiles_per_seq):
    i = pl.program_id(0)
    d = GDN_HEAD_DIM
    g0 = 3 * D_FOX
    s0 = g0 + 4 * D_GDN

    @pl.when(i == 0)
    def _regroup_weights():
        fsrc = 3 * D_FOX
        gsrc = fsrc + N_FOX_HEADS
        bsrc = gsrc + 3 * D_GDN
        zsrc = bsrc + 2 * N_GDN_HEADS
        rb = 256

        def copy_rows(dst, src, count, scale=None):
            for r in range(0, count, rb):
                v = wt_ref[src + r:src + r + rb, :]
                if scale is not None:
                    v = v * scale
                w_ref[dst + r:dst + r + rb, :] = v.astype(BF16)

        copy_rows(0, 0, D_FOX, FOX_HEAD_DIM ** -0.5 * LOG2E)
        copy_rows(D_FOX, D_FOX, 2 * D_FOX)
        copy_rows(g0, gsrc, 3 * D_GDN)
        copy_rows(g0 + 3 * D_GDN, zsrc, D_GDN)
        ff = wt_ref[fsrc:fsrc + N_FOX_HEADS, :]
        ba = wt_ref[bsrc:bsrc + 2 * N_GDN_HEADS, :]
        zero8 = jnp.zeros_like(ff)
        pad_b = (SMALL_B0 - FOX_REP * N_FOX_HEADS) // 8
        pad_end = (LANES - SMALL_A0 - N_GDN_HEADS) // 8
        small = jnp.concatenate([ff] * FOX_REP + [zero8] * pad_b + [ba] + [zero8] * pad_end, axis=0)
        w_ref[s0:, :] = small.astype(BF16)

    h = _rms(x_ref[...], g_ref[...]).astype(BF16)
    tm = h.shape[0]
    first = i % tiles_per_seq == 0
    slab = 2 * d
    for sl in range(4 * D_GDN // slab):
        cols = slice(sl * slab, (sl + 1) * slab)
        if sl * slab < 3 * D_FOX:
            fox_ref[:, cols] = _dot_nt(h, w_ref[cols, :]).astype(BF16)
        g = _dot_nt(h, w_ref[g0 + sl * slab:g0 + (sl + 1) * slab, :])
        if sl * slab >= 3 * D_GDN:
            gdn_ref[:, cols] = _silu(g).astype(BF16)
            continue
        win_ref[0:8, :] = jnp.where(first, 0.0, halo_ref[:, cols])
        win_ref[8:, :] = g
        halo_ref[:, cols] = g[tm - 8:, :]
        w = cw_ref[:, cols]
        y = g * w[CONV_K - 1:CONV_K, :]
        for j in range(1, CONV_K):
            shifted = win_ref[8 - j:8 - j + tm, :]
            y = y + shifted * w[CONV_K - 1 - j:CONV_K - j, :]
        act = _silu(y)
        if sl * slab >= 2 * D_GDN:
            gdn_ref[:, cols] = act.astype(BF16)
            continue
        for hd in range(slab // d):
            t = act[:, hd * d:(hd + 1) * d]
            r = lax.rsqrt(jnp.sum(t * t, axis=-1, keepdims=True) + EPS)
            if sl * slab < D_GDN:
                r = r * (d ** -0.5)
            gdn_ref[:, sl * slab + hd * d:sl * slab + (hd + 1) * d] = (t * r).astype(BF16)
    small_ref[...] = _dot_nt(h, w_ref[s0:, :])


def _in_proj(x2, gain, w_t, conv_w, seq, tm):
    n, d = x2.shape
    const = lambda i: (0, 0)
    row = lambda i: (i, 0)
    kern = functools.partial(_in_proj_kernel, tiles_per_seq=seq // tm)
    return pl.pallas_call(
        kern,
        grid=(n // tm,),
        in_specs=[
            pl.BlockSpec((tm, d), row),
            pl.BlockSpec((1, d), const),
            pl.BlockSpec(w_t.shape, const, pipeline_mode=pl.Buffered(1)),
            pl.BlockSpec(conv_w.shape, const),
        ],
        out_specs=[
            pl.BlockSpec((tm, 3 * D_FOX), row),
            pl.BlockSpec((tm, 4 * D_GDN), row),
            pl.BlockSpec((tm, LANES), row),
        ],
        out_shape=[
            jax.ShapeDtypeStruct((n, 3 * D_FOX), BF16),
            jax.ShapeDtypeStruct((n, 4 * D_GDN), BF16),
            jax.ShapeDtypeStruct((n, LANES), F32),
        ],
        scratch_shapes=[pltpu.VMEM((3 * D_FOX + 4 * D_GDN + LANES, d), BF16),
                        pltpu.VMEM((8, 3 * D_GDN), F32),
                        pltpu.VMEM((tm + 8, 2 * GDN_HEAD_DIM), F32)],
        compiler_params=pltpu.CompilerParams(
            dimension_semantics=("arbitrary",), vmem_limit_bytes=VMEM_LIMIT),
        name="in_proj",
    )(x2, gain, w_t, conv_w)


def _fox_kernel(q_ref, k_ref, v_ref, small_ref, vec_ref, *rest, seq, tq, tk, dg, cum_blk, n_cast):
    cast_in = rest[:n_cast]
    o_ref = rest[n_cast]
    cast_out = rest[n_cast + 1:2 * n_cast + 1]
    qx_ref, kx_ref, vaug_ref = rest[2 * n_cast + 1:]
    for src, dst in zip(cast_in, cast_out):
        dst[...] = src[...].astype(dst.dtype)
    p = pl.program_id(1)
    lane = lax.broadcasted_iota(jnp.int32, (1, LANES), 1)
    half = FOX_HEAD_DIM
    nh = N_FOX_HEADS
    head_mask = (lane < half, lane >= half)
    ones_lane = (half, 0)
    zero = jnp.zeros((), BF16)

    @pl.when(p == 0)
    def _per_sequence():
        gr = 16
        gi = lax.broadcasted_iota(jnp.int32, (gr, LANES), 0)
        gl = lax.broadcasted_iota(jnp.int32, (gr, LANES), 1)
        onehot = (gi == gl) & (gi < nh)
        sel = jnp.where(onehot, 1.0, 0.0).astype(BF16)
        bias_t = jnp.sum(jnp.where(onehot, vec_ref[VEC_FBIAS:VEC_FBIAS + 1, :], 0.0),
                         axis=-1, keepdims=True)
        tj = lax.broadcasted_iota(jnp.int32, (cum_blk, cum_blk), 0)
        ti = lax.broadcasted_iota(jnp.int32, (cum_blk, cum_blk), 1)
        triu = jnp.where(tj <= ti, 1.0, 0.0).astype(BF16)
        pr = lax.broadcasted_iota(jnp.int32, (4 * gr, LANES), 0)
        pc = lax.broadcasted_iota(jnp.int32, (4 * gr, LANES), 1)
        grp, g = pr // gr, pr % gr
        piece = (grp < 3) & (g < nh)
        ones_row = (grp == 3) & (g == 0)
        pq = jnp.where(piece & (pc == grp * nh + g), 1.0, jnp.where(
            ones_row & (pc >= 3 * nh) & (pc < 6 * nh), 1.0, 0.0)).astype(BF16)
        pk = jnp.where(piece & (pc == (3 + grp) * nh + g), -1.0, jnp.where(
            ones_row & (pc < 3 * nh), 1.0, 0.0)).astype(BF16)
        ones_t = jnp.ones((gr, cum_blk), BF16)
        n_blk = seq // cum_blk
        zts = []
        for blk in range(n_blk):
            z = small_ref[blk * cum_blk:(blk + 1) * cum_blk, :]
            zh = z.astype(BF16)
            zm = (z - zh.astype(F32)).astype(BF16)
            zts.append(_dot_nt(sel, zh) + _dot_nt(sel, zm))
        parts = [_split3(-_softplus(-(zt + bias_t))) for zt in zts]
        locs = [_dot(hi, triu) + _dot(mid, triu) + _dot(lo, triu) for hi, mid, lo in parts]
        totals = [loc[:, cum_blk - 1:cum_blk] for loc in locs]
        carry = jnp.zeros((gr, 1), F32)
        stacks = []
        for loc, tot in zip(locs, totals):
            c = loc + carry
            carry = carry + tot
            stacks.append(jnp.concatenate(list(_split3(c * LOG2E)) + [ones_t], axis=0))
        for blk, pieces in enumerate(stacks):
            rows = slice(blk * cum_blk, (blk + 1) * cum_blk)
            qx_ref[rows, :] = _dot_tn(pieces, pq).astype(BF16)
            kx_ref[rows, :] = _dot_tn(pieces, pk).astype(BF16)

    vp = v_ref[...]
    for e in range(2):
        ones_col = jnp.where(lane == ones_lane[e], 1.0, 0.0).astype(BF16)
        vaug_ref[e] = jnp.where(head_mask[e], vp, ones_col)

    def block(qaug, j0, width, row_lo, carry, masked):
        kb = jnp.concatenate([k_ref[pl.ds(j0, width), :], kx_ref[pl.ds(j0, width), :]], axis=1)
        nrows = tq - row_lo
        out = []
        for e in range(2):
            m_all, acc_all = carry[e]
            m, acc = m_all[row_lo:], acc_all[row_lo:]
            vb = vaug_ref[e, pl.ds(j0, width), :]
            s = _dot_nt(qaug[e][row_lo:], kb)
            if masked:
                rr = lax.broadcasted_iota(jnp.int32, (width, width), 0)
                cc = lax.broadcasted_iota(jnp.int32, (width, width), 1)
                top = jnp.where(cc <= rr, s[:width], -jnp.inf)
                s = top if nrows == width else jnp.concatenate([top, s[width:]], axis=0)
            m_new = jnp.maximum(m, jnp.max(s, axis=-1, keepdims=True))
            alpha = jnp.exp2(m - m_new)
            pexp = jnp.exp2(s - m_new)
            acc = alpha * acc + _dot(pexp.astype(BF16), vb)
            if row_lo:
                m_new = jnp.concatenate([m_all[:row_lo], m_new], axis=0)
                acc = jnp.concatenate([acc_all[:row_lo], acc], axis=0)
            out.append((m_new, acc))
        return tuple(out)

    for qi in range(seq // tq):
        r0 = qi * tq
        qs = q_ref[r0:r0 + tq, :]
        qx = qx_ref[r0:r0 + tq, :]
        qaug = []
        for e in range(2):
            xmask = ((lane & (nh - 1)) == 2 * p + e) & (lane < 6 * nh)
            qaug.append(jnp.concatenate(
                [jnp.where(head_mask[e], qs, zero), jnp.where(xmask, qx, zero)], axis=1))
        carry = tuple((jnp.full((tq, 1), -jnp.inf, F32), jnp.zeros((tq, LANES), F32))
                      for _ in range(2))
        for j in range(r0 // tk):
            carry = block(qaug, j * tk, tk, 0, carry, False)
        for c in range(tq // dg):
            carry = block(qaug, r0 + c * dg, dg, c * dg, carry, True)

        outs = []
        for e in range(2):
            _, acc = carry[e]
            l = jnp.sum(jnp.where(lane == ones_lane[e], acc, 0.0), axis=-1, keepdims=True)
            outs.append(acc / l)
        o = jnp.where(head_mask[0], outs[0], outs[1])
        o2 = o * o
        ss0 = jnp.sum(jnp.where(head_mask[0], o2, 0.0), axis=-1, keepdims=True)
        ss1 = jnp.sum(jnp.where(head_mask[1], o2, 0.0), axis=-1, keepdims=True)
        ms = jnp.where(head_mask[0], ss0, ss1) * (1.0 / FOX_HEAD_DIM)
        o_ref[r0:r0 + tq, :] = (o * lax.rsqrt(ms + EPS)
                                * vec_ref[VEC_FNORM:VEC_FNORM + 1, :]).astype(o_ref.dtype)


def _fox(fox_qkv, small, vecs, cast_weights, batch, seq, tq, tk, dg):
    n = batch * seq
    pairs = N_FOX_HEADS // 2
    steps = batch * pairs
    kern = functools.partial(_fox_kernel, seq=seq, tq=tq, tk=tk, dg=dg, cum_blk=256,
                             n_cast=len(cast_weights))
    slab = lambda b, p: (b * pairs + p, 0)
    cast_specs = []
    for w in cast_weights:
        rows = w.shape[0] // steps
        assert rows * steps == w.shape[0] and rows % 16 == 0, w.shape
        cast_specs.append(pl.BlockSpec((rows, w.shape[1]), slab))
    outs = pl.pallas_call(
        kern,
        grid=(batch, pairs),
        in_specs=[
            pl.BlockSpec((seq, LANES), lambda b, p: (b, p)),
            pl.BlockSpec((seq, LANES), lambda b, p: (b, pairs + p)),
            pl.BlockSpec((seq, LANES), lambda b, p: (b, 2 * pairs + p)),
            pl.BlockSpec((seq, LANES), lambda b, p: (b, 0)),
            pl.BlockSpec(vecs.shape, lambda b, p: (0, 0)),
        ] + cast_specs,
        out_specs=[pl.BlockSpec((seq, LANES), lambda b, p: (b, p))] + cast_specs,
        out_shape=[jax.ShapeDtypeStruct((n, D_FOX), BF16)]
        + [jax.ShapeDtypeStruct(w.shape, BF16) for w in cast_weights],
        scratch_shapes=[
            pltpu.VMEM((seq, LANES), BF16),
            pltpu.VMEM((seq, LANES), BF16),
            pltpu.VMEM((2, seq, LANES), BF16),
        ],
        compiler_params=pltpu.CompilerParams(
            dimension_semantics=("arbitrary", "arbitrary"),
            vmem_limit_bytes=VMEM_LIMIT),
        name="fox_attention",
    )(fox_qkv, fox_qkv, fox_qkv, small, vecs, *cast_weights)
    return outs[0], outs[1:]


def _merge_levels(xs, m_lows, sizes):
    c = m_lows[0].shape[0]
    row = lax.broadcasted_iota(jnp.int32, (c, c), 0)
    col = lax.broadcasted_iota(jnp.int32, (c, c), 1)
    for size in sizes:
        sh = size.bit_length() - 1
        rb = row >> sh
        below = ((rb & 1) == 1) & ((col >> sh) == rb - 1)
        xbfs = [x.astype(BF16) for x in xs]
        nxs = [_dot(jnp.where(below, m, 0.0).astype(BF16), xb).astype(BF16)
               for m, xb in zip(m_lows, xbfs)]
        xs = [x - _dot(xb, nx) for x, xb, nx in zip(xs, xbfs, nxs)]
        yield
    return xs


MERGE_SIZES = (2, 4, 8, 16, 32, 64)


def _gdn_kernel(q_ref, k_ref, v_ref, gate_ref, small_ref, vec_ref, o_ref,
                w2_ref, rr_ref, qp_ref, op_ref, gl_ref, st_ref, *, seq, heads, group):
    c = CHUNK
    d = GDN_HEAD_DIM
    rows = group * c
    n_groups = seq // rows
    lane = lax.broadcasted_iota(jnp.int32, (1, LANES), 1)
    row = lax.broadcasted_iota(jnp.int32, (c, c), 0)
    col = lax.broadcasted_iota(jnp.int32, (c, c), 1)
    tril = _tril_ones(rows, BF16, block=c)

    def aligned(x, m):
        return x if isinstance(x, int) else pl.multiple_of(x, m)

    def prep_stages(gi):
        r0 = aligned(gi * rows, rows)
        qa = q_ref[pl.ds(r0, rows), :].astype(F32)
        ka = k_ref[pl.ds(r0, rows), :]
        va = v_ref[pl.ds(r0, rows), :].astype(F32)
        sm = small_ref[pl.ds(r0, rows), :]
        g_all = -jnp.exp(vec_ref[VEC_ALOG:VEC_ALOG + 1, :]) * _softplus(
            sm + vec_ref[VEC_DTB:VEC_DTB + 1, :])
        gc_all = _cumsum_rows(tril, g_all)
        items = []
        for h in range(heads):
            qh = qa[:, h * d:(h + 1) * d]
            kh = ka[:, h * d:(h + 1) * d]
            vh = va[:, h * d:(h + 1) * d]
            beta_h = _sigmoid(jnp.sum(jnp.where(lane == SMALL_B0 + h, sm, 0.0),
                                      axis=-1, keepdims=True))
            gc_h = jnp.sum(jnp.where(lane == SMALL_A0 + h, gc_all, 0.0),
                           axis=-1, keepdims=True)
            for g in range(group):
                sl = slice(g * c, (g + 1) * c)
                items.append(dict(h=h, g=g, q=qh[sl], k=kh[sl].astype(F32), kbf=kh[sl], v=vh[sl],
                                  beta=beta_h[sl], gc=gc_h[sl]))
        for it in items:
            gc = it["gc"]
            hi, mid, lo = (t.astype(F32) for t in _split3(gc))
            it["al"] = jnp.where(lane == 0, hi, jnp.where(lane == 1, mid, jnp.where(
                lane == 2, lo, jnp.where(lane < 6, 1.0, 0.0)))).astype(BF16)
            it["ar"] = jnp.where(lane < 3, 1.0, jnp.where(lane == 3, -hi, jnp.where(
                lane == 4, -mid, jnp.where(lane == 5, -lo, 0.0)))).astype(BF16)
            it["kb"] = it["k"] * it["beta"]
        dmats = [_dot_nt(it["al"], it["ar"]) for it in items]
        kks = [_dot_nt(it["kb"].astype(BF16), it["kbf"]) for it in items]
        qks = [_dot_nt(it["q"].astype(BF16), it["kbf"]) for it in items]
        m_lows = []
        for it, dmat, kk, qk in zip(items, dmats, kks, qks):
            decay = jnp.exp(jnp.where(col <= row, dmat, -jnp.inf))
            m_lows.append(jnp.where(col < row, kk * decay, 0.0))
            it["a"] = (qk * decay).astype(BF16)
        yield
        eye = jnp.where(row == col, 1.0, 0.0)
        pair = (row >> 1) == (col >> 1)
        xs = [eye - jnp.where(pair, m, 0.0) for m in m_lows]
        t_invs = yield from _merge_levels(xs, m_lows, MERGE_SIZES)
        wus = []
        for it, t_inv in zip(items, t_invs):
            eg = jnp.exp(it["gc"])
            it["eg"] = eg
            rhs = jnp.concatenate(
                [(it["kb"] * eg).astype(BF16), (it["v"] * it["beta"]).astype(BF16)], axis=1)
            wus.append(_dot(t_inv.astype(BF16), rhs).astype(BF16))
        yield
        kwus, awus = [], []
        for it, wu in zip(items, wus):
            gc_last = it["gc"][c - 1:c, :]
            it["gl"] = jnp.exp(gc_last)
            kd = (it["k"] * jnp.exp(gc_last - it["gc"])).astype(BF16)
            kwus.append(_dot_tn(kd, wu))
            awus.append(_dot(it["a"], wu))
        for it, kwu, awu in zip(items, kwus, awus):
            h = it["h"]
            ci = gi * group + it["g"]
            rw = aligned(ci * d, d)
            rc = aligned(ci * c, c)
            w2_ref[h, pl.ds(rw, d), :] = kwu[:, :d].astype(BF16)
            rr_ref[h, pl.ds(rw, d), :] = kwu[:, d:]
            qp_ref[h, pl.ds(rc, c), :] = (it["q"] * it["eg"] - awu[:, :d]).astype(BF16)
            op_ref[h, pl.ds(rc, c), :] = awu[:, d:]
            gl_ref[h, pl.ds(aligned(ci * 8, 8), 8), :] = jnp.broadcast_to(it["gl"], (8, d))

    def scan_chunk(ci):
        rc = aligned(ci * c, c)
        rw = aligned(ci * d, d)
        for h in range(heads):
            state = st_ref[h]
            lhs = jnp.concatenate([w2_ref[h, pl.ds(rw, d), :], qp_ref[h, pl.ds(rc, c), :]], axis=0)
            r = _dot(lhs, state.astype(BF16))
            gl = gl_ref[h, pl.ds(aligned(ci * 8, 8), 8), :][0:1, :]
            st_ref[h] = state * gl - r[:d] + rr_ref[h, pl.ds(rw, d), :]
            o = r[d:] + op_ref[h, pl.ds(rc, c), :]
            gate = gate_ref[pl.ds(rc, c), h * d:(h + 1) * d].astype(F32)
            o_ref[pl.ds(rc, c), h * d:(h + 1) * d] = (
                _rms(o, vec_ref[VEC_ONORM:VEC_ONORM + 1, :]) * gate).astype(o_ref.dtype)

    def prep_and_scan(gi, scan_group):
        pending = [] if scan_group is None else [scan_group * group + j for j in range(group)]
        for _ in prep_stages(gi):
            if pending:
                scan_chunk(pending.pop(0))
        for ci in pending:
            scan_chunk(ci)

    st_ref[...] = jnp.zeros_like(st_ref)
    prep_and_scan(0, None)

    def body(gi, _):
        prep_and_scan(gi, gi - 1)
        return 0

    lax.fori_loop(1, n_groups, body, 0)
    for j in range(group):
        scan_chunk((n_groups - 1) * group + j)


def _gdn(gdn_in, small, vecs, batch, seq, group):
    n = batch * seq
    d = GDN_HEAD_DIM
    heads = N_GDN_HEADS
    n_chunks = seq // CHUNK
    kern = functools.partial(_gdn_kernel, seq=seq, heads=heads, group=group)
    col_block = lambda part: (lambda b: (b, part))
    const = lambda b: (0, 0)
    return pl.pallas_call(
        kern,
        grid=(batch,),
        in_specs=[
            pl.BlockSpec((seq, D_GDN), col_block(0)),
            pl.BlockSpec((seq, D_GDN), col_block(1)),
            pl.BlockSpec((seq, D_GDN), col_block(2)),
            pl.BlockSpec((seq, D_GDN), col_block(3)),
            pl.BlockSpec((seq, LANES), lambda b: (b, 0)),
            pl.BlockSpec(vecs.shape, const),
        ],
        out_specs=pl.BlockSpec((seq, D_GDN), lambda b: (b, 0)),
        out_shape=jax.ShapeDtypeStruct((n, D_GDN), BF16),
        scratch_shapes=[
            pltpu.VMEM((heads, n_chunks * d, d), BF16),
            pltpu.VMEM((heads, n_chunks * d, d), F32),
            pltpu.VMEM((heads, seq, d), BF16),
            pltpu.VMEM((heads, seq, d), F32),
            pltpu.VMEM((heads, n_chunks * 8, d), F32),
            pltpu.VMEM((heads, d, d), F32),
        ],
        compiler_params=pltpu.CompilerParams(
            dimension_semantics=("arbitrary",), vmem_limit_bytes=VMEM_LIMIT),
        name="gated_deltanet",
    )(gdn_in, gdn_in, gdn_in, gdn_in, small, vecs)


def _out_mlp_kernel(fox_ref, gdn_ref, x_ref, wo_ref, pmix_ref, pre_ref,
                    wup_ref, wdn_ref, post_ref, o_ref, *, ff_blk):
    mixed = _dot(fox_ref[...], wo_ref[:D_FOX, :]) + _dot(gdn_ref[...], wo_ref[D_FOX:, :])
    x1 = x_ref[...] + _rms(mixed, pmix_ref[...])
    h = _rms(x1, pre_ref[...]).astype(BF16)
    d_ff = wup_ref.shape[1]
    y = None
    for j in range(d_ff // ff_blk):
        a = _dot(h, wup_ref[:, j * ff_blk:(j + 1) * ff_blk])
        a = jnp.square(jnp.maximum(a, 0.0)).astype(BF16)
        t = _dot(a, wdn_ref[j * ff_blk:(j + 1) * ff_blk, :])
        y = t if y is None else y + t
    o_ref[...] = x1 + _rms(y, post_ref[...])


def _out_mlp(fox_o, gdn_o, x2, wo, pmix, pre, wup, wdn, post, tm):
    n, d = x2.shape
    const = lambda i: (0, 0)
    row = lambda i: (i, 0)
    single = pl.Buffered(1)
    kern = functools.partial(_out_mlp_kernel, ff_blk=1024)
    return pl.pallas_call(
        kern,
        grid=(n // tm,),
        in_specs=[
            pl.BlockSpec((tm, fox_o.shape[1]), row),
            pl.BlockSpec((tm, gdn_o.shape[1]), row),
            pl.BlockSpec((tm, d), row),
            pl.BlockSpec(wo.shape, const, pipeline_mode=single),
            pl.BlockSpec((1, d), const),
            pl.BlockSpec((1, d), const),
            pl.BlockSpec(wup.shape, const, pipeline_mode=single),
            pl.BlockSpec(wdn.shape, const, pipeline_mode=single),
            pl.BlockSpec((1, d), const),
        ],
        out_specs=pl.BlockSpec((tm, d), row),
        out_shape=jax.ShapeDtypeStruct((n, d), F32),
        compiler_params=pltpu.CompilerParams(
            dimension_semantics=("arbitrary",), vmem_limit_bytes=VMEM_LIMIT),
        name="out_mlp",
    )(fox_o, gdn_o, x2, wo, pmix, pre, wup, wdn, post)


def _vec_row(values):
    values = values.astype(F32)
    return jnp.concatenate([values, jnp.zeros((LANES - values.shape[0],), F32)])


def kernel(x, pre_mix_norm, w_in, fox_f_bias, fox_out_norm, gdn_conv_w, gdn_a_log, gdn_dt_bias,
           gdn_out_norm, w_out, post_mix_norm, pre_mlp_norm, w_up, w_down, post_mlp_norm):
    b, s, d = x.shape
    n = b * s
    x2 = x.reshape(n, d)


    zeros_b = jnp.zeros((SMALL_A0,), F32)
    vecs = jnp.stack([
        _vec_row(jnp.tile(fox_f_bias, FOX_REP)),
        _vec_row(jnp.tile(fox_out_norm, 2)),
        _vec_row(jnp.concatenate([zeros_b, gdn_a_log.astype(F32)])),
        _vec_row(jnp.concatenate([zeros_b, gdn_dt_bias.astype(F32)])),
        _vec_row(gdn_out_norm),
        jnp.zeros((LANES,), F32), jnp.zeros((LANES,), F32), jnp.zeros((LANES,), F32)])

    fox_qkv, gdn_in, small = _in_proj(x2, pre_mix_norm.reshape(1, d).astype(F32),
                                      w_in.astype(F32).T, gdn_conv_w.astype(F32), s, tm=1024)
    fox_o, (wo, wup, wdn) = _fox(
        fox_qkv, small, vecs, [w_out.astype(F32), w_up.astype(F32), w_down.astype(F32)],
        b, s, tq=1024, tk=512, dg=512)
    gdn_o = _gdn(gdn_in, small, vecs, b, s, group=4)
    out = _out_mlp(fox_o, gdn_o, x2, wo,
                   post_mix_norm.reshape(1, d).astype(F32), pre_mlp_norm.reshape(1, d).astype(F32),
                   wup, wdn, post_mlp_norm.reshape(1, d).astype(F32), tm=1024)
    return out.reshape(b, s, d)
```

```python
import functools

import jax
import jax.numpy as jnp
from jax import lax
from jax.experimental import pallas as pl
from jax.experimental.pallas import tpu as pltpu

F32 = jnp.float32
BF16 = jnp.bfloat16

EPS = 1e-6
LOG2E = 1.4426950408889634
LANES = 128
FOX_HEAD_DIM = 64
N_FOX_HEADS = 8
D_FOX = FOX_HEAD_DIM * N_FOX_HEADS
GDN_HEAD_DIM = 128
N_GDN_HEADS = 4
D_GDN = GDN_HEAD_DIM * N_GDN_HEADS
CHUNK = 128
CONV_K = 4

FOX_REP = 6
SMALL_B0 = 64
SMALL_A0 = SMALL_B0 + N_GDN_HEADS

VEC_FBIAS, VEC_FNORM, VEC_ALOG, VEC_DTB, VEC_ONORM = range(5)

VMEM_LIMIT = 56 * 1024 * 1024


def _dot(a, b):
    return jnp.dot(a, b, preferred_element_type=F32)


def _dot_nt(a, b):
    return lax.dot_general(a, b, (((1,), (1,)), ((), ())), preferred_element_type=F32)


def _dot_tn(a, b):
    return lax.dot_general(a, b, (((0,), (0,)), ((), ())), preferred_element_type=F32)


def _rms(x, w):
    return x * lax.rsqrt(jnp.mean(x * x, axis=-1, keepdims=True) + EPS) * w


def _split3(x):
    hi = x.astype(BF16)
    r1 = x - hi.astype(F32)
    mid = r1.astype(BF16)
    lo = (r1 - mid.astype(F32)).astype(BF16)
    return hi, mid, lo


def _tril_ones(n, dtype, block=None):
    r = lax.broadcasted_iota(jnp.int32, (n, n), 0)
    c = lax.broadcasted_iota(jnp.int32, (n, n), 1)
    keep = c <= r
    if block is not None:
        sh = block.bit_length() - 1
        keep = keep & ((r >> sh) == (c >> sh))
    return jnp.where(keep, 1.0, 0.0).astype(dtype)


def _cumsum_rows(tril_bf16, x):
    hi, mid, lo = _split3(x)
    return _dot(tril_bf16, hi) + _dot(tril_bf16, mid) + _dot(tril_bf16, lo)


def _sigmoid(x):
    return 0.5 + 0.5 * jnp.tanh(0.5 * x)


def _silu(x):
    h = 0.5 * x
    return h + h * jnp.tanh(h)


def _softplus(x):
    return jnp.maximum(x, 0.0) + jnp.log1p(jnp.exp(-jnp.abs(x)))


def _in_proj_kernel(x_ref, g_ref, wt_ref, cw_ref, fox_ref, gdn_ref, small_ref,
                    w_ref, halo_ref, win_ref, *, tiles_per_seq):
    i = pl.program_id(0)
    d = GDN_HEAD_DIM
    g0 = 3 * D_FOX
    s0 = g0 + 4 * D_GDN

    @pl.when(i == 0)
    def _regroup_weights():
        fsrc = 3 * D_FOX
        gsrc = fsrc + N_FOX_HEADS
        bsrc = gsrc + 3 * D_GDN
        zsrc = bsrc + 2 * N_GDN_HEADS
        rb = 256

        def copy_rows(dst, src, count, scale=None):
            for r in range(0, count, rb):
                v = wt_ref[src + r:src + r + rb, :]
                if scale is not None:
                    v = v * scale
                w_ref[dst + r:dst + r + rb, :] = v.astype(BF16)

        copy_rows(0, 0, D_FOX, FOX_HEAD_DIM ** -0.5 * LOG2E)
        copy_rows(D_FOX, D_FOX, 2 * D_FOX)
        copy_rows(g0, gsrc, 3 * D_GDN)
        copy_rows(g0 + 3 * D_GDN, zsrc, D_GDN)
        ff = wt_ref[fsrc:fsrc + N_FOX_HEADS, :]
        ba = wt_ref[bsrc:bsrc + 2 * N_GDN_HEADS, :]
        zero8 = jnp.zeros_like(ff)
        pad_b = (SMALL_B0 - FOX_REP * N_FOX_HEADS) // 8
        pad_end = (LANES - SMALL_A0 - N_GDN_HEADS) // 8
        small = jnp.concatenate([ff] * FOX_REP + [zero8] * pad_b + [ba] + [zero8] * pad_end, axis=0)
        w_ref[s0:, :] = small.astype(BF16)

    h = _rms(x_ref[...], g_ref[...]).astype(BF16)
    tm = h.shape[0]
    first = i % tiles_per_seq == 0
    slab = 2 * d
    for sl in range(4 * D_GDN // slab):
        cols = slice(sl * slab, (sl + 1) * slab)
        if sl * slab < 3 * D_FOX:
            fox_ref[:, cols] = _dot_nt(h, w_ref[cols, :]).astype(BF16)
        g = _dot_nt(h, w_ref[g0 + sl * slab:g0 + (sl + 1) * slab, :])
        if sl * slab >= 3 * D_GDN:
            gdn_ref[:, cols] = _silu(g).astype(BF16)
            continue
        win_ref[0:8, :] = jnp.where(first, 0.0, halo_ref[:, cols])
        win_ref[8:, :] = g
        halo_ref[:, cols] = g[tm - 8:, :]
        w = cw_ref[:, cols]
        y = g * w[CONV_K - 1:CONV_K, :]
        for j in range(1, CONV_K):
            shifted = win_ref[8 - j:8 - j + tm, :]
            y = y + shifted * w[CONV_K - 1 - j:CONV_K - j, :]
        act = _silu(y)
        if sl * slab >= 2 * D_GDN:
            gdn_ref[:, cols] = act.astype(BF16)
            continue
        for hd in range(slab // d):
            t = act[:, hd * d:(hd + 1) * d]
            r = lax.rsqrt(jnp.sum(t * t, axis=-1, keepdims=True) + EPS)
            if sl * slab < D_GDN:
                r = r * (d ** -0.5)
            gdn_ref[:, sl * slab + hd * d:sl * slab + (hd + 1) * d] = (t * r).astype(BF16)
    small_ref[...] = _dot_nt(h, w_ref[s0:, :])


def _in_proj(x2, gain, w_t, conv_w, seq, tm):
    n, d = x2.shape
    const = lambda i: (0, 0)
    row = lambda i: (i, 0)
    kern = functools.partial(_in_proj_kernel, tiles_per_seq=seq // tm)
    return pl.pallas_call(
        kern,
        grid=(n // tm,),
        in_specs=[
            pl.BlockSpec((tm, d), row),
            pl.BlockSpec((1, d), const),
            pl.BlockSpec(w_t.shape, const, pipeline_mode=pl.Buffered(1)),
            pl.BlockSpec(conv_w.shape, const),
        ],
        out_specs=[
            pl.BlockSpec((tm, 3 * D_FOX), row),
            pl.BlockSpec((tm, 4 * D_GDN), row),
            pl.BlockSpec((tm, LANES), row),
        ],
        out_shape=[
            jax.ShapeDtypeStruct((n, 3 * D_FOX), BF16),
            jax.ShapeDtypeStruct((n, 4 * D_GDN), BF16),
            jax.ShapeDtypeStruct((n, LANES), F32),
        ],
        scratch_shapes=[pltpu.VMEM((3 * D_FOX + 4 * D_GDN + LANES, d), BF16),
                        pltpu.VMEM((8, 3 * D_GDN), F32),
                        pltpu.VMEM((tm + 8, 2 * GDN_HEAD_DIM), F32)],
        compiler_params=pltpu.CompilerParams(
            dimension_semantics=("arbitrary",), vmem_limit_bytes=VMEM_LIMIT),
        name="in_proj",
    )(x2, gain, w_t, conv_w)


def _fox_kernel(q_ref, k_ref, v_ref, small_ref, vec_ref, *rest, seq, tq, tk, dg, cum_blk, n_cast):
    cast_in = rest[:n_cast]
    o_ref = rest[n_cast]
    cast_out = rest[n_cast + 1:2 * n_cast + 1]
    qx_ref, kx_ref, vaug_ref = rest[2 * n_cast + 1:]
    for src, dst in zip(cast_in, cast_out):
        dst[...] = src[...].astype(dst.dtype)
    p = pl.program_id(1)
    lane = lax.broadcasted_iota(jnp.int32, (1, LANES), 1)
    half = FOX_HEAD_DIM
    nh = N_FOX_HEADS
    head_mask = (lane < half, lane >= half)
    ones_lane = (half, 0)
    zero = jnp.zeros((), BF16)

    @pl.when(p == 0)
    def _per_sequence():
        gr = 16
        gi = lax.broadcasted_iota(jnp.int32, (gr, LANES), 0)
        gl = lax.broadcasted_iota(jnp.int32, (gr, LANES), 1)
        onehot = (gi == gl) & (gi < nh)
        sel = jnp.where(onehot, 1.0, 0.0).astype(BF16)
        bias_t = jnp.sum(jnp.where(onehot, vec_ref[VEC_FBIAS:VEC_FBIAS + 1, :], 0.0),
                         axis=-1, keepdims=True)
        tj = lax.broadcasted_iota(jnp.int32, (cum_blk, cum_blk), 0)
        ti = lax.broadcasted_iota(jnp.int32, (cum_blk, cum_blk), 1)
        triu = jnp.where(tj <= ti, 1.0, 0.0).astype(BF16)
        pr = lax.broadcasted_iota(jnp.int32, (4 * gr, LANES), 0)
        pc = lax.broadcasted_iota(jnp.int32, (4 * gr, LANES), 1)
        grp, g = pr // gr, pr % gr
        piece = (grp < 3) & (g < nh)
        ones_row = (grp == 3) & (g == 0)
        pq = jnp.where(piece & (pc == grp * nh + g), 1.0, jnp.where(
            ones_row & (pc >= 3 * nh) & (pc < 6 * nh), 1.0, 0.0)).astype(BF16)
        pk = jnp.where(piece & (pc == (3 + grp) * nh + g), -1.0, jnp.where(
            ones_row & (pc < 3 * nh), 1.0, 0.0)).astype(BF16)
        ones_t = jnp.ones((gr, cum_blk), BF16)
        n_blk = seq // cum_blk
        zts = []
        for blk in range(n_blk):
            z = small_ref[blk * cum_blk:(blk + 1) * cum_blk, :]
            zh = z.astype(BF16)
            zm = (z - zh.astype(F32)).astype(BF16)
            zts.append(_dot_nt(sel, zh) + _dot_nt(sel, zm))
        parts = [_split3(-_softplus(-(zt + bias_t))) for zt in zts]
        locs = [_dot(hi, triu) + _dot(mid, triu) + _dot(lo, triu) for hi, mid, lo in parts]
        totals = [loc[:, cum_blk - 1:cum_blk] for loc in locs]
        carry = jnp.zeros((gr, 1), F32)
        stacks = []
        for loc, tot in zip(locs, totals):
            c = loc + carry
            carry = carry + tot
            stacks.append(jnp.concatenate(list(_split3(c * LOG2E)) + [ones_t], axis=0))
        for blk, pieces in enumerate(stacks):
            rows = slice(blk * cum_blk, (blk + 1) * cum_blk)
            qx_ref[rows, :] = _dot_tn(pieces, pq).astype(BF16)
            kx_ref[rows, :] = _dot_tn(pieces, pk).astype(BF16)

    vp = v_ref[...]
    for e in range(2):
        ones_col = jnp.where(lane == ones_lane[e], 1.0, 0.0).astype(BF16)
        vaug_ref[e] = jnp.where(head_mask[e], vp, ones_col)

    def block(qaug, j0, width, row_lo, carry, masked):
        kb = jnp.concatenate([k_ref[pl.ds(j0, width), :], kx_ref[pl.ds(j0, width), :]], axis=1)
        nrows = tq - row_lo
        out = []
        for e in range(2):
            m_all, acc_all = carry[e]
            m, acc = m_all[row_lo:], acc_all[row_lo:]
            vb = vaug_ref[e, pl.ds(j0, width), :]
            s = _dot_nt(qaug[e][row_lo:], kb)
            if masked:
                rr = lax.broadcasted_iota(jnp.int32, (width, width), 0)
                cc = lax.broadcasted_iota(jnp.int32, (width, width), 1)
                top = jnp.where(cc <= rr, s[:width], -jnp.inf)
                s = top if nrows == width else jnp.concatenate([top, s[width:]], axis=0)
            m_new = jnp.maximum(m, jnp.max(s, axis=-1, keepdims=True))
            alpha = jnp.exp2(m - m_new)
            pexp = jnp.exp2(s - m_new)
            acc = alpha * acc + _dot(pexp.astype(BF16), vb)
            if row_lo:
                m_new = jnp.concatenate([m_all[:row_lo], m_new], axis=0)
                acc = jnp.concatenate([acc_all[:row_lo], acc], axis=0)
            out.append((m_new, acc))
        return tuple(out)

    for qi in range(seq // tq):
        r0 = qi * tq
        qs = q_ref[r0:r0 + tq, :]
        qx = qx_ref[r0:r0 + tq, :]
        qaug = []
        for e in range(2):
            xmask = ((lane & (nh - 1)) == 2 * p + e) & (lane < 6 * nh)
            qaug.append(jnp.concatenate(
                [jnp.where(head_mask[e], qs, zero), jnp.where(xmask, qx, zero)], axis=1))
        carry = tuple((jnp.full((tq, 1), -jnp.inf, F32), jnp.zeros((tq, LANES), F32))
                      for _ in range(2))
        for j in range(r0 // tk):
            carry = block(qaug, j * tk, tk, 0, carry, False)
        for c in range(tq // dg):
            carry = block(qaug, r0 + c * dg, dg, c * dg, carry, True)

        outs = []
        for e in range(2):
            _, acc = carry[e]
            l = jnp.sum(jnp.where(lane == ones_lane[e], acc, 0.0), axis=-1, keepdims=True)
            outs.append(acc / l)
        o = jnp.where(head_mask[0], outs[0], outs[1])
        o2 = o * o
        ss0 = jnp.sum(jnp.where(head_mask[0], o2, 0.0), axis=-1, keepdims=True)
        ss1 = jnp.sum(jnp.where(head_mask[1], o2, 0.0), axis=-1, keepdims=True)
        ms = jnp.where(head_mask[0], ss0, ss1) * (1.0 / FOX_HEAD_DIM)
        o_ref[r0:r0 + tq, :] = (o * lax.rsqrt(ms + EPS)
                                * vec_ref[VEC_FNORM:VEC_FNORM + 1, :]).astype(o_ref.dtype)


def _fox(fox_qkv, small, vecs, cast_weights, batch, seq, tq, tk, dg):
    n = batch * seq
    pairs = N_FOX_HEADS // 2
    steps = batch * pairs
    kern = functools.partial(_fox_kernel, seq=seq, tq=tq, tk=tk, dg=dg, cum_blk=256,
                             n_cast=len(cast_weights))
    slab = lambda b, p: (b * pairs + p, 0)
    cast_specs = []
    for w in cast_weights:
        rows = w.shape[0] // steps
        assert rows * steps == w.shape[0] and rows % 16 == 0, w.shape
        cast_specs.append(pl.BlockSpec((rows, w.shape[1]), slab))
    outs = pl.pallas_call(
        kern,
        grid=(batch, pairs),
        in_specs=[
            pl.BlockSpec((seq, LANES), lambda b, p: (b, p)),
            pl.BlockSpec((seq, LANES), lambda b, p: (b, pairs + p)),
            pl.BlockSpec((seq, LANES), lambda b, p: (b, 2 * pairs + p)),
            pl.BlockSpec((seq, LANES), lambda b, p: (b, 0)),
            pl.BlockSpec(vecs.shape, lambda b, p: (0, 0)),
        ] + cast_specs,
        out_specs=[pl.BlockSpec((seq, LANES), lambda b, p: (b, p))] + cast_specs,
        out_shape=[jax.ShapeDtypeStruct((n, D_FOX), BF16)]
        + [jax.ShapeDtypeStruct(w.shape, BF16) for w in cast_weights],
        scratch_shapes=[
            pltpu.VMEM((seq, LANES), BF16),
            pltpu.VMEM((seq, LANES), BF16),
            pltpu.VMEM((2, seq, LANES), BF16),
        ],
        compiler_params=pltpu.CompilerParams(
            dimension_semantics=("arbitrary", "arbitrary"),
            vmem_limit_bytes=VMEM_LIMIT),
        name="fox_attention",
    )(fox_qkv, fox_qkv, fox_qkv, small, vecs, *cast_weights)
    return outs[0], outs[1:]


def _merge_levels(xs, m_lows, sizes):
    c = m_lows[0].shape[0]
    row = lax.broadcasted_iota(jnp.int32, (c, c), 0)
    col = lax.broadcasted_iota(jnp.int32, (c, c), 1)
    for size in sizes:
        sh = size.bit_length() - 1
        rb = row >> sh
        below = ((rb & 1) == 1) & ((col >> sh) == rb - 1)
        xbfs = [x.astype(BF16) for x in xs]
        nxs = [_dot(jnp.where(below, m, 0.0).astype(BF16), xb).astype(BF16)
               for m, xb in zip(m_lows, xbfs)]
        xs = [x - _dot(xb, nx) for x, xb, nx in zip(xs, xbfs, nxs)]
        yield
    return xs


MERGE_SIZES = (2, 4, 8, 16, 32, 64)


def _gdn_kernel(q_ref, k_ref, v_ref, gate_ref, small_ref, vec_ref, o_ref,
                w2_ref, rr_ref, qp_ref, op_ref, gl_ref, st_ref, *, seq, heads, group):
    c = CHUNK
    d = GDN_HEAD_DIM
    rows = group * c
    n_groups = seq // rows
    lane = lax.broadcasted_iota(jnp.int32, (1, LANES), 1)
    row = lax.broadcasted_iota(jnp.int32, (c, c), 0)
    col = lax.broadcasted_iota(jnp.int32, (c, c), 1)
    tril = _tril_ones(c, BF16)

    def aligned(x, m):
        return x if isinstance(x, int) else pl.multiple_of(x, m)

    def prep_stages(gi):
        r0 = aligned(gi * rows, rows)
        qa = q_ref[pl.ds(r0, rows), :].astype(F32)
        ka = k_ref[pl.ds(r0, rows), :]
        va = v_ref[pl.ds(r0, rows), :].astype(F32)
        sm = small_ref[pl.ds(r0, rows), :]
        g_all = -jnp.exp(vec_ref[VEC_ALOG:VEC_ALOG + 1, :]) * _softplus(
            sm + vec_ref[VEC_DTB:VEC_DTB + 1, :])
        p_hi, p_mid, p_lo = _split3(g_all)
        gc_all = jnp.concatenate(
            [_dot(tril, p_hi[g * c:(g + 1) * c]) + _dot(tril, p_mid[g * c:(g + 1) * c])
             + _dot(tril, p_lo[g * c:(g + 1) * c]) for g in range(group)], axis=0)
        items = []
        for h in range(heads):
            qh = qa[:, h * d:(h + 1) * d]
            kh = ka[:, h * d:(h + 1) * d]
            vh = va[:, h * d:(h + 1) * d]
            beta_h = _sigmoid(jnp.sum(jnp.where(lane == SMALL_B0 + h, sm, 0.0),
                                      axis=-1, keepdims=True))
            gc_h = jnp.sum(jnp.where(lane == SMALL_A0 + h, gc_all, 0.0),
                           axis=-1, keepdims=True)
            for g in range(group):
                sl = slice(g * c, (g + 1) * c)
                items.append(dict(h=h, g=g, q=qh[sl], k=kh[sl].astype(F32), kbf=kh[sl], v=vh[sl],
                                  beta=beta_h[sl], gc=gc_h[sl]))
        for it in items:
            gc = it["gc"]
            hi, mid, lo = (t.astype(F32) for t in _split3(gc))
            it["al"] = jnp.where(lane == 0, hi, jnp.where(lane == 1, mid, jnp.where(
                lane == 2, lo, jnp.where(lane < 6, 1.0, 0.0)))).astype(BF16)
            it["ar"] = jnp.where(lane < 3, 1.0, jnp.where(lane == 3, -hi, jnp.where(
                lane == 4, -mid, jnp.where(lane == 5, -lo, 0.0)))).astype(BF16)
            it["kb"] = it["k"] * it["beta"]
        dmats = [_dot_nt(it["al"], it["ar"]) for it in items]
        kks = [_dot_nt(it["kb"].astype(BF16), it["kbf"]) for it in items]
        qks = [_dot_nt(it["q"].astype(BF16), it["kbf"]) for it in items]
        m_lows = []
        for it, dmat, kk, qk in zip(items, dmats, kks, qks):
            decay = jnp.exp(jnp.where(col <= row, dmat, -jnp.inf))
            m_lows.append(jnp.where(col < row, kk * decay, 0.0))
            it["a"] = (qk * decay).astype(BF16)
        yield
        eye = jnp.where(row == col, 1.0, 0.0)
        pair = (row >> 1) == (col >> 1)
        xs = [eye - jnp.where(pair, m, 0.0) for m in m_lows]
        t_invs = yield from _merge_levels(xs, m_lows, MERGE_SIZES)
        wus = []
        for it, t_inv in zip(items, t_invs):
            eg = jnp.exp(it["gc"])
            it["eg"] = eg
            rhs = jnp.concatenate(
                [(it["kb"] * eg).astype(BF16), (it["v"] * it["beta"]).astype(BF16)], axis=1)
            wus.append(_dot(t_inv.astype(BF16), rhs).astype(BF16))
        yield
        kwus, awus = [], []
        for it, wu in zip(items, wus):
            gc_last = it["gc"][c - 1:c, :]
            it["gl"] = jnp.exp(gc_last)
            kd = (it["k"] * jnp.exp(gc_last - it["gc"])).astype(BF16)
            kwus.append(_dot_tn(kd, wu))
            awus.append(_dot(it["a"], wu))
        for it, kwu, awu in zip(items, kwus, awus):
            h = it["h"]
            ci = gi * group + it["g"]
            rw = aligned(ci * d, d)
            rc = aligned(ci * c, c)
            w2_ref[h, pl.ds(rw, d), :] = kwu[:, :d].astype(BF16)
            rr_ref[h, pl.ds(rw, d), :] = kwu[:, d:]
            qp_ref[h, pl.ds(rc, c), :] = (it["q"] * it["eg"] - awu[:, :d]).astype(BF16)
            op_ref[h, pl.ds(rc, c), :] = awu[:, d:]
            gl_ref[h, pl.ds(aligned(ci * 8, 8), 8), :] = jnp.broadcast_to(it["gl"], (8, d))

    def scan_chunk(ci):
        rc = aligned(ci * c, c)
        rw = aligned(ci * d, d)
        for h in range(heads):
            state = st_ref[h]
            lhs = jnp.concatenate([w2_ref[h, pl.ds(rw, d), :], qp_ref[h, pl.ds(rc, c), :]], axis=0)
            r = _dot(lhs, state.astype(BF16))
            gl = gl_ref[h, pl.ds(aligned(ci * 8, 8), 8), :][0:1, :]
            st_ref[h] = state * gl - r[:d] + rr_ref[h, pl.ds(rw, d), :]
            o = r[d:] + op_ref[h, pl.ds(rc, c), :]
            gate = gate_ref[pl.ds(rc, c), h * d:(h + 1) * d].astype(F32)
            o_ref[pl.ds(rc, c), h * d:(h + 1) * d] = (
                _rms(o, vec_ref[VEC_ONORM:VEC_ONORM + 1, :]) * gate).astype(o_ref.dtype)

    def prep_and_scan(gi, scan_group):
        pending = [] if scan_group is None else [scan_group * group + j for j in range(group)]
        for _ in prep_stages(gi):
            if pending:
                scan_chunk(pending.pop(0))
        for ci in pending:
            scan_chunk(ci)

    st_ref[...] = jnp.zeros_like(st_ref)
    prep_and_scan(0, None)

    def body(gi, _):
        prep_and_scan(gi, gi - 1)
        return 0

    lax.fori_loop(1, n_groups, body, 0)
    for j in range(group):
        scan_chunk((n_groups - 1) * group + j)


def _gdn(gdn_in, small, vecs, batch, seq, group):
    n = batch * seq
    d = GDN_HEAD_DIM
    heads = N_GDN_HEADS
    n_chunks = seq // CHUNK
    kern = functools.partial(_gdn_kernel, seq=seq, heads=heads, group=group)
    col_block = lambda part: (lambda b: (b, part))
    const = lambda b: (0, 0)
    return pl.pallas_call(
        kern,
        grid=(batch,),
        in_specs=[
            pl.BlockSpec((seq, D_GDN), col_block(0)),
            pl.BlockSpec((seq, D_GDN), col_block(1)),
            pl.BlockSpec((seq, D_GDN), col_block(2)),
            pl.BlockSpec((seq, D_GDN), col_block(3)),
            pl.BlockSpec((seq, LANES), lambda b: (b, 0)),
            pl.BlockSpec(vecs.shape, const),
        ],
        out_specs=pl.BlockSpec((seq, D_GDN), lambda b: (b, 0)),
        out_shape=jax.ShapeDtypeStruct((n, D_GDN), BF16),
        scratch_shapes=[
            pltpu.VMEM((heads, n_chunks * d, d), BF16),
            pltpu.VMEM((heads, n_chunks * d, d), F32),
            pltpu.VMEM((heads, seq, d), BF16),
            pltpu.VMEM((heads, seq, d), F32),
            pltpu.VMEM((heads, n_chunks * 8, d), F32),
            pltpu.VMEM((heads, d, d), F32),
        ],
        compiler_params=pltpu.CompilerParams(
            dimension_semantics=("arbitrary",), vmem_limit_bytes=VMEM_LIMIT),
        name="gated_deltanet",
    )(gdn_in, gdn_in, gdn_in, gdn_in, small, vecs)


def _out_mlp_kernel(fox_ref, gdn_ref, x_ref, wo_ref, pmix_ref, pre_ref,
                    wup_ref, wdn_ref, post_ref, o_ref, *, ff_blk):
    mixed = _dot(fox_ref[...], wo_ref[:D_FOX, :]) + _dot(gdn_ref[...], wo_ref[D_FOX:, :])
    x1 = x_ref[...] + _rms(mixed, pmix_ref[...])
    h = _rms(x1, pre_ref[...]).astype(BF16)
    d_ff = wup_ref.shape[1]
    y = None
    for j in range(d_ff // ff_blk):
        a = _dot(h, wup_ref[:, j * ff_blk:(j + 1) * ff_blk])
        a = jnp.square(jnp.maximum(a, 0.0)).astype(BF16)
        t = _dot(a, wdn_ref[j * ff_blk:(j + 1) * ff_blk, :])
        y = t if y is None else y + t
    o_ref[...] = x1 + _rms(y, post_ref[...])


def _out_mlp(fox_o, gdn_o, x2, wo, pmix, pre, wup, wdn, post, tm):
    n, d = x2.shape
    const = lambda i: (0, 0)
    row = lambda i: (i, 0)
    single = pl.Buffered(1)
    kern = functools.partial(_out_mlp_kernel, ff_blk=1024)
    return pl.pallas_call(
        kern,
        grid=(n // tm,),
        in_specs=[
            pl.BlockSpec((tm, fox_o.shape[1]), row),
            pl.BlockSpec((tm, gdn_o.shape[1]), row),
            pl.BlockSpec((tm, d), row),
            pl.BlockSpec(wo.shape, const, pipeline_mode=single),
            pl.BlockSpec((1, d), const),
            pl.BlockSpec((1, d), const),
            pl.BlockSpec(wup.shape, const, pipeline_mode=single),
            pl.BlockSpec(wdn.shape, const, pipeline_mode=single),
            pl.BlockSpec((1, d), const),
        ],
        out_specs=pl.BlockSpec((tm, d), row),
        out_shape=jax.ShapeDtypeStruct((n, d), F32),
        compiler_params=pltpu.CompilerParams(
            dimension_semantics=("arbitrary",), vmem_limit_bytes=VMEM_LIMIT),
        name="out_mlp",
    )(fox_o, gdn_o, x2, wo, pmix, pre, wup, wdn, post)


def _vec_row(values):
    values = values.astype(F32)
    return jnp.concatenate([values, jnp.zeros((LANES - values.shape[0],), F32)])


def kernel(x, pre_mix_norm, w_in, fox_f_bias, fox_out_norm, gdn_conv_w, gdn_a_log, gdn_dt_bias,
           gdn_out_norm, w_out, post_mix_norm, pre_mlp_norm, w_up, w_down, post_mlp_norm):
    b, s, d = x.shape
    n = b * s
    x2 = x.reshape(n, d)


    zeros_b = jnp.zeros((SMALL_A0,), F32)
    vecs = jnp.stack([
        _vec_row(jnp.tile(fox_f_bias, FOX_REP)),
        _vec_row(jnp.tile(fox_out_norm, 2)),
        _vec_row(jnp.concatenate([zeros_b, gdn_a_log.astype(F32)])),
        _vec_row(jnp.concatenate([zeros_b, gdn_dt_bias.astype(F32)])),
        _vec_row(gdn_out_norm),
        jnp.zeros((LANES,), F32), jnp.zeros((LANES,), F32), jnp.zeros((LANES,), F32)])

    fox_qkv, gdn_in, small = _in_proj(x2, pre_mix_norm.reshape(1, d).astype(F32),
                                      w_in.astype(F32).T, gdn_conv_w.astype(F32), s, tm=1024)
    fox_o, (wo, wup, wdn) = _fox(
        fox_qkv, small, vecs, [w_out.astype(F32), w_up.astype(F32), w_down.astype(F32)],
        b, s, tq=1024, tk=512, dg=512)
    gdn_o = _gdn(gdn_in, small, vecs, b, s, group=4)
    out = _out_mlp(fox_o, gdn_o, x2, wo,
                   post_mix_norm.reshape(1, d).astype(F32), pre_mlp_norm.reshape(1, d).astype(F32),
                   wup, wdn, post_mlp_norm.reshape(1, d).astype(F32), tm=1024)
    return out.reshape(b, s, d)
```

```python
import functools

import jax
import jax.numpy as jnp
from jax import lax
from jax.experimental import pallas as pl
from jax.experimental.pallas import tpu as pltpu

F32 = jnp.float32
BF16 = jnp.bfloat16

EPS = 1e-6
LOG2E = 1.4426950408889634
LANES = 128
FOX_HEAD_DIM = 64
N_FOX_HEADS = 8
D_FOX = FOX_HEAD_DIM * N_FOX_HEADS
GDN_HEAD_DIM = 128
N_GDN_HEADS = 4
D_GDN = GDN_HEAD_DIM * N_GDN_HEADS
CHUNK = 128
CONV_K = 4

FOX_REP = 6
SMALL_B0 = 64
SMALL_A0 = SMALL_B0 + N_GDN_HEADS

VEC_FBIAS, VEC_FNORM, VEC_ALOG, VEC_DTB, VEC_ONORM = range(5)

VMEM_LIMIT = 56 * 1024 * 1024


def _dot(a, b):
    return jnp.dot(a, b, preferred_element_type=F32)


def _dot_nt(a, b):
    return lax.dot_general(a, b, (((1,), (1,)), ((), ())), preferred_element_type=F32)


def _dot_tn(a, b):
    return lax.dot_general(a, b, (((0,), (0,)), ((), ())), preferred_element_type=F32)


def _rms(x, w):
    return x * lax.rsqrt(jnp.mean(x * x, axis=-1, keepdims=True) + EPS) * w


def _split3(x):
    hi = x.astype(BF16)
    r1 = x - hi.astype(F32)
    mid = r1.astype(BF16)
    lo = (r1 - mid.astype(F32)).astype(BF16)
    return hi, mid, lo


def _tril_ones(n, dtype, block=None):
    r = lax.broadcasted_iota(jnp.int32, (n, n), 0)
    c = lax.broadcasted_iota(jnp.int32, (n, n), 1)
    keep = c <= r
    if block is not None:
        sh = block.bit_length() - 1
        keep = keep & ((r >> sh) == (c >> sh))
    return jnp.where(keep, 1.0, 0.0).astype(dtype)


def _cumsum_rows(tril_bf16, x):
    hi, mid, lo = _split3(x)
    return _dot(tril_bf16, hi) + _dot(tril_bf16, mid) + _dot(tril_bf16, lo)


def _sigmoid(x):
    return 0.5 + 0.5 * jnp.tanh(0.5 * x)


def _silu(x):
    h = 0.5 * x
    return h + h * jnp.tanh(h)


def _softplus(x):
    return jnp.maximum(x, 0.0) + jnp.log1p(jnp.exp(-jnp.abs(x)))


def _in_proj_kernel(x_ref, g_ref, wt_ref, cw_ref, fox_ref, gdn_ref, small_ref,
                    w_ref, halo_ref, win_ref, *, tiles_per_seq):
    i = pl.program_id(0)
    d = GDN_HEAD_DIM
    g0 = 3 * D_FOX
    s0 = g0 + 4 * D_GDN

    @pl.when(i == 0)
    def _regroup_weights():
        fsrc = 3 * D_FOX
        gsrc = fsrc + N_FOX_HEADS
        bsrc = gsrc + 3 * D_GDN
        zsrc = bsrc + 2 * N_GDN_HEADS
        rb = 256

        def copy_rows(dst, src, count, scale=None):
            for r in range(0, count, rb):
                v = wt_ref[src + r:src + r + rb, :]
                if scale is not None:
                    v = v * scale
                w_ref[dst + r:dst + r + rb, :] = v.astype(BF16)

        copy_rows(0, 0, D_FOX, FOX_HEAD_DIM ** -0.5 * LOG2E)
        copy_rows(D_FOX, D_FOX, 2 * D_FOX)
        copy_rows(g0, gsrc, 3 * D_GDN)
        copy_rows(g0 + 3 * D_GDN, zsrc, D_GDN)
        ff = wt_ref[fsrc:fsrc + N_FOX_HEADS, :]
        ba = wt_ref[bsrc:bsrc + 2 * N_GDN_HEADS, :]
        zero8 = jnp.zeros_like(ff)
        pad_b = (SMALL_B0 - FOX_REP * N_FOX_HEADS) // 8
        pad_end = (LANES - SMALL_A0 - N_GDN_HEADS) // 8
        small = jnp.concatenate([ff] * FOX_REP + [zero8] * pad_b + [ba] + [zero8] * pad_end, axis=0)
        w_ref[s0:, :] = small.astype(BF16)

    h = _rms(x_ref[...], g_ref[...]).astype(BF16)
    tm = h.shape[0]
    first = i % tiles_per_seq == 0
    slab = 2 * d
    for sl in range(4 * D_GDN // slab):
        cols = slice(sl * slab, (sl + 1) * slab)
        if sl * slab < 3 * D_FOX:
            fox_ref[:, cols] = _dot_nt(h, w_ref[cols, :]).astype(BF16)
        g = _dot_nt(h, w_ref[g0 + sl * slab:g0 + (sl + 1) * slab, :])
        if sl * slab >= 3 * D_GDN:
            gdn_ref[:, cols] = _silu(g).astype(BF16)
            continue
        win_ref[0:8, :] = jnp.where(first, 0.0, halo_ref[:, cols])
        win_ref[8:, :] = g
        halo_ref[:, cols] = g[tm - 8:, :]
        w = cw_ref[:, cols]
        y = g * w[CONV_K - 1:CONV_K, :]
        for j in range(1, CONV_K):
            shifted = win_ref[8 - j:8 - j + tm, :]
            y = y + shifted * w[CONV_K - 1 - j:CONV_K - j, :]
        act = _silu(y)
        if sl * slab >= 2 * D_GDN:
            gdn_ref[:, cols] = act.astype(BF16)
            continue
        for hd in range(slab // d):
            t = act[:, hd * d:(hd + 1) * d]
            r = lax.rsqrt(jnp.sum(t * t, axis=-1, keepdims=True) + EPS)
            if sl * slab < D_GDN:
                r = r * (d ** -0.5)
            gdn_ref[:, sl * slab + hd * d:sl * slab + (hd + 1) * d] = (t * r).astype(BF16)
    small_ref[...] = _dot_nt(h, w_ref[s0:, :])


def _in_proj(x2, gain, w_t, conv_w, seq, tm):
    n, d = x2.shape
    const = lambda i: (0, 0)
    row = lambda i: (i, 0)
    kern = functools.partial(_in_proj_kernel, tiles_per_seq=seq // tm)
    return pl.pallas_call(
        kern,
        grid=(n // tm,),
        in_specs=[
            pl.BlockSpec((tm, d), row),
            pl.BlockSpec((1, d), const),
            pl.BlockSpec(w_t.shape, const, pipeline_mode=pl.Buffered(1)),
            pl.BlockSpec(conv_w.shape, const),
        ],
        out_specs=[
            pl.BlockSpec((tm, 3 * D_FOX), row),
            pl.BlockSpec((tm, 4 * D_GDN), row),
            pl.BlockSpec((tm, LANES), row),
        ],
        out_shape=[
            jax.ShapeDtypeStruct((n, 3 * D_FOX), BF16),
            jax.ShapeDtypeStruct((n, 4 * D_GDN), BF16),
            jax.ShapeDtypeStruct((n, LANES), F32),
        ],
        scratch_shapes=[pltpu.VMEM((3 * D_FOX + 4 * D_GDN + LANES, d), BF16),
                        pltpu.VMEM((8, 3 * D_GDN), F32),
                        pltpu.VMEM((tm + 8, 2 * GDN_HEAD_DIM), F32)],
        compiler_params=pltpu.CompilerParams(
            dimension_semantics=("arbitrary",), vmem_limit_bytes=VMEM_LIMIT),
        name="in_proj",
    )(x2, gain, w_t, conv_w)


def _fox_kernel(q_ref, k_ref, v_ref, small_ref, vec_ref, *rest, seq, tq, tk, dg, cum_blk, n_cast):
    cast_in = rest[:n_cast]
    o_ref = rest[n_cast]
    cast_out = rest[n_cast + 1:2 * n_cast + 1]
    qx_ref, kx_ref, vaug_ref = rest[2 * n_cast + 1:]
    for src, dst in zip(cast_in, cast_out):
        dst[...] = src[...].astype(dst.dtype)
    p = pl.program_id(1)
    lane = lax.broadcasted_iota(jnp.int32, (1, LANES), 1)
    half = FOX_HEAD_DIM
    nh = N_FOX_HEADS
    head_mask = (lane < half, lane >= half)
    ones_lane = (half, 0)
    zero = jnp.zeros((), BF16)

    @pl.when(p == 0)
    def _per_sequence():
        gr = 16
        gi = lax.broadcasted_iota(jnp.int32, (gr, LANES), 0)
        gl = lax.broadcasted_iota(jnp.int32, (gr, LANES), 1)
        onehot = (gi == gl) & (gi < nh)
        sel = jnp.where(onehot, 1.0, 0.0).astype(BF16)
        bias_t = jnp.sum(jnp.where(onehot, vec_ref[VEC_FBIAS:VEC_FBIAS + 1, :], 0.0),
                         axis=-1, keepdims=True)
        tj = lax.broadcasted_iota(jnp.int32, (cum_blk, cum_blk), 0)
        ti = lax.broadcasted_iota(jnp.int32, (cum_blk, cum_blk), 1)
        triu = jnp.where(tj <= ti, 1.0, 0.0).astype(BF16)
        pr = lax.broadcasted_iota(jnp.int32, (4 * gr, LANES), 0)
        pc = lax.broadcasted_iota(jnp.int32, (4 * gr, LANES), 1)
        grp, g = pr // gr, pr % gr
        piece = (grp < 3) & (g < nh)
        ones_row = (grp == 3) & (g == 0)
        pq = jnp.where(piece & (pc == grp * nh + g), 1.0, jnp.where(
            ones_row & (pc >= 3 * nh) & (pc < 6 * nh), 1.0, 0.0)).astype(BF16)
        pk = jnp.where(piece & (pc == (3 + grp) * nh + g), -1.0, jnp.where(
            ones_row & (pc < 3 * nh), 1.0, 0.0)).astype(BF16)
        ones_t = jnp.ones((gr, cum_blk), BF16)
        n_blk = seq // cum_blk
        zts = []
        for blk in range(n_blk):
            z = small_ref[blk * cum_blk:(blk + 1) * cum_blk, :]
            zh = z.astype(BF16)
            zm = (z - zh.astype(F32)).astype(BF16)
            zts.append(_dot_nt(sel, zh) + _dot_nt(sel, zm))
        parts = [_split3(-_softplus(-(zt + bias_t))) for zt in zts]
        locs = [_dot(hi, triu) + _dot(mid, triu) + _dot(lo, triu) for hi, mid, lo in parts]
        totals = [loc[:, cum_blk - 1:cum_blk] for loc in locs]
        carry = jnp.zeros((gr, 1), F32)
        stacks = []
        for loc, tot in zip(locs, totals):
            c = loc + carry
            carry = carry + tot
            stacks.append(jnp.concatenate(list(_split3(c * LOG2E)) + [ones_t], axis=0))
        for blk, pieces in enumerate(stacks):
            rows = slice(blk * cum_blk, (blk + 1) * cum_blk)
            qx_ref[rows, :] = _dot_tn(pieces, pq).astype(BF16)
            kx_ref[rows, :] = _dot_tn(pieces, pk).astype(BF16)

    vp = v_ref[...]
    for e in range(2):
        ones_col = jnp.where(lane == ones_lane[e], 1.0, 0.0).astype(BF16)
        vaug_ref[e] = jnp.where(head_mask[e], vp, ones_col)

    def block(qaug, j0, width, row_lo, carry, masked):
        kb = jnp.concatenate([k_ref[pl.ds(j0, width), :], kx_ref[pl.ds(j0, width), :]], axis=1)
        nrows = tq - row_lo
        out = []
        for e in range(2):
            m_all, acc_all = carry[e]
            m, acc = m_all[row_lo:], acc_all[row_lo:]
            vb = vaug_ref[e, pl.ds(j0, width), :]
            s = _dot_nt(qaug[e][row_lo:], kb)
            if masked:
                rr = lax.broadcasted_iota(jnp.int32, (width, width), 0)
                cc = lax.broadcasted_iota(jnp.int32, (width, width), 1)
                top = jnp.where(cc <= rr, s[:width], -jnp.inf)
                s = top if nrows == width else jnp.concatenate([top, s[width:]], axis=0)
            m_new = jnp.maximum(m, jnp.max(s, axis=-1, keepdims=True))
            alpha = jnp.exp2(m - m_new)
            pexp = jnp.exp2(s - m_new)
            acc = alpha * acc + _dot(pexp.astype(BF16), vb)
            if row_lo:
                m_new = jnp.concatenate([m_all[:row_lo], m_new], axis=0)
                acc = jnp.concatenate([acc_all[:row_lo], acc], axis=0)
            out.append((m_new, acc))
        return tuple(out)

    for qi in range(seq // tq):
        r0 = qi * tq
        qs = q_ref[r0:r0 + tq, :]
        qx = qx_ref[r0:r0 + tq, :]
        qaug = []
        for e in range(2):
            xmask = ((lane & (nh - 1)) == 2 * p + e) & (lane < 6 * nh)
            qaug.append(jnp.concatenate(
                [jnp.where(head_mask[e], qs, zero), jnp.where(xmask, qx, zero)], axis=1))
        carry = tuple((jnp.full((tq, 1), -jnp.inf, F32), jnp.zeros((tq, LANES), F32))
                      for _ in range(2))
        for j in range(r0 // tk):
            carry = block(qaug, j * tk, tk, 0, carry, False)
        for c in range(tq // dg):
            carry = block(qaug, r0 + c * dg, dg, c * dg, carry, True)

        outs = []
        for e in range(2):
            _, acc = carry[e]
            l = jnp.sum(jnp.where(lane == ones_lane[e], acc, 0.0), axis=-1, keepdims=True)
            outs.append(acc / l)
        o = jnp.where(head_mask[0], outs[0], outs[1])
        o2 = o * o
        ss0 = jnp.sum(jnp.where(head_mask[0], o2, 0.0), axis=-1, keepdims=True)
        ss1 = jnp.sum(jnp.where(head_mask[1], o2, 0.0), axis=-1, keepdims=True)
        ms = jnp.where(head_mask[0], ss0, ss1) * (1.0 / FOX_HEAD_DIM)
        o_ref[r0:r0 + tq, :] = (o * lax.rsqrt(ms + EPS)
                                * vec_ref[VEC_FNORM:VEC_FNORM + 1, :]).astype(o_ref.dtype)


def _fox(fox_qkv, small, vecs, cast_weights, batch, seq, tq, tk, dg):
    n = batch * seq
    pairs = N_FOX_HEADS // 2
    steps = batch * pairs
    kern = functools.partial(_fox_kernel, seq=seq, tq=tq, tk=tk, dg=dg, cum_blk=256,
                             n_cast=len(cast_weights))
    slab = lambda b, p: (b * pairs + p, 0)
    cast_specs = []
    for w in cast_weights:
        rows = w.shape[0] // steps
        assert rows * steps == w.shape[0] and rows % 16 == 0, w.shape
        cast_specs.append(pl.BlockSpec((rows, w.shape[1]), slab))
    outs = pl.pallas_call(
        kern,
        grid=(batch, pairs),
        in_specs=[
            pl.BlockSpec((seq, LANES), lambda b, p: (b, p)),
            pl.BlockSpec((seq, LANES), lambda b, p: (b, pairs + p)),
            pl.BlockSpec((seq, LANES), lambda b, p: (b, 2 * pairs + p)),
            pl.BlockSpec((seq, LANES), lambda b, p: (b, 0)),
            pl.BlockSpec(vecs.shape, lambda b, p: (0, 0)),
        ] + cast_specs,
        out_specs=[pl.BlockSpec((seq, LANES), lambda b, p: (b, p))] + cast_specs,
        out_shape=[jax.ShapeDtypeStruct((n, D_FOX), BF16)]
        + [jax.ShapeDtypeStruct(w.shape, BF16) for w in cast_weights],
        scratch_shapes=[
            pltpu.VMEM((seq, LANES), BF16),
            pltpu.VMEM((seq, LANES), BF16),
            pltpu.VMEM((2, seq, LANES), BF16),
        ],
        compiler_params=pltpu.CompilerParams(
            dimension_semantics=("arbitrary", "arbitrary"),
            vmem_limit_bytes=VMEM_LIMIT),
        name="fox_attention",
    )(fox_qkv, fox_qkv, fox_qkv, small, vecs, *cast_weights)
    return outs[0], outs[1:]


def _merge_levels(xs, m_lows, sizes):
    c = m_lows[0].shape[0]
    row = lax.broadcasted_iota(jnp.int32, (c, c), 0)
    col = lax.broadcasted_iota(jnp.int32, (c, c), 1)
    for size in sizes:
        sh = size.bit_length() - 1
        rb = row >> sh
        below = ((rb & 1) == 1) & ((col >> sh) == rb - 1)
        xbfs = [x.astype(BF16) for x in xs]
        nxs = [_dot(jnp.where(below, m, 0.0).astype(BF16), xb).astype(BF16)
               for m, xb in zip(m_lows, xbfs)]
        xs = [x - _dot(xb, nx) for x, xb, nx in zip(xs, xbfs, nxs)]
        yield
    return xs


MERGE_SIZES = (2, 4, 8, 16, 32, 64)


def _gdn_kernel(q_ref, k_ref, v_ref, gate_ref, small_ref, vec_ref, o_ref,
                w2_ref, rr_ref, qp_ref, op_ref, gl_ref, st_ref, *, seq, heads, group):
    c = CHUNK
    d = GDN_HEAD_DIM
    rows = group * c
    n_groups = seq // rows
    lane = lax.broadcasted_iota(jnp.int32, (1, LANES), 1)
    row = lax.broadcasted_iota(jnp.int32, (c, c), 0)
    col = lax.broadcasted_iota(jnp.int32, (c, c), 1)
    tril = _tril_ones(c, BF16)

    def aligned(x, m):
        return x if isinstance(x, int) else pl.multiple_of(x, m)

    def prep_stages(gi):
        r0 = aligned(gi * rows, rows)
        qa = q_ref[pl.ds(r0, rows), :].astype(F32)
        ka = k_ref[pl.ds(r0, rows), :]
        va = v_ref[pl.ds(r0, rows), :].astype(F32)
        sm = small_ref[pl.ds(r0, rows), :]
        g_all = -jnp.exp(vec_ref[VEC_ALOG:VEC_ALOG + 1, :]) * _softplus(
            sm + vec_ref[VEC_DTB:VEC_DTB + 1, :])
        p_hi, p_mid, p_lo = _split3(g_all)
        gc_all = jnp.concatenate(
            [_dot(tril, p_hi[g * c:(g + 1) * c]) + _dot(tril, p_mid[g * c:(g + 1) * c])
             + _dot(tril, p_lo[g * c:(g + 1) * c]) for g in range(group)], axis=0)
        items = []
        for h in range(heads):
            qh = qa[:, h * d:(h + 1) * d]
            kh = ka[:, h * d:(h + 1) * d]
            vh = va[:, h * d:(h + 1) * d]
            beta_h = _sigmoid(jnp.sum(jnp.where(lane == SMALL_B0 + h, sm, 0.0),
                                      axis=-1, keepdims=True))
            gc_h = jnp.sum(jnp.where(lane == SMALL_A0 + h, gc_all, 0.0),
                           axis=-1, keepdims=True)
            for g in range(group):
                sl = slice(g * c, (g + 1) * c)
                items.append(dict(h=h, g=g, q=qh[sl], k=kh[sl].astype(F32), kbf=kh[sl], v=vh[sl],
                                  beta=beta_h[sl], gc=gc_h[sl]))
        for it in items:
            gc = it["gc"]
            hi, mid, lo = (t.astype(F32) for t in _split3(gc))
            it["al"] = jnp.where(lane == 0, hi, jnp.where(lane == 1, mid, jnp.where(
                lane == 2, lo, jnp.where(lane < 6, 1.0, 0.0)))).astype(BF16)
            it["ar"] = jnp.where(lane < 3, 1.0, jnp.where(lane == 3, -hi, jnp.where(
                lane == 4, -mid, jnp.where(lane == 5, -lo, 0.0)))).astype(BF16)
            it["kb"] = it["k"] * it["beta"]
        dmats = [_dot_nt(it["al"], it["ar"]) for it in items]
        kqs = [_dot_nt(jnp.concatenate([it["kb"].astype(BF16), it["q"].astype(BF16)], axis=0),
                       it["kbf"]) for it in items]
        kks = [kq[:c] for kq in kqs]
        qks = [kq[c:] for kq in kqs]
        m_lows = []
        for it, dmat, kk, qk in zip(items, dmats, kks, qks):
            decay = jnp.exp(jnp.where(col <= row, dmat, -jnp.inf))
            m_lows.append(jnp.where(col < row, kk * decay, 0.0))
            it["a"] = (qk * decay).astype(BF16)
        yield
        eye = jnp.where(row == col, 1.0, 0.0)
        pair = (row >> 1) == (col >> 1)
        xs = [eye - jnp.where(pair, m, 0.0) for m in m_lows]
        t_invs = yield from _merge_levels(xs, m_lows, MERGE_SIZES)
        wus = []
        for it, t_inv in zip(items, t_invs):
            eg = jnp.exp(it["gc"])
            it["eg"] = eg
            rhs = jnp.concatenate(
                [(it["kb"] * eg).astype(BF16), (it["v"] * it["beta"]).astype(BF16)], axis=1)
            wus.append(_dot(t_inv.astype(BF16), rhs).astype(BF16))
        yield
        kwus, awus = [], []
        for it, wu in zip(items, wus):
            gc_last = it["gc"][c - 1:c, :]
            it["gl"] = jnp.exp(gc_last)
            kd = (it["k"] * jnp.exp(gc_last - it["gc"])).astype(BF16)
            kwus.append(_dot_tn(kd, wu))
            awus.append(_dot(it["a"], wu))
        for it, kwu, awu in zip(items, kwus, awus):
            h = it["h"]
            ci = gi * group + it["g"]
            rw = aligned(ci * d, d)
            rc = aligned(ci * c, c)
            w2_ref[h, pl.ds(rw, d), :] = kwu[:, :d].astype(BF16)
            rr_ref[h, pl.ds(rw, d), :] = kwu[:, d:]
            qp_ref[h, pl.ds(rc, c), :] = (it["q"] * it["eg"] - awu[:, :d]).astype(BF16)
            op_ref[h, pl.ds(rc, c), :] = awu[:, d:]
            gl_ref[h, pl.ds(aligned(ci * 8, 8), 8), :] = jnp.broadcast_to(it["gl"], (8, d))

    def scan_chunk(ci):
        rc = aligned(ci * c, c)
        rw = aligned(ci * d, d)
        for h in range(heads):
            state = st_ref[h]
            lhs = jnp.concatenate([w2_ref[h, pl.ds(rw, d), :], qp_ref[h, pl.ds(rc, c), :]], axis=0)
            r = _dot(lhs, state.astype(BF16))
            gl = gl_ref[h, pl.ds(aligned(ci * 8, 8), 8), :][0:1, :]
            st_ref[h] = state * gl - r[:d] + rr_ref[h, pl.ds(rw, d), :]
            o = r[d:] + op_ref[h, pl.ds(rc, c), :]
            gate = gate_ref[pl.ds(rc, c), h * d:(h + 1) * d].astype(F32)
            o_ref[pl.ds(rc, c), h * d:(h + 1) * d] = (
                _rms(o, vec_ref[VEC_ONORM:VEC_ONORM + 1, :]) * gate).astype(o_ref.dtype)

    def prep_and_scan(gi, scan_group):
        pending = [] if scan_group is None else [scan_group * group + j for j in range(group)]
        for _ in prep_stages(gi):
            if pending:
                scan_chunk(pending.pop(0))
        for ci in pending:
            scan_chunk(ci)

    st_ref[...] = jnp.zeros_like(st_ref)
    prep_and_scan(0, None)

    def body(gi, _):
        prep_and_scan(gi, gi - 1)
        return 0

    lax.fori_loop(1, n_groups, body, 0)
    for j in range(group):
        scan_chunk((n_groups - 1) * group + j)


def _gdn(gdn_in, small, vecs, batch, seq, group):
    n = batch * seq
    d = GDN_HEAD_DIM
    heads = N_GDN_HEADS
    n_chunks = seq // CHUNK
    kern = functools.partial(_gdn_kernel, seq=seq, heads=heads, group=group)
    col_block = lambda part: (lambda b: (b, part))
    const = lambda b: (0, 0)
    return pl.pallas_call(
        kern,
        grid=(batch,),
        in_specs=[
            pl.BlockSpec((seq, D_GDN), col_block(0)),
            pl.BlockSpec((seq, D_GDN), col_block(1)),
            pl.BlockSpec((seq, D_GDN), col_block(2)),
            pl.BlockSpec((seq, D_GDN), col_block(3)),
            pl.BlockSpec((seq, LANES), lambda b: (b, 0)),
            pl.BlockSpec(vecs.shape, const),
        ],
        out_specs=pl.BlockSpec((seq, D_GDN), lambda b: (b, 0)),
        out_shape=jax.ShapeDtypeStruct((n, D_GDN), BF16),
        scratch_shapes=[
            pltpu.VMEM((heads, n_chunks * d, d), BF16),
            pltpu.VMEM((heads, n_chunks * d, d), F32),
            pltpu.VMEM((heads, seq, d), BF16),
            pltpu.VMEM((heads, seq, d), F32),
            pltpu.VMEM((heads, n_chunks * 8, d), F32),
            pltpu.VMEM((heads, d, d), F32),
        ],
        compiler_params=pltpu.CompilerParams(
            dimension_semantics=("arbitrary",), vmem_limit_bytes=VMEM_LIMIT),
        name="gated_deltanet",
    )(gdn_in, gdn_in, gdn_in, gdn_in, small, vecs)


def _out_mlp_kernel(fox_ref, gdn_ref, x_ref, wo_ref, pmix_ref, pre_ref,
                    wup_ref, wdn_ref, post_ref, o_ref, *, ff_blk):
    mixed = _dot(fox_ref[...], wo_ref[:D_FOX, :]) + _dot(gdn_ref[...], wo_ref[D_FOX:, :])
    x1 = x_ref[...] + _rms(mixed, pmix_ref[...])
    h = _rms(x1, pre_ref[...]).astype(BF16)
    d_ff = wup_ref.shape[1]
    y = None
    for j in range(d_ff // ff_blk):
        a = _dot(h, wup_ref[:, j * ff_blk:(j + 1) * ff_blk])
        a = jnp.square(jnp.maximum(a, 0.0)).astype(BF16)
        t = _dot(a, wdn_ref[j * ff_blk:(j + 1) * ff_blk, :])
        y = t if y is None else y + t
    o_ref[...] = x1 + _rms(y, post_ref[...])


def _out_mlp(fox_o, gdn_o, x2, wo, pmix, pre, wup, wdn, post, tm):
    n, d = x2.shape
    const = lambda i: (0, 0)
    row = lambda i: (i, 0)
    single = pl.Buffered(1)
    kern = functools.partial(_out_mlp_kernel, ff_blk=1024)
    return pl.pallas_call(
        kern,
        grid=(n // tm,),
        in_specs=[
            pl.BlockSpec((tm, fox_o.shape[1]), row),
            pl.BlockSpec((tm, gdn_o.shape[1]), row),
            pl.BlockSpec((tm, d), row),
            pl.BlockSpec(wo.shape, const, pipeline_mode=single),
            pl.BlockSpec((1, d), const),
            pl.BlockSpec((1, d), const),
            pl.BlockSpec(wup.shape, const, pipeline_mode=single),
            pl.BlockSpec(wdn.shape, const, pipeline_mode=single),
            pl.BlockSpec((1, d), const),
        ],
        out_specs=pl.BlockSpec((tm, d), row),
        out_shape=jax.ShapeDtypeStruct((n, d), F32),
        compiler_params=pltpu.CompilerParams(
            dimension_semantics=("arbitrary",), vmem_limit_bytes=VMEM_LIMIT),
        name="out_mlp",
    )(fox_o, gdn_o, x2, wo, pmix, pre, wup, wdn, post)


def _vec_row(values):
    values = values.astype(F32)
    return jnp.concatenate([values, jnp.zeros((LANES - values.shape[0],), F32)])


def kernel(x, pre_mix_norm, w_in, fox_f_bias, fox_out_norm, gdn_conv_w, gdn_a_log, gdn_dt_bias,
           gdn_out_norm, w_out, post_mix_norm, pre_mlp_norm, w_up, w_down, post_mlp_norm):
    b, s, d = x.shape
    n = b * s
    x2 = x.reshape(n, d)


    zeros_b = jnp.zeros((SMALL_A0,), F32)
    vecs = jnp.stack([
        _vec_row(jnp.tile(fox_f_bias, FOX_REP)),
        _vec_row(jnp.tile(fox_out_norm, 2)),
        _vec_row(jnp.concatenate([zeros_b, gdn_a_log.astype(F32)])),
        _vec_row(jnp.concatenate([zeros_b, gdn_dt_bias.astype(F32)])),
        _vec_row(gdn_out_norm),
        jnp.zeros((LANES,), F32), jnp.zeros((LANES,), F32), jnp.zeros((LANES,), F32)])

    fox_qkv, gdn_in, small = _in_proj(x2, pre_mix_norm.reshape(1, d).astype(F32),
                                      w_in.astype(F32).T, gdn_conv_w.astype(F32), s, tm=1024)
    fox_o, (wo, wup, wdn) = _fox(
        fox_qkv, small, vecs, [w_out.astype(F32), w_up.astype(F32), w_down.astype(F32)],
        b, s, tq=1024, tk=512, dg=512)
    gdn_o = _gdn(gdn_in, small, vecs, b, s, group=4)
    out = _out_mlp(fox_o, gdn_o, x2, wo,
                   post_mix_norm.reshape(1, d).astype(F32), pre_mlp_norm.reshape(1, d).astype(F32),
                   wup, wdn, post_mlp_norm.reshape(1, d).astype(F32), tm=1024)
    return out.reshape(b, s, d)
```
